```python
import math
import jax, jax.numpy as jnp
from jax import lax
import numpy as np

D_MODEL = 1024
BATCH = 4
SEQ = 4096
DEPTH = 2
DEC_BATCH = 32
DEC_SEQ = 1
PAST_LEN = 16384
PAGE_SIZE = 128

N_EVEN = (DEPTH + 1) // 2
N_ODD = DEPTH // 2
RET_HEADS = 4
RET_DK = 128
RET_DV = 128
RET_WIDTH = RET_HEADS * RET_DV
RET_CHUNK = 128
POOL_WINDOWS = (2, 4, 8, 16)
POOL_GROUPS = 4
POOL_WIDTH = D_MODEL // 2
POOL_GW = POOL_WIDTH // POOL_GROUPS
POOL_KEEP = max(POOL_WINDOWS) - 1
DIFF_HEADS = 8
DIFF_D = 64
DIFF_HD = 2 * DIFF_D
DIFF_WIDTH = DIFF_HEADS * DIFF_HD
Q_BLOCK = 128
REL_BUCKETS = 32
REL_MAX_DIST = 128
MEM_LEN = 256
MEM_HEADS = 4
MEM_HD = D_MODEL // MEM_HEADS
D_FF = 2816
EPS = 1e-6
NEG_INF = -1e30

kernel_name = 'hybrid_retpool_diffattn_macaron_step'

f32 = jnp.float32


def rms_norm(x, g):
    xf = x.astype(f32)
    y = xf * lax.rsqrt(jnp.mean(xf * xf, axis=-1, keepdims=True) + EPS)
    return (y * g.astype(f32)).astype(x.dtype)


def swiglu(h, w_gate, w_up, w_down):
    return (jax.nn.silu(h @ w_gate) * (h @ w_up)) @ w_down


def rotary(x, pos):
    half = x.shape[-1] // 2
    inv = 1.0 / (10000.0 ** jnp.linspace(0.0, 1.0, half, dtype=f32))
    ang = pos.astype(f32)[:, None] * inv[None, :]
    cos = jnp.cos(ang)[None, :, None, :]
    sin = jnp.sin(ang)[None, :, None, :]
    xf = x.astype(f32)
    x1, x2 = xf[..., :half], xf[..., half:]
    return jnp.concatenate([x1 * cos - x2 * sin, x2 * cos + x1 * sin], axis=-1)


def retention(q, k, v, s0):
    B, L, H, _ = q.shape
    dv = v.shape[-1]
    c = RET_CHUNK if L % RET_CHUNK == 0 else L
    n = L // c
    log_g = jnp.log1p(-jnp.exp2(-5.0 - jnp.arange(H, dtype=f32)))
    i = jnp.arange(c)
    rel = i[:, None] - i[None, :]
    intra = jnp.where(rel >= 0, jnp.exp(log_g[:, None, None] * jnp.maximum(rel, 0).astype(f32)), 0.0)
    q_dec = jnp.exp(log_g[None, :] * (i[:, None] + 1).astype(f32))
    k_dec = jnp.exp(log_g[:, None] * (c - 1 - i)[None, :].astype(f32))
    c_dec = jnp.exp(log_g * c)

    def to_chunks(t):
        return t.reshape(B, n, c, H, t.shape[-1]).transpose(1, 0, 2, 3, 4)

    def step(state, blk):
        qb, kb, vb = blk
        att = jnp.einsum('bihd,bjhd->bhij', qb, kb) * intra[None]
        o = (jnp.einsum('bhij,bjhe->bihe', att, vb)
             + jnp.einsum('bihd,bhde->bihe', qb, state) * q_dec[None, :, :, None])
        state = state * c_dec[None, :, None, None] + jnp.einsum('bjhd,bjhe,hj->bhde', kb, vb, k_dec)
        return state, o

    s_new, o = lax.scan(step, s0, (to_chunks(q), to_chunks(k), to_chunks(v)))
    return o.transpose(1, 0, 2, 3, 4).reshape(B, L, H, dv), s_new


def pool_mix(p, prev_rows, pool_w, pool_scale, pos):
    B, L, _ = p.shape
    ext = jnp.concatenate([prev_rows.astype(p.dtype), p], axis=1)
    cs = jnp.cumsum(ext.astype(f32), axis=1)
    cs = jnp.concatenate([jnp.zeros((B, 1, POOL_WIDTH), f32), cs], axis=1)
    end = cs[:, POOL_KEEP + 1:POOL_KEEP + 1 + L]
    means = []
    for g, w in enumerate(POOL_WINDOWS):
        sl = slice(g * POOL_GW, (g + 1) * POOL_GW)
        win = end[..., sl] - cs[:, POOL_KEEP + 1 - w:POOL_KEEP + 1 - w + L, sl]
        cnt = jnp.minimum(w, pos + 1).astype(f32)[None, :, None]
        means.append(win / cnt)
    mean = jnp.stack(means, axis=2)
    pooled = mean - p.astype(f32).reshape(B, L, POOL_GROUPS, POOL_GW)
    mixed = jnp.einsum('blgc,gcd->blgd', pooled, pool_w.astype(f32)).reshape(B, L, POOL_WIDTH)
    mixed = mixed * pool_scale.astype(f32)
    return mixed.astype(p.dtype), ext[:, -POOL_KEEP:]


def even_mixer(h, w_in, ret_gain, pool_w, pool_scale, w_out, s0, prev_rows, pos0):
    B, L, _ = h.shape
    qk = RET_HEADS * RET_DK
    z = h @ w_in
    q = z[..., :qk].reshape(B, L, RET_HEADS, RET_DK)
    k = z[..., qk:2 * qk].reshape(B, L, RET_HEADS, RET_DK)
    v = z[..., 2 * qk:2 * qk + RET_WIDTH].reshape(B, L, RET_HEADS, RET_DV).astype(f32)
    g = z[..., 2 * qk + RET_WIDTH:2 * qk + 2 * RET_WIDTH]
    p = z[..., 2 * qk + 2 * RET_WIDTH:]
    pos = pos0 + jnp.arange(L, dtype=jnp.int32)
    q = rotary(q, pos)
    k = rotary(k, pos) * (RET_DK ** -0.5)
    o, s_new = retention(q, k, v, s0.astype(f32))
    mu = jnp.mean(o, axis=-1, keepdims=True)
    var = jnp.mean(jnp.square(o - mu), axis=-1, keepdims=True)
    o = ((o - mu) * lax.rsqrt(var + EPS)).reshape(B, L, RET_WIDTH) * ret_gain.astype(f32)
    ret_out = (jax.nn.silu(g.astype(f32)) * o).astype(h.dtype)
    pool_out, new_rows = pool_mix(p, prev_rows, pool_w, pool_scale, pos)
    y = jnp.concatenate([ret_out, pool_out], axis=-1) @ w_out
    return y, s_new.astype(h.dtype), new_rows


def rel_bucket(dist):
    n = jnp.maximum(dist, 0)
    max_exact = REL_BUCKETS // 2
    nf = jnp.maximum(n, 1).astype(f32)
    large = max_exact + (jnp.log(nf / max_exact) / math.log(REL_MAX_DIST / max_exact)
                         * (REL_BUCKETS - max_exact)).astype(jnp.int32)
    large = jnp.minimum(large, REL_BUCKETS - 1)
    return jnp.where(n < max_exact, n, large)


def diff_qkv(h, w_qkv):
    B, L, _ = h.shape
    z = (h @ w_qkv).reshape(B, L, 3, DIFF_HEADS, DIFF_HD)
    return z[:, :, 0], z[:, :, 1], z[:, :, 2]


def diff_attend(q, k, v, bias, mask, lam):
    scale = DIFF_D ** -0.5

    def logits(qq, kk):
        s = jnp.einsum('bqhd,bkhd->bhqk', qq, kk, preferred_element_type=f32) * scale + bias
        return jnp.where(mask, s, NEG_INF)

    a1 = jax.nn.softmax(logits(q[..., :DIFF_D], k[..., :DIFF_D]), axis=-1)
    a2 = jax.nn.softmax(logits(q[..., DIFF_D:], k[..., DIFF_D:]), axis=-1)
    a = a1 - lam * a2
    return jnp.einsum('bhqk,bkhe->bqhe', a.astype(v.dtype), v)


def diff_prompt(q, k, v, rel_bias, lam):
    B, S = q.shape[:2]
    nb = S // Q_BLOCK
    k_pos = jnp.arange(S, dtype=jnp.int32)
    q_blocks = q.reshape(B, nb, Q_BLOCK, DIFF_HEADS, DIFF_HD).transpose(1, 0, 2, 3, 4)

    def block(args):
        bi, qb = args
        q_pos = bi * Q_BLOCK + jnp.arange(Q_BLOCK, dtype=jnp.int32)
        dist = q_pos[:, None] - k_pos[None, :]
        bias = rel_bias[rel_bucket(dist)].astype(f32).transpose(2, 0, 1)
        return diff_attend(qb, k, v, bias, dist >= 0, lam)

    o = lax.map(block, (jnp.arange(nb, dtype=jnp.int32), q_blocks))
    return o.transpose(1, 0, 2, 3, 4).reshape(B, S, DIFF_HEADS, DIFF_HD)


def diff_post(o, gain, lam_init, w_out):
    B, L = o.shape[:2]
    y = rms_norm(o, gain) * (1.0 - lam_init)
    return y.reshape(B, L, DIFF_WIDTH) @ w_out


def mem_kv(mem, w_ckv):
    B, M, _ = mem.shape
    z = (mem @ w_ckv).reshape(B, M, 2, MEM_HEADS, MEM_HD)
    return z[:, :, 0], z[:, :, 1]


def cross_attn(h, mk, mv, w_cq, w_co):
    B, L, _ = h.shape
    q = (h @ w_cq).reshape(B, L, MEM_HEADS, MEM_HD)
    s = jnp.einsum('blhd,bmhd->bhlm', q, mk, preferred_element_type=f32) * (MEM_HD ** -0.5)
    a = jax.nn.softmax(s, axis=-1).astype(mv.dtype)
    return jnp.einsum('bhlm,bmhd->blhd', a, mv).reshape(B, L, D_MODEL) @ w_co


def setup_inputs(seed: int = 0) -> dict:
    key = jax.random.key(seed)
    ks = jax.random.split(key, 32)

    def nrm(k, shape, scale=1.0):
        return jax.random.normal(k, shape, f32) * scale

    n_pages = PAST_LEN // PAGE_SIZE
    n_pool = (DEC_BATCH * n_pages * 5) // 4
    perm = jax.random.permutation(ks[6], n_pool)
    page_table = perm[:DEC_BATCH * n_pages].reshape(DEC_BATCH, n_pages).astype(jnp.int32)
    qk = RET_HEADS * RET_DK
    even_in = 2 * qk + 2 * RET_WIDTH + POOL_WIDTH
    mix_even = RET_WIDTH + POOL_WIDTH
    return {
        'x_prompt': nrm(ks[0], (BATCH, SEQ, D_MODEL)),
        'x_sample': nrm(ks[1], (DEC_BATCH, DEC_SEQ, D_MODEL)),
        'state_ret': nrm(ks[2], (N_EVEN, DEC_BATCH, RET_HEADS, RET_DK, RET_DV)),
        'state_pool': nrm(ks[3], (N_EVEN, DEC_BATCH, POOL_KEEP, POOL_WIDTH)),
        'cache_k_diff': nrm(ks[4], (N_ODD, n_pool, PAGE_SIZE, DIFF_HEADS, DIFF_HD)),
        'cache_v_diff': nrm(ks[5], (N_ODD, n_pool, PAGE_SIZE, DIFF_HEADS, DIFF_HD)),
        'cache_mem_k': nrm(ks[7], (DEPTH, DEC_BATCH, MEM_LEN, MEM_HEADS, MEM_HD)),
        'cache_mem_v': nrm(ks[8], (DEPTH, DEC_BATCH, MEM_LEN, MEM_HEADS, MEM_HD)),
        'page_table': page_table,
        'mem_prompt': nrm(ks[9], (BATCH, MEM_LEN, D_MODEL)),
        'norms': 1.0 + 0.02 * nrm(ks[10], (DEPTH, 4, D_MODEL)),
        'final_norm': 1.0 + 0.02 * nrm(ks[11], (D_MODEL,)),
        'ffn_w_gate': nrm(ks[12], (DEPTH, 2, D_MODEL, D_FF), D_MODEL ** -0.5),
        'ffn_w_up': nrm(ks[13], (DEPTH, 2, D_MODEL, D_FF), D_MODEL ** -0.5),
        'ffn_w_down': nrm(ks[14], (DEPTH, 2, D_FF, D_MODEL), D_FF ** -0.5),
        'w_in_even': nrm(ks[15], (N_EVEN, D_MODEL, even_in), D_MODEL ** -0.5),
        'ret_gain': 1.0 + 0.02 * nrm(ks[16], (N_EVEN, RET_WIDTH)),
        'pool_w': nrm(ks[17], (N_EVEN, POOL_GROUPS, POOL_GW, POOL_GW), POOL_GW ** -0.5),
        'pool_scale': 1.0 + 0.02 * nrm(ks[18], (N_EVEN, POOL_WIDTH)),
        'w_out_even': nrm(ks[19], (N_EVEN, mix_even, D_MODEL), mix_even ** -0.5),
        'w_qkv_odd': nrm(ks[20], (N_ODD, D_MODEL, 3 * DIFF_WIDTH), D_MODEL ** -0.5),
        'lambda_q1': 0.1 * nrm(ks[21], (N_ODD, DIFF_D)),
        'lambda_k1': 0.1 * nrm(ks[22], (N_ODD, DIFF_D)),
        'lambda_q2': 0.1 * nrm(ks[23], (N_ODD, DIFF_D)),
        'lambda_k2': 0.1 * nrm(ks[24], (N_ODD, DIFF_D)),
        'subln_gain': 1.0 + 0.02 * nrm(ks[25], (N_ODD, DIFF_HD)),
        'w_out_odd': nrm(ks[26], (N_ODD, DIFF_WIDTH, D_MODEL), DIFF_WIDTH ** -0.5),
        'rel_bias': 0.5 * nrm(ks[27], (REL_BUCKETS, DIFF_HEADS)),
        'w_cq': nrm(ks[28], (DEPTH, D_MODEL, D_MODEL), D_MODEL ** -0.5),
        'w_ckv': nrm(ks[29], (DEPTH, D_MODEL, 2 * D_MODEL), D_MODEL ** -0.5),
        'w_co': nrm(ks[30], (DEPTH, D_MODEL, D_MODEL), D_MODEL ** -0.5),
    }


def reference(x_prompt, x_sample, state_ret, state_pool, cache_k_diff, cache_v_diff, cache_mem_k, cache_mem_v,
              page_table, mem_prompt, norms, final_norm, ffn_w_gate, ffn_w_up, ffn_w_down, w_in_even, ret_gain,
              pool_w, pool_scale, w_out_even, w_qkv_odd, lambda_q1, lambda_k1, lambda_q2, lambda_k2, subln_gain,
              w_out_odd, rel_bias, w_cq, w_ckv, w_co):
    xp, xs = x_prompt, x_sample
    B = xp.shape[0]
    Ld = xs.shape[1]
    past = page_table.shape[1] * PAGE_SIZE
    q_pos_s = past + jnp.arange(Ld, dtype=jnp.int32)
    k_pos_s = jnp.arange(past + Ld, dtype=jnp.int32)
    dist_s = q_pos_s[:, None] - k_pos_s[None, :]
    bias_s = rel_bias[rel_bucket(dist_s)].astype(f32).transpose(2, 0, 1)
    mask_s = dist_s >= 0

    p_ret, p_pool, p_k, p_v, p_mk, p_mv = [], [], [], [], [], []
    s_ret, s_pool, s_k, s_v = [], [], [], []
    for l in range(DEPTH):
        xp = xp + 0.5 * swiglu(rms_norm(xp, norms[l, 0]), ffn_w_gate[l, 0], ffn_w_up[l, 0], ffn_w_down[l, 0])
        xs = xs + 0.5 * swiglu(rms_norm(xs, norms[l, 0]), ffn_w_gate[l, 0], ffn_w_up[l, 0], ffn_w_down[l, 0])
        hp = rms_norm(xp, norms[l, 1])
        hs = rms_norm(xs, norms[l, 1])
        if l % 2 == 0:
            e = l // 2
            s0p = jnp.zeros((B, RET_HEADS, RET_DK, RET_DV), f32)
            rows0 = jnp.zeros((B, POOL_KEEP, POOL_WIDTH), hp.dtype)
            mp, rp, pp = even_mixer(hp, w_in_even[e], ret_gain[e], pool_w[e], pool_scale[e], w_out_even[e],
                                    s0p, rows0, 0)
            ms, rs, ps = even_mixer(hs, w_in_even[e], ret_gain[e], pool_w[e], pool_scale[e], w_out_even[e],
                                    state_ret[e], state_pool[e], past)
            p_ret.append(rp)
            p_pool.append(pp)
            s_ret.append(rs)
            s_pool.append(ps)
        else:
            o = l // 2
            lam_init = 0.8 - 0.6 * math.exp(-0.3 * l)
            lam = (jnp.exp(jnp.sum(lambda_q1[o].astype(f32) * lambda_k1[o].astype(f32)))
                   - jnp.exp(jnp.sum(lambda_q2[o].astype(f32) * lambda_k2[o].astype(f32))) + lam_init)
            qp, kp, vp = diff_qkv(hp, w_qkv_odd[o])
            mp = diff_post(diff_prompt(qp, kp, vp, rel_bias, lam), subln_gain[o], lam_init, w_out_odd[o])
            qs, ks_, vs = diff_qkv(hs, w_qkv_odd[o])

            def per_seq(args):
                qq, kk, vv, pages = args
                kpast = cache_k_diff[o, pages].reshape(-1, DIFF_HEADS, DIFF_HD).astype(kk.dtype)
                vpast = cache_v_diff[o, pages].reshape(-1, DIFF_HEADS, DIFF_HD).astype(vv.dtype)
                k_all = jnp.concatenate([kpast, kk], axis=0)[None]
                v_all = jnp.concatenate([vpast, vv], axis=0)[None]
                return diff_attend(qq[None], k_all, v_all, bias_s, mask_s, lam)[0]

            att_s = lax.map(per_seq, (qs, ks_, vs, page_table))
            ms = diff_post(att_s, subln_gain[o], lam_init, w_out_odd[o])
            p_k.append(kp)
            p_v.append(vp)
            s_k.append(ks_)
            s_v.append(vs)
        xp = xp + mp
        xs = xs + ms
        mkp, mvp = mem_kv(mem_prompt, w_ckv[l])
        p_mk.append(mkp)
        p_mv.append(mvp)
        xp = xp + cross_attn(rms_norm(xp, norms[l, 2]), mkp, mvp, w_cq[l], w_co[l])
        xs = xs + cross_attn(rms_norm(xs, norms[l, 2]), cache_mem_k[l], cache_mem_v[l], w_cq[l], w_co[l])
        xp = xp + 0.5 * swiglu(rms_norm(xp, norms[l, 3]), ffn_w_gate[l, 1], ffn_w_up[l, 1], ffn_w_down[l, 1])
        xs = xs + 0.5 * swiglu(rms_norm(xs, norms[l, 3]), ffn_w_gate[l, 1], ffn_w_up[l, 1], ffn_w_down[l, 1])

    y_prompt = rms_norm(xp, final_norm)
    y_sample = rms_norm(xs, final_norm)
    return (y_prompt, y_sample,
            jnp.stack(p_ret), jnp.stack(p_pool), jnp.stack(p_k), jnp.stack(p_v), jnp.stack(p_mk), jnp.stack(p_mv),
            jnp.stack(s_ret), jnp.stack(s_pool), jnp.stack(s_k), jnp.stack(s_v))
```

```python
import functools
import math

import jax
import jax.numpy as jnp
from jax import lax
from jax.experimental import pallas as pl
from jax.experimental.pallas import tpu as pltpu

f32 = jnp.float32
bf16 = jnp.bfloat16
i32 = jnp.int32

EPS = 1e-6
NEG_INF = -1e30

RET_HEADS = 4
RET_DK = 128
POOL_WINDOWS = (2, 4, 8, 16)
POOL_KEEP = max(POOL_WINDOWS) - 1
DIFF_HEADS = 8
DIFF_D = 64
DIFF_HD = 2 * DIFF_D
REL_BUCKETS = 32
REL_MAX_DIST = 128
MEM_HEADS = 4
PAGE_SIZE = 128

VMEM_LIMIT_BYTES = 56 * 1024 * 1024
LANES = 128
SUBLANES = 8

ROW_TILE = 512
ATT_TILE = 512
PAGES_PER_STEP = 8


def _params(*sem):
    return pltpu.CompilerParams(dimension_semantics=sem, vmem_limit_bytes=VMEM_LIMIT_BYTES)


def _rms(x, g):
    return x * lax.rsqrt(jnp.mean(x * x, axis=-1, keepdims=True) + EPS) * g


def _dot(a, b):
    return jnp.dot(a, b, preferred_element_type=f32)


def _dot_nt(a, b):
    return lax.dot_general(a, b, (((1,), (1,)), ((), ())), preferred_element_type=f32)


def _dot_tn(a, b):
    return lax.dot_general(a, b, (((0,), (0,)), ((), ())), preferred_element_type=f32)


def _resident(shape):
    nd = len(shape)
    return pl.BlockSpec(shape, lambda *_: (0,) * nd, pipeline_mode=pl.Buffered(1))


def _ffn_kernel(x_ref, g_ref, wg_ref, wu_ref, wd_ref, *rest, fchunk, final):
    o_ref = rest[-1]
    x = x_ref[...]
    h = _rms(x, g_ref[...]).astype(bf16)
    acc = jnp.zeros(x.shape, f32)
    for c0 in range(0, wg_ref.shape[1], fchunk):
        gate = _dot(h, wg_ref[:, c0:c0 + fchunk])
        up = _dot(h, wu_ref[:, c0:c0 + fchunk])
        a = (jax.nn.silu(gate) * up).astype(bf16)
        acc = acc + _dot(a, wd_ref[c0:c0 + fchunk, :])
    y = x + 0.5 * acc
    if final:
        y = _rms(y, rest[0][...])
    o_ref[...] = y


def _ffn(x, g, wg, wu, wd, final_g=None):
    m, d = x.shape
    f = wg.shape[1]
    tm = min(ROW_TILE, m)
    fchunk = f // 2 if (f // 2) % LANES == 0 else f
    final = final_g is not None
    in_specs = [pl.BlockSpec((tm, d), lambda i: (i, 0)), _resident((1, d)),
                _resident((d, f)), _resident((d, f)), _resident((f, d))]
    args = [x, g.reshape(1, d), wg, wu, wd]
    if final:
        in_specs.append(_resident((1, d)))
        args.append(final_g.reshape(1, d))
    return pl.pallas_call(
        functools.partial(_ffn_kernel, fchunk=fchunk, final=final),
        grid=(m // tm,),
        in_specs=in_specs,
        out_specs=pl.BlockSpec((tm, d), lambda i: (i, 0)),
        out_shape=jax.ShapeDtypeStruct((m, d), f32),
        compiler_params=_params("arbitrary"),
        name="ffn",
    )(*args)


def _proj_kernel(*refs, norm, groups, nchunk):
    x_ref = refs[0]
    g_ref = refs[1] if norm else None
    w_ref = refs[2] if norm else refs[1]
    o_refs = refs[(3 if norm else 2):]
    x = x_ref[...]
    h = (_rms(x, g_ref[...]) if norm else x).astype(bf16)
    for off, width, targets in groups:
        for c0 in range(0, width, nchunk):
            n = min(nchunk, width - c0)
            r = _dot(h, w_ref[:, off + c0:off + c0 + n])
            for oi, scale in targets:
                o_refs[oi][:, c0:c0 + n] = (r if scale == 1.0 else r * scale).astype(o_refs[oi].dtype)


def _proj(x, g, w, groups, outs):
    m, d = x.shape
    n = w.shape[1]
    tm = min(ROW_TILE, m)
    norm = g is not None
    in_specs = [pl.BlockSpec((tm, d), lambda i: (i, 0))]
    args = [x]
    if norm:
        in_specs.append(_resident((1, d)))
        args.append(g.reshape(1, d))
    in_specs.append(_resident((d, n)))
    args.append(w)
    return pl.pallas_call(
        functools.partial(_proj_kernel, norm=norm, groups=groups, nchunk=512),
        grid=(m // tm,),
        in_specs=in_specs,
        out_specs=[pl.BlockSpec((tm, wd), lambda i: (i, 0)) for wd, _ in outs],
        out_shape=[jax.ShapeDtypeStruct((m, wd), dt) for wd, dt in outs],
        compiler_params=_params("arbitrary"),
        name="proj",
    )(*args)


def _mmres_kernel(x_ref, a_ref, w_ref, o_ref):
    o_ref[...] = x_ref[...] + _dot(a_ref[...].astype(bf16), w_ref[...])


def _mmres(x, a, w):
    m, d = x.shape
    k = a.shape[1]
    tm = min(ROW_TILE, m)
    return pl.pallas_call(
        _mmres_kernel,
        grid=(m // tm,),
        in_specs=[pl.BlockSpec((tm, d), lambda i: (i, 0)), pl.BlockSpec((tm, k), lambda i: (i, 0)),
                  _resident((k, d))],
        out_specs=pl.BlockSpec((tm, d), lambda i: (i, 0)),
        out_shape=jax.ShapeDtypeStruct((m, d), f32),
        compiler_params=_params("arbitrary"),
        name="mmres",
    )(x, a, w)


def _ret_log_gamma(h):
    return math.log1p(-(2.0 ** (-5.0 - h)))


def _rotate(x, cosf, sinf):
    return x * cosf + pltpu.roll(x, x.shape[-1] // 2, axis=1) * sinf


def _group_norm_gate(o, gain, gate):
    mu = jnp.mean(o, axis=-1, keepdims=True)
    var = jnp.mean(jnp.square(o - mu), axis=-1, keepdims=True)
    return jax.nn.silu(gate) * ((o - mu) * lax.rsqrt(var + EPS) * gain)


def _even_prompt_kernel(zq_ref, zk_ref, zv_ref, zg_ref, zp_ref, cos_ref, sin_ref, gain_ref, pw_ref, ps_ref,
                        mix_ref, sret_ref, spool_ref, state_ref, ext_ref, intra_ref, *, t):
    b = pl.program_id(0)
    c = pl.program_id(1)
    nc = pl.num_programs(1)
    halo = POOL_KEEP + 1
    dk = RET_DK

    @pl.when(jnp.logical_and(b == 0, c == 0))
    def _():
        rel = lax.broadcasted_iota(i32, (t, t), 0) - lax.broadcasted_iota(i32, (t, t), 1)
        relf = jnp.maximum(rel, 0).astype(f32)
        for h in range(RET_HEADS):
            intra_ref[h] = jnp.where(rel >= 0, jnp.exp(_ret_log_gamma(h) * relf), 0.0)

    @pl.when(c == 0)
    def _():
        state_ref[...] = jnp.zeros(state_ref.shape, f32)
        ext_ref[0:halo, :] = jnp.zeros((halo, ext_ref.shape[1]), f32)

    cosf = cos_ref[...]
    sinf = sin_ref[...]
    row = lax.broadcasted_iota(i32, (t, 1), 0).astype(f32)
    for h in range(RET_HEADS):
        sl = slice(h * dk, (h + 1) * dk)
        lg = _ret_log_gamma(h)
        qr = _rotate(zq_ref[:, sl], cosf, sinf)
        kr = _rotate(zk_ref[:, sl], cosf, sinf) * (dk ** -0.5)
        qb = qr.astype(bf16)
        vb = zv_ref[:, sl].astype(bf16)
        att = _dot_nt(qb, kr.astype(bf16)) * intra_ref[h]
        s_old = state_ref[h]
        o = _dot(att.astype(bf16), vb) + _dot(qb, s_old.astype(bf16)) * jnp.exp(lg * (row + 1.0))
        k_dec = jnp.exp(lg * (float(t - 1) - row))
        state_ref[h] = s_old * math.exp(lg * t) + _dot_tn((kr * k_dec).astype(bf16), vb)
        mix_ref[:, sl] = _group_norm_gate(o, gain_ref[:, sl], zg_ref[:, sl]).astype(mix_ref.dtype)

    p = zp_ref[...]
    ext_ref[halo:halo + t, :] = p
    pos = c * t + lax.broadcasted_iota(i32, (t, 1), 0)
    gw = p.shape[1] // len(POOL_WINDOWS)
    ret_w = RET_HEADS * dk
    for gi, w in enumerate(POOL_WINDOWS):
        sl = slice(gi * gw, (gi + 1) * gw)
        win = p[:, sl]
        for back in range(1, w):
            win = win + ext_ref[halo - back:halo - back + t, sl]
        cnt = jnp.minimum(w, pos + 1).astype(f32)
        pooled = win / cnt - p[:, sl]
        mixed = _dot(pooled.astype(bf16), pw_ref[gi]) * ps_ref[:, sl]
        mix_ref[:, ret_w + gi * gw:ret_w + (gi + 1) * gw] = mixed.astype(mix_ref.dtype)

    @pl.when(c == nc - 1)
    def _():
        sret_ref[0, 0] = state_ref[...]
        spool_ref[0, 0] = ext_ref[t + 1:t + halo, :]

    ext_ref[0:halo, :] = ext_ref[t:t + halo, :]


def _even_prompt(z, cosf, sinf, gain, pw, ps, batch, seq):
    t = ROW_TILE
    nc = seq // t
    rw = RET_HEADS * RET_DK
    pwid = ps.shape[0]
    zspec = lambda j: pl.BlockSpec((t, rw), lambda b, c: (b * nc + c, j))
    return pl.pallas_call(
        functools.partial(_even_prompt_kernel, t=t),
        grid=(batch, nc),
        in_specs=[zspec(0), zspec(1), zspec(2), zspec(3), zspec(4),
                  pl.BlockSpec((t, RET_DK), lambda b, c: (c, 0)), pl.BlockSpec((t, RET_DK), lambda b, c: (c, 0)),
                  _resident((1, rw)), _resident(pw.shape), _resident((1, pwid))],
        out_specs=[pl.BlockSpec((t, rw + pwid), lambda b, c: (b * nc + c, 0)),
                   pl.BlockSpec((1, 1, RET_HEADS, RET_DK, RET_DK), lambda b, c: (0, b, 0, 0, 0)),
                   pl.BlockSpec((1, 1, POOL_KEEP, pwid), lambda b, c: (0, b, 0, 0))],
        out_shape=[jax.ShapeDtypeStruct((batch * seq, rw + pwid), bf16),
                   jax.ShapeDtypeStruct((1, batch, RET_HEADS, RET_DK, RET_DK), f32),
                   jax.ShapeDtypeStruct((1, batch, POOL_KEEP, pwid), f32)],
        scratch_shapes=[pltpu.VMEM((RET_HEADS, RET_DK, RET_DK), f32),
                        pltpu.VMEM((POOL_KEEP + 1 + t, pwid), f32),
                        pltpu.VMEM((RET_HEADS, t, t), f32)],
        compiler_params=_params("arbitrary", "arbitrary"),
        name="even_prompt",
    )(z, z, z, z, z, cosf, sinf, gain.reshape(1, rw), pw, ps.reshape(1, pwid))


def _even_sample_kernel(z_ref, cos_ref, sin_ref, s0_ref, rows_ref, gain_ref, pw_ref, ps_ref,
                        mix_ref, sret_ref, spool_ref, ext_ref, *, pos):
    dk = RET_DK
    rw = RET_HEADS * dk
    z = jnp.broadcast_to(z_ref[0], (SUBLANES, z_ref.shape[2]))
    cosf = jnp.broadcast_to(cos_ref[...], (SUBLANES, dk))
    sinf = jnp.broadcast_to(sin_ref[...], (SUBLANES, dk))
    first = lax.broadcasted_iota(i32, (SUBLANES, dk), 0) == 0
    for h in range(RET_HEADS):
        sl = slice(h * dk, (h + 1) * dk)
        gamma = math.exp(_ret_log_gamma(h))
        qr = _rotate(z[:, sl], cosf, sinf)
        kr = _rotate(z[:, rw + h * dk:rw + (h + 1) * dk], cosf, sinf) * (dk ** -0.5)
        v = z[:, 2 * rw + h * dk:2 * rw + (h + 1) * dk]
        s_new = s0_ref[0, 0, h] * gamma + _dot_tn(jnp.where(first, kr, 0.0), v)
        sret_ref[0, 0, h] = s_new
        o = _dot(qr, s_new)
        gate = z[:, 3 * rw + h * dk:3 * rw + (h + 1) * dk]
        mix_ref[0, :, sl] = _group_norm_gate(o, gain_ref[:, sl], gate)[0:1]

    keep = POOL_KEEP
    pwid = ps_ref.shape[1]
    p = z[0:1, 4 * rw:4 * rw + pwid]
    ext_ref[0:keep, :] = rows_ref[0, 0]
    ext_ref[keep:keep + 1, :] = p
    spool_ref[0, 0] = ext_ref[1:keep + 1, :]
    gw = pwid // len(POOL_WINDOWS)
    for gi, w in enumerate(POOL_WINDOWS):
        sl = slice(gi * gw, (gi + 1) * gw)
        win = jnp.sum(ext_ref[keep + 1 - w:keep + 1, sl], axis=0, keepdims=True)
        pooled = win / float(min(w, pos + 1)) - p[:, sl]
        mixed = _dot(jnp.broadcast_to(pooled, (SUBLANES, gw)), pw_ref[gi])[0:1] * ps_ref[:, sl]
        mix_ref[0, :, rw + gi * gw:rw + (gi + 1) * gw] = mixed


def _even_sample(z, cosf, sinf, state_ret, state_pool, gain, pw, ps, pos):
    nb, zw = z.shape
    rw = RET_HEADS * RET_DK
    pwid = ps.shape[0]
    return pl.pallas_call(
        functools.partial(_even_sample_kernel, pos=pos),
        grid=(nb,),
        in_specs=[pl.BlockSpec((1, 1, zw), lambda b: (b, 0, 0)),
                  _resident((1, RET_DK)), _resident((1, RET_DK)),
                  pl.BlockSpec((1, 1, RET_HEADS, RET_DK, RET_DK), lambda b: (0, b, 0, 0, 0)),
                  pl.BlockSpec((1, 1, POOL_KEEP, pwid), lambda b: (0, b, 0, 0)),
                  _resident((1, rw)), _resident(pw.shape), _resident((1, pwid))],
        out_specs=[pl.BlockSpec((1, 1, rw + pwid), lambda b: (b, 0, 0)),
                   pl.BlockSpec((1, 1, RET_HEADS, RET_DK, RET_DK), lambda b: (0, b, 0, 0, 0)),
                   pl.BlockSpec((1, 1, POOL_KEEP, pwid), lambda b: (0, b, 0, 0))],
        out_shape=[jax.ShapeDtypeStruct((nb, 1, rw + pwid), f32),
                   jax.ShapeDtypeStruct((1, nb, RET_HEADS, RET_DK, RET_DK), f32),
                   jax.ShapeDtypeStruct((1, nb, POOL_KEEP, pwid), f32)],
        scratch_shapes=[pltpu.VMEM((POOL_KEEP + 1, pwid), f32)],
        compiler_params=_params("arbitrary"),
        name="even_sample",
    )(z.reshape(nb, 1, zw), cosf, sinf, state_ret, state_pool, gain.reshape(1, rw), pw, ps.reshape(1, pwid))


def _rel_bias_of(dist, rb_ref, h):
    n = jnp.maximum(dist, 0)
    max_exact = REL_BUCKETS // 2
    nf = jnp.maximum(n, 1).astype(f32)
    large = max_exact + (jnp.log(nf / max_exact) / math.log(REL_MAX_DIST / max_exact)
                         * (REL_BUCKETS - max_exact)).astype(i32)
    large = jnp.minimum(large, REL_BUCKETS - 1)
    bucket = jnp.where(n < max_exact, n, large)
    out = jnp.zeros(dist.shape, f32)
    for bk in range(REL_BUCKETS):
        out = jnp.where(bucket == bk, rb_ref[bk, h], out)
    return out


def _bias_kernel(rb_ref, lq1_ref, lk1_ref, lq2_ref, lk2_ref, tile_ref, tab_ref, lam_ref, *, t, lam_init):
    h = pl.program_id(0)
    rel = lax.broadcasted_iota(i32, (t, t), 0) - lax.broadcasted_iota(i32, (t, t), 1)
    diag = jnp.where(rel >= 0, _rel_bias_of(rel, rb_ref, h), NEG_INF)
    below = _rel_bias_of(rel + t, rb_ref, h)
    tile_ref[0, 0, 0:t, :] = diag
    tile_ref[0, 0, t:2 * t, :] = diag
    tile_ref[0, 1, 0:t, :] = below
    tile_ref[0, 1, t:2 * t, :] = below
    tab_ref[0] = _rel_bias_of(lax.broadcasted_iota(i32, (1, 2 * REL_MAX_DIST), 1), rb_ref, h)
    lam = (jnp.exp(jnp.sum(lq1_ref[...] * lk1_ref[...], axis=-1, keepdims=True))
           - jnp.exp(jnp.sum(lq2_ref[...] * lk2_ref[...], axis=-1, keepdims=True)) + lam_init)
    lam_ref[...] = jnp.broadcast_to(lam, lam_ref.shape)


def _bias_tables(rel_bias, lq1, lk1, lq2, lk2, lam_init, t):
    nh = rel_bias.shape[1]
    d = lq1.shape[0]
    vec = lambda a: a.reshape(1, d)
    return pl.pallas_call(
        functools.partial(_bias_kernel, t=t, lam_init=lam_init),
        grid=(nh,),
        in_specs=[pl.BlockSpec(memory_space=pltpu.SMEM)] + [_resident((1, d))] * 4,
        out_specs=[pl.BlockSpec((1, 2, 2 * t, t), lambda h: (h, 0, 0, 0)),
                   pl.BlockSpec((1, 1, 2 * REL_MAX_DIST), lambda h: (h, 0, 0)),
                   pl.BlockSpec((SUBLANES, LANES), lambda h: (0, 0))],
        out_shape=[jax.ShapeDtypeStruct((nh, 2, 2 * t, t), f32),
                   jax.ShapeDtypeStruct((nh, 1, 2 * REL_MAX_DIST), f32),
                   jax.ShapeDtypeStruct((SUBLANES, LANES), f32)],
        compiler_params=_params("arbitrary"),
        name="bias_tables",
    )(rel_bias, vec(lq1), vec(lk1), vec(lq2), vec(lk2))


def _sub_ln(o, gain, lam_init):
    return o * lax.rsqrt(jnp.mean(o * o, axis=-1, keepdims=True) + EPS) * gain * (1.0 - lam_init)


def _attn_prompt_kernel(lam_ref, far_ref, q_ref, k_ref, v_ref, tile_ref, gain_ref, o_ref,
                        qq_ref, m_ref, l_ref, acc_ref, *, t, lam_init):
    h = pl.program_id(1)
    qi = pl.program_id(2)
    q = q_ref[...]
    lane = lax.broadcasted_iota(i32, q.shape, 1)
    zero = jnp.zeros_like(q)
    qq_ref[0:t, :] = jnp.where(lane < DIFF_D, q, zero)
    qq_ref[t:2 * t, :] = jnp.where(lane >= DIFF_D, q, zero)
    m_ref[...] = jnp.full(m_ref.shape, NEG_INF, f32)
    l_ref[...] = jnp.zeros(l_ref.shape, f32)
    acc_ref[...] = jnp.zeros(acc_ref.shape, f32)

    def step(k0, bias):
        kt = k_ref[pl.ds(k0, t), :]
        vt = v_ref[pl.ds(k0, t), :]
        s = _dot_nt(qq_ref[...], kt) + bias
        m_old = m_ref[...]
        m_new = jnp.maximum(m_old, jnp.max(s, axis=-1, keepdims=True))
        alpha = jnp.exp(m_old - m_new)
        p = jnp.exp(s - m_new)
        l_ref[...] = alpha * l_ref[...] + jnp.sum(p, axis=-1, keepdims=True)
        acc_ref[...] = alpha * acc_ref[...] + _dot(p.astype(bf16), vt)
        m_ref[...] = m_new

    far = far_ref[h]

    def far_body(ki, carry):
        step(pl.multiple_of(ki * t, t), far)
        return carry

    lax.fori_loop(0, jnp.maximum(qi - 1, 0), far_body, 0)

    @pl.when(qi >= 1)
    def _():
        step(pl.multiple_of((qi - 1) * t, t), tile_ref[0, 1])

    step(pl.multiple_of(qi * t, t), tile_ref[0, 0])

    o = acc_ref[...] / l_ref[...]
    o = o[0:t] - lam_ref[0] * o[t:2 * t]
    o_ref[...] = _sub_ln(o, gain_ref[...], lam_init).astype(o_ref.dtype)


def _attn_prompt(q, k, v, tiles, far, lam, gain, batch, seq, lam_init):
    t = ATT_TILE
    nq = seq // t
    hd = DIFF_HD
    smem = pl.BlockSpec(memory_space=pltpu.SMEM)
    return pl.pallas_call(
        functools.partial(_attn_prompt_kernel, t=t, lam_init=lam_init),
        grid=(batch, DIFF_HEADS, nq),
        in_specs=[smem, smem,
                  pl.BlockSpec((t, hd), lambda b, h, i: (b * nq + i, h)),
                  pl.BlockSpec((seq, hd), lambda b, h, i: (b, h)),
                  pl.BlockSpec((seq, hd), lambda b, h, i: (b, h)),
                  pl.BlockSpec((1, 2, 2 * t, t), lambda b, h, i: (h, 0, 0, 0)),
                  _resident((1, hd))],
        out_specs=pl.BlockSpec((t, hd), lambda b, h, i: (b * nq + i, h)),
        out_shape=jax.ShapeDtypeStruct((batch * seq, DIFF_HEADS * hd), bf16),
        scratch_shapes=[pltpu.VMEM((2 * t, hd), bf16), pltpu.VMEM((2 * t, 1), f32),
                        pltpu.VMEM((2 * t, 1), f32), pltpu.VMEM((2 * t, hd), f32)],
        compiler_params=_params("arbitrary", "arbitrary", "arbitrary"),
        name="attn_prompt",
    )(lam, far, q, k, v, tiles, gain.reshape(1, hd))


def _attn_decode_kernel(pt_ref, lam_ref, q_ref, kn_ref, vn_ref, bias_ref, b0_ref, gain_ref, *rest,
                        npg, lam_init):
    k_refs = rest[:npg]
    v_refs = rest[npg:2 * npg]
    o_ref = rest[2 * npg]
    qm_ref, kbuf_ref, vbuf_ref, m_ref, l_ref, acc_ref = rest[2 * npg + 1:]
    c = pl.program_id(1)
    nc = pl.num_programs(1)
    nrow = 2 * DIFF_HEADS
    width = q_ref.shape[2]

    @pl.when(c == 0)
    def _():
        rowi = lax.broadcasted_iota(i32, (nrow, width), 0)
        coli = lax.broadcasted_iota(i32, (nrow, width), 1)
        sel = (coli // DIFF_D) == rowi
        qm = jnp.where(sel, jnp.broadcast_to(q_ref[0], (nrow, width)), 0.0).astype(bf16)
        qm_ref[...] = qm
        kn = jnp.broadcast_to(kn_ref[0], (nrow, width)).astype(bf16)
        s_new = jnp.sum(qm.astype(f32) * kn.astype(f32), axis=-1, keepdims=True) + b0_ref[...]
        m_ref[...] = s_new
        l_ref[...] = jnp.ones(l_ref.shape, f32)
        acc_ref[...] = jnp.broadcast_to(vn_ref[0], (nrow, width)).astype(bf16).astype(f32)

    for i in range(npg):
        kbuf_ref[i * PAGE_SIZE:(i + 1) * PAGE_SIZE, :] = k_refs[i][0].astype(bf16)
        vbuf_ref[i * PAGE_SIZE:(i + 1) * PAGE_SIZE, :] = v_refs[i][0].astype(bf16)

    s = _dot_nt(qm_ref[...], kbuf_ref[...]) + bias_ref[(c == nc - 1).astype(i32)]
    m_old = m_ref[...]
    m_new = jnp.maximum(m_old, jnp.max(s, axis=-1, keepdims=True))
    alpha = jnp.exp(m_old - m_new)
    p = jnp.exp(s - m_new)
    l_ref[...] = alpha * l_ref[...] + jnp.sum(p, axis=-1, keepdims=True)
    acc_ref[...] = alpha * acc_ref[...] + _dot(p.astype(bf16), vbuf_ref[...])
    m_ref[...] = m_new

    @pl.when(c == nc - 1)
    def _():
        on = acc_ref[...] / l_ref[...]
        for h in range(DIFF_HEADS):
            sl = slice(h * DIFF_HD, (h + 1) * DIFF_HD)
            o = on[2 * h:2 * h + 1, sl] - lam_ref[0] * on[2 * h + 1:2 * h + 2, sl]
            o_ref[0, :, sl] = _sub_ln(o, gain_ref[...], lam_init)


def _attn_decode(q, kn, vn, cache_k, cache_v, page_table, bias2, b0, lam, gain, lam_init):
    nb, width = q.shape
    npages = page_table.shape[1]
    npg = PAGES_PER_STEP
    nc = npages // npg
    nrow = 2 * DIFF_HEADS
    ntok = npg * PAGE_SIZE
    row3 = lambda a: a.reshape(nb, 1, width)
    page_spec = lambda i: pl.BlockSpec((1, PAGE_SIZE, width), lambda b, c, pt: (pt[b, c * npg + i], 0, 0))
    rowspec = pl.BlockSpec((1, 1, width), lambda b, c, pt: (b, 0, 0))
    const = lambda shape: pl.BlockSpec(shape, lambda b, c, pt: (0,) * len(shape))
    grid_spec = pltpu.PrefetchScalarGridSpec(
        num_scalar_prefetch=1,
        grid=(nb, nc),
        in_specs=[pl.BlockSpec(memory_space=pltpu.SMEM), rowspec, rowspec, rowspec,
                  const((2, nrow, ntok)), const((nrow, 1)), const((1, DIFF_HD))]
                 + [page_spec(i) for i in range(npg)] * 2,
        out_specs=rowspec,
        scratch_shapes=[pltpu.VMEM((nrow, width), bf16), pltpu.VMEM((ntok, width), bf16),
                        pltpu.VMEM((ntok, width), bf16), pltpu.VMEM((nrow, 1), f32),
                        pltpu.VMEM((nrow, 1), f32), pltpu.VMEM((nrow, width), f32)],
    )
    out = pl.pallas_call(
        functools.partial(_attn_decode_kernel, npg=npg, lam_init=lam_init),
        grid_spec=grid_spec,
        out_shape=jax.ShapeDtypeStruct((nb, 1, width), f32),
        compiler_params=_params("arbitrary", "arbitrary"),
        name="attn_decode",
    )(page_table, lam, row3(q), row3(kn), row3(vn), bias2, b0, gain.reshape(1, DIFF_HD),
      *([cache_k] * npg), *([cache_v] * npg))
    return out.reshape(nb, width)


def _softmax_rows(s):
    e = jnp.exp(s - jnp.max(s, axis=-1, keepdims=True))
    return e / jnp.sum(e, axis=-1, keepdims=True)


def _cross_prompt_kernel(x_ref, g_ref, wq_ref, mk_ref, mv_ref, wo_ref, o_ref, a_ref):
    x = x_ref[...]
    d = x.shape[1]
    hd = d // MEM_HEADS
    h = _rms(x, g_ref[...]).astype(bf16)
    q = (_dot(h, wq_ref[...]) * (hd ** -0.5)).astype(bf16)
    for hh in range(MEM_HEADS):
        sl = slice(hh * hd, (hh + 1) * hd)
        a = _softmax_rows(_dot_nt(q[:, sl], mk_ref[:, sl])).astype(bf16)
        a_ref[:, sl] = _dot(a, mv_ref[:, sl]).astype(bf16)
    o_ref[...] = x + _dot(a_ref[...], wo_ref[...])


def _cross_prompt(x, g, wq, mk, mv, wo, seq):
    m, d = x.shape
    tm = ROW_TILE
    per = seq // tm
    mem = mk.shape[0] // (m // seq)
    return pl.pallas_call(
        _cross_prompt_kernel,
        grid=(m // tm,),
        in_specs=[pl.BlockSpec((tm, d), lambda i: (i, 0)), _resident((1, d)), _resident((d, d)),
                  pl.BlockSpec((mem, d), lambda i: (i // per, 0)), pl.BlockSpec((mem, d), lambda i: (i // per, 0)),
                  _resident((d, d))],
        out_specs=pl.BlockSpec((tm, d), lambda i: (i, 0)),
        out_shape=jax.ShapeDtypeStruct((m, d), f32),
        scratch_shapes=[pltpu.VMEM((tm, d), bf16)],
        compiler_params=_params("arbitrary"),
        name="cross_prompt",
    )(x, g.reshape(1, d), wq, mk, mv, wo)


def _cross_sample_kernel(q_ref, mk_ref, mv_ref, o_ref):
    d = q_ref.shape[2]
    hd = d // MEM_HEADS
    rowi = lax.broadcasted_iota(i32, (SUBLANES, d), 0)
    coli = lax.broadcasted_iota(i32, (SUBLANES, d), 1)
    qm = jnp.where((coli // hd) == rowi, jnp.broadcast_to(q_ref[0] * (hd ** -0.5), (SUBLANES, d)), 0.0)
    a = _softmax_rows(_dot_nt(qm.astype(bf16), mk_ref[0].astype(bf16))).astype(bf16)
    o = _dot(a, mv_ref[0].astype(bf16))
    for hh in range(MEM_HEADS):
        sl = slice(hh * hd, (hh + 1) * hd)
        o_ref[0, :, sl] = o[hh:hh + 1, sl]


def _cross_sample(q, mk, mv):
    nb, d = q.shape
    mem = mk.shape[1]
    return pl.pallas_call(
        _cross_sample_kernel,
        grid=(nb,),
        in_specs=[pl.BlockSpec((1, 1, d), lambda b: (b, 0, 0)),
                  pl.BlockSpec((1, mem, d), lambda b: (b, 0, 0)), pl.BlockSpec((1, mem, d), lambda b: (b, 0, 0))],
        out_specs=pl.BlockSpec((1, 1, d), lambda b: (b, 0, 0)),
        out_shape=jax.ShapeDtypeStruct((nb, 1, d), f32),
        compiler_params=_params("arbitrary"),
        name="cross_sample",
    )(q.reshape(nb, 1, d), mk, mv).reshape(nb, d)


def _rotary_tables(pos, half):
    inv = 1.0 / (10000.0 ** jnp.linspace(0.0, 1.0, half, dtype=f32))
    ang = pos.astype(f32)[:, None] * inv[None, :]
    cos, sin = jnp.cos(ang), jnp.sin(ang)
    return jnp.concatenate([cos, cos], axis=-1), jnp.concatenate([-sin, sin], axis=-1)


def kernel(x_prompt, x_sample, state_ret, state_pool, cache_k_diff, cache_v_diff, cache_mem_k, cache_mem_v, page_table, mem_prompt, norms, final_norm, ffn_w_gate, ffn_w_up, ffn_w_down, w_in_even, ret_gain, pool_w, pool_scale, w_out_even, w_qkv_odd, lambda_q1, lambda_k1, lambda_q2, lambda_k2, subln_gain, w_out_odd, rel_bias, w_cq, w_ckv, w_co):
    batch, seq, d = x_prompt.shape
    nb = x_sample.shape[0]
    depth = norms.shape[0]
    past = page_table.shape[1] * PAGE_SIZE
    mem_len = mem_prompt.shape[1]
    rw = RET_HEADS * RET_DK
    dw = DIFF_HEADS * DIFF_HD

    xp = x_prompt.reshape(batch * seq, d)
    xs = x_sample.reshape(nb, d)
    mem2d = mem_prompt.reshape(batch * mem_len, d)

    cos_p, sin_p = _rotary_tables(jnp.arange(seq, dtype=i32), RET_DK // 2)
    cos_s, sin_s = _rotary_tables(jnp.full((1,), past, dtype=i32), RET_DK // 2)

    wg, wu, wd = ffn_w_gate.astype(bf16), ffn_w_up.astype(bf16), ffn_w_down.astype(bf16)
    w_in, w_oe = w_in_even.astype(bf16), w_out_even.astype(bf16)
    w_qkv, w_oo = w_qkv_odd.astype(bf16), w_out_odd.astype(bf16)
    wcq, wckv, wco = w_cq.astype(bf16), w_ckv.astype(bf16), w_co.astype(bf16)
    pw = pool_w.astype(bf16)

    p_ret, p_pool, p_k, p_v, p_mk, p_mv = [], [], [], [], [], []
    s_ret, s_pool, s_k, s_v = [], [], [], []
    for l in range(depth):
        last = l == depth - 1
        xp = _ffn(xp, norms[l, 0], wg[l, 0], wu[l, 0], wd[l, 0])
        xs = _ffn(xs, norms[l, 0], wg[l, 0], wu[l, 0], wd[l, 0])
        if l % 2 == 0:
            e = l // 2
            zw = w_in.shape[2]
            (zp,) = _proj(xp, norms[l, 1], w_in[e], [(0, zw, [(0, 1.0)])], [(zw, f32)])
            (zs,) = _proj(xs, norms[l, 1], w_in[e], [(0, zw, [(0, 1.0)])], [(zw, f32)])
            mixp, rp, pp = _even_prompt(zp, cos_p, sin_p, ret_gain[e], pw[e], pool_scale[e], batch, seq)
            mixs, rs, ps = _even_sample(zs, cos_s, sin_s, state_ret[e:e + 1], state_pool[e:e + 1], ret_gain[e],
                                        pool_w[e], pool_scale[e], past)
            xp = _mmres(xp, mixp, w_oe[e])
            xs = _mmres(xs, mixs.reshape(nb, rw + pool_scale.shape[1]), w_oe[e])
            p_ret.append(rp[0])
            p_pool.append(pp[0])
            s_ret.append(rs[0])
            s_pool.append(ps[0])
        else:
            o = l // 2
            lam_init = 0.8 - 0.6 * math.exp(-0.3 * l)
            scale = DIFF_D ** -0.5
            tiles, tab, lam_t = _bias_tables(rel_bias, lambda_q1[o], lambda_k1[o], lambda_q2[o], lambda_k2[o],
                                             lam_init, ATT_TILE)
            lam = lam_t[0, 0:1]
            tab = tab[:, 0, :]
            far = tab[:, REL_MAX_DIST]
            qkv_groups = [(0, dw, [(0, scale)]), (dw, dw, [(1, 1.0), (3, 1.0)]), (2 * dw, dw, [(2, 1.0), (4, 1.0)])]
            qb, kp, vp, kb, vb = _proj(xp, norms[l, 1], w_qkv[o], qkv_groups,
                                       [(dw, bf16), (dw, f32), (dw, f32), (dw, bf16), (dw, bf16)])
            attp = _attn_prompt(qb, kb, vb, tiles, far, lam, subln_gain[o], batch, seq, lam_init)
            xp = _mmres(xp, attp, w_oo[o])
            qs, ks_, vs = _proj(xs, norms[l, 1], w_qkv[o],
                                [(0, dw, [(0, scale)]), (dw, dw, [(1, 1.0)]), (2 * dw, dw, [(2, 1.0)])],
                                [(dw, f32), (dw, f32), (dw, f32)])
            ntok = PAGES_PER_STEP * PAGE_SIZE
            rows = lambda a: jnp.repeat(a, 2, axis=0)
            far_rows = jnp.broadcast_to(rows(far)[:, None], (2 * DIFF_HEADS, ntok))
            last_page = rows(tab[:, PAGE_SIZE:0:-1])
            bias2 = jnp.stack([far_rows, jnp.concatenate([far_rows[:, :ntok - PAGE_SIZE], last_page], axis=1)])
            b0 = rows(tab[:, 0])[:, None]
            ck = cache_k_diff[o].reshape(cache_k_diff.shape[1], PAGE_SIZE, dw)
            cv = cache_v_diff[o].reshape(cache_v_diff.shape[1], PAGE_SIZE, dw)
            atts = _attn_decode(qs, ks_, vs, ck, cv, page_table, bias2, b0, lam, subln_gain[o], lam_init)
            xs = _mmres(xs, atts, w_oo[o])
            p_k.append(kp.reshape(batch, seq, DIFF_HEADS, DIFF_HD))
            p_v.append(vp.reshape(batch, seq, DIFF_HEADS, DIFF_HD))
            s_k.append(ks_.reshape(nb, 1, DIFF_HEADS, DIFF_HD))
            s_v.append(vs.reshape(nb, 1, DIFF_HEADS, DIFF_HD))
        mk, mv, mkb, mvb = _proj(mem2d, None, wckv[l], [(0, d, [(0, 1.0), (2, 1.0)]), (d, d, [(1, 1.0), (3, 1.0)])],
                                 [(d, f32), (d, f32), (d, bf16), (d, bf16)])
        p_mk.append(mk.reshape(batch, mem_len, MEM_HEADS, d // MEM_HEADS))
        p_mv.append(mv.reshape(batch, mem_len, MEM_HEADS, d // MEM_HEADS))
        xp = _cross_prompt(xp, norms[l, 2], wcq[l], mkb, mvb, wco[l], seq)
        (qcs,) = _proj(xs, norms[l, 2], wcq[l], [(0, d, [(0, 1.0)])], [(d, f32)])
        acs = _cross_sample(qcs, cache_mem_k[l].reshape(nb, mem_len, d), cache_mem_v[l].reshape(nb, mem_len, d))
        xs = _mmres(xs, acs, wco[l])
        fin = final_norm if last else None
        xp = _ffn(xp, norms[l, 3], wg[l, 1], wu[l, 1], wd[l, 1], fin)
        xs = _ffn(xs, norms[l, 3], wg[l, 1], wu[l, 1], wd[l, 1], fin)

    return (xp.reshape(batch, seq, d), xs.reshape(nb, 1, d),
            jnp.stack(p_ret), jnp.stack(p_pool), jnp.stack(p_k), jnp.stack(p_v), jnp.stack(p_mk), jnp.stack(p_mv),
            jnp.stack(s_ret), jnp.stack(s_pool), jnp.stack(s_k), jnp.stack(s_v))
```

```python
import functools
import math

import jax
import jax.numpy as jnp
from jax import lax
from jax.experimental import pallas as pl
from jax.experimental.pallas import tpu as pltpu

f32 = jnp.float32
bf16 = jnp.bfloat16
i32 = jnp.int32

EPS = 1e-6
NEG_INF = -1e30
LOG2E = math.log2(math.e)

RET_HEADS = 4
RET_DK = 128
POOL_WINDOWS = (2, 4, 8, 16)
POOL_KEEP = max(POOL_WINDOWS) - 1
DIFF_HEADS = 8
DIFF_D = 64
DIFF_HD = 2 * DIFF_D
REL_BUCKETS = 32
REL_MAX_DIST = 128
MEM_HEADS = 4
PAGE_SIZE = 128

VMEM_LIMIT_BYTES = 56 * 1024 * 1024
LANES = 128
SUBLANES = 8

ROW_TILE = 512
ATT_TILE = 512
ATT_QUERY_BLOCK = 512
PAGES_PER_STEP = 8


def _params(*sem):
    return pltpu.CompilerParams(dimension_semantics=sem, vmem_limit_bytes=VMEM_LIMIT_BYTES)


def _rms(x, g):
    return x * lax.rsqrt(jnp.mean(x * x, axis=-1, keepdims=True) + EPS) * g


def _dot(a, b):
    return jnp.dot(a, b, preferred_element_type=f32)


def _dot_nt(a, b):
    return lax.dot_general(a, b, (((1,), (1,)), ((), ())), preferred_element_type=f32)


def _dot_tn(a, b):
    return lax.dot_general(a, b, (((0,), (0,)), ((), ())), preferred_element_type=f32)


def _resident(shape):
    nd = len(shape)
    return pl.BlockSpec(shape, lambda *_: (0,) * nd, pipeline_mode=pl.Buffered(1))


def _ffn_kernel(x_ref, g_ref, wg_ref, wu_ref, wd_ref, *rest, fchunk, final):
    o_ref = rest[-1]
    x = x_ref[...]
    h = _rms(x, g_ref[...]).astype(bf16)
    acc = jnp.zeros(x.shape, f32)
    for c0 in range(0, wg_ref.shape[1], fchunk):
        gate = _dot(h, wg_ref[:, c0:c0 + fchunk])
        up = _dot(h, wu_ref[:, c0:c0 + fchunk])
        a = (jax.nn.silu(gate) * up).astype(bf16)
        acc = acc + _dot(a, wd_ref[c0:c0 + fchunk, :])
    y = x + 0.5 * acc
    if final:
        y = _rms(y, rest[0][...])
    o_ref[...] = y


def _ffn(x, g, wg, wu, wd, final_g=None):
    m, d = x.shape
    f = wg.shape[1]
    tm = min(ROW_TILE, m)
    fchunk = f // 2 if (f // 2) % LANES == 0 else f
    final = final_g is not None
    in_specs = [pl.BlockSpec((tm, d), lambda i: (i, 0)), _resident((1, d)),
                _resident((d, f)), _resident((d, f)), _resident((f, d))]
    args = [x, g.reshape(1, d), wg, wu, wd]
    if final:
        in_specs.append(_resident((1, d)))
        args.append(final_g.reshape(1, d))
    return pl.pallas_call(
        functools.partial(_ffn_kernel, fchunk=fchunk, final=final),
        grid=(m // tm,),
        in_specs=in_specs,
        out_specs=pl.BlockSpec((tm, d), lambda i: (i, 0)),
        out_shape=jax.ShapeDtypeStruct((m, d), f32),
        compiler_params=_params("arbitrary"),
        name="ffn",
    )(*args)


def _proj_kernel(*refs, norm, groups, nchunk):
    x_ref = refs[0]
    g_ref = refs[1] if norm else None
    w_ref = refs[2] if norm else refs[1]
    o_refs = refs[(3 if norm else 2):]
    x = x_ref[...]
    h = (_rms(x, g_ref[...]) if norm else x).astype(bf16)
    for off, width, targets in groups:
        for c0 in range(0, width, nchunk):
            n = min(nchunk, width - c0)
            r = _dot(h, w_ref[:, off + c0:off + c0 + n])
            for oi, scale in targets:
                o_refs[oi][:, c0:c0 + n] = (r if scale == 1.0 else r * scale).astype(o_refs[oi].dtype)


def _proj(x, g, w, groups, outs):
    m, d = x.shape
    n = w.shape[1]
    tm = min(ROW_TILE, m)
    norm = g is not None
    in_specs = [pl.BlockSpec((tm, d), lambda i: (i, 0))]
    args = [x]
    if norm:
        in_specs.append(_resident((1, d)))
        args.append(g.reshape(1, d))
    in_specs.append(_resident((d, n)))
    args.append(w)
    return pl.pallas_call(
        functools.partial(_proj_kernel, norm=norm, groups=groups, nchunk=512),
        grid=(m // tm,),
        in_specs=in_specs,
        out_specs=[pl.BlockSpec((tm, wd), lambda i: (i, 0)) for wd, _ in outs],
        out_shape=[jax.ShapeDtypeStruct((m, wd), dt) for wd, dt in outs],
        compiler_params=_params("arbitrary"),
        name="proj",
    )(*args)


def _mmres_kernel(x_ref, a_ref, w_ref, o_ref):
    o_ref[...] = x_ref[...] + _dot(a_ref[...].astype(bf16), w_ref[...])


def _mmres(x, a, w):
    m, d = x.shape
    k = a.shape[1]
    tm = min(ROW_TILE, m)
    return pl.pallas_call(
        _mmres_kernel,
        grid=(m // tm,),
        in_specs=[pl.BlockSpec((tm, d), lambda i: (i, 0)), pl.BlockSpec((tm, k), lambda i: (i, 0)),
                  _resident((k, d))],
        out_specs=pl.BlockSpec((tm, d), lambda i: (i, 0)),
        out_shape=jax.ShapeDtypeStruct((m, d), f32),
        compiler_params=_params("arbitrary"),
        name="mmres",
    )(x, a, w)


def _ret_log_gamma(h):
    return math.log1p(-(2.0 ** (-5.0 - h)))


def _rotate(x, cosf, sinf):
    return x * cosf + pltpu.roll(x, x.shape[-1] // 2, axis=1) * sinf


def _group_norm_gate(o, gain, gate):
    mu = jnp.mean(o, axis=-1, keepdims=True)
    var = jnp.mean(jnp.square(o - mu), axis=-1, keepdims=True)
    return jax.nn.silu(gate) * ((o - mu) * lax.rsqrt(var + EPS) * gain)


def _even_prompt_kernel(zq_ref, zk_ref, zv_ref, zg_ref, zp_ref, cos_ref, sin_ref, gain_ref, pw_ref, ps_ref,
                        mix_ref, sret_ref, spool_ref, state_ref, ext_ref, intra_ref, *, t):
    b = pl.program_id(0)
    c = pl.program_id(1)
    nc = pl.num_programs(1)
    halo = POOL_KEEP + 1
    dk = RET_DK

    @pl.when(jnp.logical_and(b == 0, c == 0))
    def _():
        rel = lax.broadcasted_iota(i32, (t, t), 0) - lax.broadcasted_iota(i32, (t, t), 1)
        relf = jnp.maximum(rel, 0).astype(f32)
        for h in range(RET_HEADS):
            intra_ref[h] = jnp.where(rel >= 0, jnp.exp(_ret_log_gamma(h) * relf), 0.0)

    @pl.when(c == 0)
    def _():
        state_ref[...] = jnp.zeros(state_ref.shape, f32)
        ext_ref[0:halo, :] = jnp.zeros((halo, ext_ref.shape[1]), f32)

    cosf = cos_ref[...]
    sinf = sin_ref[...]
    row = lax.broadcasted_iota(i32, (t, 1), 0).astype(f32)
    for h in range(RET_HEADS):
        sl = slice(h * dk, (h + 1) * dk)
        lg = _ret_log_gamma(h)
        qr = _rotate(zq_ref[:, sl], cosf, sinf)
        kr = _rotate(zk_ref[:, sl], cosf, sinf) * (dk ** -0.5)
        qb = qr.astype(bf16)
        vb = zv_ref[:, sl].astype(bf16)
        att = _dot_nt(qb, kr.astype(bf16)) * intra_ref[h]
        s_old = state_ref[h]
        o = _dot(att.astype(bf16), vb) + _dot(qb, s_old.astype(bf16)) * jnp.exp(lg * (row + 1.0))
        k_dec = jnp.exp(lg * (float(t - 1) - row))
        state_ref[h] = s_old * math.exp(lg * t) + _dot_tn((kr * k_dec).astype(bf16), vb)
        mix_ref[:, sl] = _group_norm_gate(o, gain_ref[:, sl], zg_ref[:, sl]).astype(mix_ref.dtype)

    p = zp_ref[...]
    ext_ref[halo:halo + t, :] = p
    pos = c * t + lax.broadcasted_iota(i32, (t, 1), 0)
    gw = p.shape[1] // len(POOL_WINDOWS)
    ret_w = RET_HEADS * dk
    for gi, w in enumerate(POOL_WINDOWS):
        sl = slice(gi * gw, (gi + 1) * gw)
        win = p[:, sl]
        for back in range(1, w):
            win = win + ext_ref[halo - back:halo - back + t, sl]
        cnt = jnp.minimum(w, pos + 1).astype(f32)
        pooled = win / cnt - p[:, sl]
        mixed = _dot(pooled.astype(bf16), pw_ref[gi]) * ps_ref[:, sl]
        mix_ref[:, ret_w + gi * gw:ret_w + (gi + 1) * gw] = mixed.astype(mix_ref.dtype)

    @pl.when(c == nc - 1)
    def _():
        sret_ref[0, 0] = state_ref[...]
        spool_ref[0, 0] = ext_ref[t + 1:t + halo, :]

    ext_ref[0:halo, :] = ext_ref[t:t + halo, :]


def _even_prompt(z, cosf, sinf, gain, pw, ps, batch, seq):
    t = ROW_TILE
    nc = seq // t
    rw = RET_HEADS * RET_DK
    pwid = ps.shape[0]
    zspec = lambda j: pl.BlockSpec((t, rw), lambda b, c: (b * nc + c, j))
    return pl.pallas_call(
        functools.partial(_even_prompt_kernel, t=t),
        grid=(batch, nc),
        in_specs=[zspec(0), zspec(1), zspec(2), zspec(3), zspec(4),
                  pl.BlockSpec((t, RET_DK), lambda b, c: (c, 0)), pl.BlockSpec((t, RET_DK), lambda b, c: (c, 0)),
                  _resident((1, rw)), _resident(pw.shape), _resident((1, pwid))],
        out_specs=[pl.BlockSpec((t, rw + pwid), lambda b, c: (b * nc + c, 0)),
                   pl.BlockSpec((1, 1, RET_HEADS, RET_DK, RET_DK), lambda b, c: (0, b, 0, 0, 0)),
                   pl.BlockSpec((1, 1, POOL_KEEP, pwid), lambda b, c: (0, b, 0, 0))],
        out_shape=[jax.ShapeDtypeStruct((batch * seq, rw + pwid), bf16),
                   jax.ShapeDtypeStruct((1, batch, RET_HEADS, RET_DK, RET_DK), f32),
                   jax.ShapeDtypeStruct((1, batch, POOL_KEEP, pwid), f32)],
        scratch_shapes=[pltpu.VMEM((RET_HEADS, RET_DK, RET_DK), f32),
                        pltpu.VMEM((POOL_KEEP + 1 + t, pwid), f32),
                        pltpu.VMEM((RET_HEADS, t, t), f32)],
        compiler_params=_params("arbitrary", "arbitrary"),
        name="even_prompt",
    )(z, z, z, z, z, cosf, sinf, gain.reshape(1, rw), pw, ps.reshape(1, pwid))


def _even_sample_kernel(z_ref, cos_ref, sin_ref, s0_ref, rows_ref, gain_ref, pw_ref, ps_ref,
                        mix_ref, sret_ref, spool_ref, ext_ref, *, pos):
    dk = RET_DK
    rw = RET_HEADS * dk
    z = jnp.broadcast_to(z_ref[0], (SUBLANES, z_ref.shape[2]))
    cosf = jnp.broadcast_to(cos_ref[...], (SUBLANES, dk))
    sinf = jnp.broadcast_to(sin_ref[...], (SUBLANES, dk))
    first = lax.broadcasted_iota(i32, (SUBLANES, dk), 0) == 0
    for h in range(RET_HEADS):
        sl = slice(h * dk, (h + 1) * dk)
        gamma = math.exp(_ret_log_gamma(h))
        qr = _rotate(z[:, sl], cosf, sinf)
        kr = _rotate(z[:, rw + h * dk:rw + (h + 1) * dk], cosf, sinf) * (dk ** -0.5)
        v = z[:, 2 * rw + h * dk:2 * rw + (h + 1) * dk]
        s_new = s0_ref[0, 0, h] * gamma + _dot_tn(jnp.where(first, kr, 0.0), v)
        sret_ref[0, 0, h] = s_new
        o = _dot(qr, s_new)
        gate = z[:, 3 * rw + h * dk:3 * rw + (h + 1) * dk]
        mix_ref[0, :, sl] = _group_norm_gate(o, gain_ref[:, sl], gate)[0:1]

    keep = POOL_KEEP
    pwid = ps_ref.shape[1]
    p = z[0:1, 4 * rw:4 * rw + pwid]
    ext_ref[0:keep, :] = rows_ref[0, 0]
    ext_ref[keep:keep + 1, :] = p
    spool_ref[0, 0] = ext_ref[1:keep + 1, :]
    gw = pwid // len(POOL_WINDOWS)
    for gi, w in enumerate(POOL_WINDOWS):
        sl = slice(gi * gw, (gi + 1) * gw)
        win = jnp.sum(ext_ref[keep + 1 - w:keep + 1, sl], axis=0, keepdims=True)
        pooled = win / float(min(w, pos + 1)) - p[:, sl]
        mixed = _dot(jnp.broadcast_to(pooled, (SUBLANES, gw)), pw_ref[gi])[0:1] * ps_ref[:, sl]
        mix_ref[0, :, rw + gi * gw:rw + (gi + 1) * gw] = mixed


def _even_sample(z, cosf, sinf, state_ret, state_pool, gain, pw, ps, pos):
    nb, zw = z.shape
    rw = RET_HEADS * RET_DK
    pwid = ps.shape[0]
    return pl.pallas_call(
        functools.partial(_even_sample_kernel, pos=pos),
        grid=(nb,),
        in_specs=[pl.BlockSpec((1, 1, zw), lambda b: (b, 0, 0)),
                  _resident((1, RET_DK)), _resident((1, RET_DK)),
                  pl.BlockSpec((1, 1, RET_HEADS, RET_DK, RET_DK), lambda b: (0, b, 0, 0, 0)),
                  pl.BlockSpec((1, 1, POOL_KEEP, pwid), lambda b: (0, b, 0, 0)),
                  _resident((1, rw)), _resident(pw.shape), _resident((1, pwid))],
        out_specs=[pl.BlockSpec((1, 1, rw + pwid), lambda b: (b, 0, 0)),
                   pl.BlockSpec((1, 1, RET_HEADS, RET_DK, RET_DK), lambda b: (0, b, 0, 0, 0)),
                   pl.BlockSpec((1, 1, POOL_KEEP, pwid), lambda b: (0, b, 0, 0))],
        out_shape=[jax.ShapeDtypeStruct((nb, 1, rw + pwid), f32),
                   jax.ShapeDtypeStruct((1, nb, RET_HEADS, RET_DK, RET_DK), f32),
                   jax.ShapeDtypeStruct((1, nb, POOL_KEEP, pwid), f32)],
        scratch_shapes=[pltpu.VMEM((POOL_KEEP + 1, pwid), f32)],
        compiler_params=_params("arbitrary"),
        name="even_sample",
    )(z.reshape(nb, 1, zw), cosf, sinf, state_ret, state_pool, gain.reshape(1, rw), pw, ps.reshape(1, pwid))


def _rel_bias_of(dist, rb_ref, h):
    n = jnp.maximum(dist, 0)
    max_exact = REL_BUCKETS // 2
    nf = jnp.maximum(n, 1).astype(f32)
    large = max_exact + (jnp.log(nf / max_exact) / math.log(REL_MAX_DIST / max_exact)
                         * (REL_BUCKETS - max_exact)).astype(i32)
    large = jnp.minimum(large, REL_BUCKETS - 1)
    bucket = jnp.where(n < max_exact, n, large)
    out = jnp.zeros(dist.shape, f32)
    for bk in range(REL_BUCKETS):
        out = jnp.where(bucket == bk, rb_ref[bk, h], out)
    return out


def _bias_kernel(rb_ref, lq1_ref, lk1_ref, lq2_ref, lk2_ref, tile_ref, tab_ref, lam_ref, *, t, lam_init):
    h = pl.program_id(0)
    rel = lax.broadcasted_iota(i32, (t, t), 1) - lax.broadcasted_iota(i32, (t, t), 0)
    diag = jnp.where(rel >= 0, _rel_bias_of(rel, rb_ref, h) * LOG2E, NEG_INF)
    below = _rel_bias_of(rel + t, rb_ref, h) * LOG2E
    tile_ref[0, 0, :, 0:t] = diag
    tile_ref[0, 0, :, t:2 * t] = diag
    tile_ref[0, 1, :, 0:t] = below
    tile_ref[0, 1, :, t:2 * t] = below
    tab_ref[0] = _rel_bias_of(lax.broadcasted_iota(i32, (1, 2 * REL_MAX_DIST), 1), rb_ref, h)
    lam = (jnp.exp(jnp.sum(lq1_ref[...] * lk1_ref[...], axis=-1, keepdims=True))
           - jnp.exp(jnp.sum(lq2_ref[...] * lk2_ref[...], axis=-1, keepdims=True)) + lam_init)
    lam_ref[...] = jnp.broadcast_to(lam, lam_ref.shape)


def _bias_tables(rel_bias, lq1, lk1, lq2, lk2, lam_init, t):
    nh = rel_bias.shape[1]
    d = lq1.shape[0]
    vec = lambda a: a.reshape(1, d)
    return pl.pallas_call(
        functools.partial(_bias_kernel, t=t, lam_init=lam_init),
        grid=(nh,),
        in_specs=[pl.BlockSpec(memory_space=pltpu.SMEM)] + [_resident((1, d))] * 4,
        out_specs=[pl.BlockSpec((1, 2, t, 2 * t), lambda h: (h, 0, 0, 0)),
                   pl.BlockSpec((1, 1, 2 * REL_MAX_DIST), lambda h: (h, 0, 0)),
                   pl.BlockSpec((SUBLANES, LANES), lambda h: (0, 0))],
        out_shape=[jax.ShapeDtypeStruct((nh, 2, t, 2 * t), f32),
                   jax.ShapeDtypeStruct((nh, 1, 2 * REL_MAX_DIST), f32),
                   jax.ShapeDtypeStruct((SUBLANES, LANES), f32)],
        compiler_params=_params("arbitrary"),
        name="bias_tables",
    )(rel_bias, vec(lq1), vec(lk1), vec(lq2), vec(lk2))


def _sub_ln(o, gain, lam_init):
    return o * lax.rsqrt(jnp.mean(o * o, axis=-1, keepdims=True) + EPS) * gain * (1.0 - lam_init)


def _attn_prompt_kernel(lam_ref, far_ref, q_ref, k_ref, vt_ref, tile_ref, gain_ref, o_ref,
                        qq_ref, m_ref, l_ref, acc_ref, *, t, cb, lam_init):
    h = pl.program_id(1)
    qi = pl.program_id(2)
    q = q_ref[...]
    lane = lax.broadcasted_iota(i32, q.shape, 1)
    zero = jnp.zeros_like(q)
    qq_ref[0:t, :] = jnp.where(lane < DIFF_D, q, zero)
    qq_ref[t:2 * t, :] = jnp.where(lane >= DIFF_D, q, zero)
    m_ref[...] = jnp.full(m_ref.shape, NEG_INF, f32)
    l_ref[...] = jnp.zeros(l_ref.shape, f32)
    acc_ref[...] = jnp.zeros(acc_ref.shape, f32)

    far = far_ref[h] * LOG2E

    def step(k0, tile):
        kt = k_ref[pl.ds(k0, t), :]
        vt = vt_ref[:, pl.ds(k0, t)]
        nblk = 2 * t // cb
        scores = lambda i: _dot_nt(kt, qq_ref[i * cb:(i + 1) * cb, :])
        s_next = scores(0)
        for i in range(nblk):
            cols = slice(i * cb, (i + 1) * cb)
            s = s_next
            if i + 1 < nblk:
                s_next = scores(i + 1)
            m_old = m_ref[:, cols]
            if tile is None:
                m_new = jnp.maximum(m_old, jnp.max(s, axis=0, keepdims=True) + far)
                p = jnp.exp2(s - (m_new - far))
            else:
                s = s + tile_ref[0, tile, :, cols]
                m_new = jnp.maximum(m_old, jnp.max(s, axis=0, keepdims=True))
                p = jnp.exp2(s - m_new)
            alpha = jnp.exp2(m_old - m_new)
            l_ref[:, cols] = alpha * l_ref[:, cols] + jnp.sum(p, axis=0, keepdims=True)
            acc_ref[:, cols] = alpha * acc_ref[:, cols] + _dot(vt, p.astype(bf16))
            m_ref[:, cols] = m_new

    def far_body(ki, carry):
        step(pl.multiple_of(ki * t, t), None)
        return carry

    def near_body(ki, carry):
        step(pl.multiple_of(ki * t, t), qi - ki)
        return carry

    n_far = jnp.maximum(qi - 1, 0)
    lax.fori_loop(0, n_far, far_body, 0)
    lax.fori_loop(n_far, qi + 1, near_body, 0)

    o = acc_ref[...] / l_ref[...]
    o = o[:, 0:t] - lam_ref[0] * o[:, t:2 * t]
    y = o * lax.rsqrt(jnp.mean(o * o, axis=0, keepdims=True) + EPS) * gain_ref[...] * (1.0 - lam_init)
    o_ref[...] = y.T.astype(o_ref.dtype)


def _attn_prompt(q, k, vt, tiles, far, lam, gain, batch, seq, lam_init):
    t = ATT_TILE
    nq = seq // t
    hd = DIFF_HD
    nh = DIFF_HEADS
    smem = pl.BlockSpec(memory_space=pltpu.SMEM)
    return pl.pallas_call(
        functools.partial(_attn_prompt_kernel, t=t, cb=ATT_QUERY_BLOCK, lam_init=lam_init),
        grid=(batch, nh, nq),
        in_specs=[smem, smem,
                  pl.BlockSpec((t, hd), lambda b, h, i: (b * nq + i, h)),
                  pl.BlockSpec((seq, hd), lambda b, h, i: (b, h)),
                  pl.BlockSpec((hd, seq), lambda b, h, i: (b * nh + h, 0)),
                  pl.BlockSpec((1, 2, t, 2 * t), lambda b, h, i: (h, 0, 0, 0)),
                  _resident((hd, 1))],
        out_specs=pl.BlockSpec((t, hd), lambda b, h, i: (b * nq + i, h)),
        out_shape=jax.ShapeDtypeStruct((batch * seq, nh * hd), bf16),
        scratch_shapes=[pltpu.VMEM((2 * t, hd), bf16), pltpu.VMEM((1, 2 * t), f32),
                        pltpu.VMEM((1, 2 * t), f32), pltpu.VMEM((hd, 2 * t), f32)],
        compiler_params=_params("arbitrary", "arbitrary", "arbitrary"),
        name="attn_prompt",
    )(lam, far, q, k, vt, tiles, gain.reshape(hd, 1))


def _attn_decode_kernel(pt_ref, lam_ref, q_ref, kn_ref, vn_ref, bias_ref, b0_ref, gain_ref, *rest,
                        npg, lam_init):
    k_refs = rest[:npg]
    v_refs = rest[npg:2 * npg]
    o_ref = rest[2 * npg]
    qh_ref, kbuf_ref, vbuf_ref, m_ref, l_ref, acc_ref = rest[2 * npg + 1:]
    c = pl.program_id(1)
    nc = pl.num_programs(1)
    nh = DIFF_HEADS
    prow = PAGE_SIZE * nh

    def both_streams(x):
        return jnp.concatenate([x, x], axis=0)

    @pl.when(c == 0)
    def _():
        q = q_ref[0]
        lane = lax.broadcasted_iota(i32, q.shape, 1)
        qh = jnp.concatenate([jnp.where(lane < DIFF_D, q, 0.0), jnp.where(lane >= DIFF_D, q, 0.0)],
                             axis=0).astype(bf16)
        qh_ref[...] = qh
        kn = both_streams(kn_ref[0]).astype(bf16).astype(f32)
        m_ref[...] = jnp.sum(qh.astype(f32) * kn, axis=-1, keepdims=True) + b0_ref[...]
        l_ref[...] = jnp.ones(l_ref.shape, f32)
        acc_ref[...] = both_streams(vn_ref[0]).astype(bf16).astype(f32)

    for i in range(npg):
        kbuf_ref[i * prow:(i + 1) * prow, :] = k_refs[i][...].astype(bf16)
        vbuf_ref[i * prow:(i + 1) * prow, :] = v_refs[i][...].astype(bf16)

    s = _dot_nt(qh_ref[...], kbuf_ref[...]) + bias_ref[(c == nc - 1).astype(i32)]
    m_old = m_ref[...]
    m_new = jnp.maximum(m_old, jnp.max(s, axis=-1, keepdims=True))
    alpha = jnp.exp(m_old - m_new)
    p = jnp.exp(s - m_new)
    l_ref[...] = alpha * l_ref[...] + jnp.sum(p, axis=-1, keepdims=True)
    acc_ref[...] = alpha * acc_ref[...] + _dot(p.astype(bf16), vbuf_ref[...])
    m_ref[...] = m_new

    @pl.when(c == nc - 1)
    def _():
        on = acc_ref[...] / l_ref[...]
        o_ref[0] = _sub_ln(on[0:nh] - lam_ref[0] * on[nh:2 * nh], gain_ref[...], lam_init)


def _attn_decode(q, kn, vn, cache_k, cache_v, layer, page_table, bias2, b0, lam, gain, lam_init):
    nb, nh, hd = q.shape
    pool = cache_k.shape[1]
    npages = page_table.shape[1]
    npg = PAGES_PER_STEP
    nc = npages // npg
    nrow = 2 * nh
    prow = PAGE_SIZE * nh
    page_spec = lambda i: pl.BlockSpec((prow, hd), lambda b, c, pt: (layer * pool + pt[b, c * npg + i], 0))
    rowspec = pl.BlockSpec((1, nh, hd), lambda b, c, pt: (b, 0, 0))
    const = lambda shape: pl.BlockSpec(shape, lambda b, c, pt: (0,) * len(shape))
    grid_spec = pltpu.PrefetchScalarGridSpec(
        num_scalar_prefetch=1,
        grid=(nb, nc),
        in_specs=[pl.BlockSpec(memory_space=pltpu.SMEM), rowspec, rowspec, rowspec,
                  const((2, nrow, npg * prow)), const((nrow, 1)), const((1, hd))]
                 + [page_spec(i) for i in range(npg)] * 2,
        out_specs=rowspec,
        scratch_shapes=[pltpu.VMEM((nrow, hd), bf16), pltpu.VMEM((npg * prow, hd), bf16),
                        pltpu.VMEM((npg * prow, hd), bf16), pltpu.VMEM((nrow, 1), f32),
                        pltpu.VMEM((nrow, 1), f32), pltpu.VMEM((nrow, hd), f32)],
    )
    ck = cache_k.reshape(-1, hd)
    cv = cache_v.reshape(-1, hd)
    return pl.pallas_call(
        functools.partial(_attn_decode_kernel, npg=npg, lam_init=lam_init),
        grid_spec=grid_spec,
        out_shape=jax.ShapeDtypeStruct((nb, nh, hd), f32),
        compiler_params=_params("arbitrary", "arbitrary"),
        name="attn_decode",
    )(page_table, lam, q, kn, vn, bias2, b0, gain.reshape(1, hd), *([ck] * npg), *([cv] * npg))


def _softmax_rows(s):
    e = jnp.exp(s - jnp.max(s, axis=-1, keepdims=True))
    return e / jnp.sum(e, axis=-1, keepdims=True)


def _cross_prompt_kernel(x_ref, g_ref, wq_ref, mk_ref, mv_ref, wo_ref, o_ref, a_ref):
    x = x_ref[...]
    d = x.shape[1]
    hd = d // MEM_HEADS
    h = _rms(x, g_ref[...]).astype(bf16)
    q = (_dot(h, wq_ref[...]) * (hd ** -0.5)).astype(bf16)
    for hh in range(MEM_HEADS):
        sl = slice(hh * hd, (hh + 1) * hd)
        a = _softmax_rows(_dot_nt(q[:, sl], mk_ref[:, sl])).astype(bf16)
        a_ref[:, sl] = _dot(a, mv_ref[:, sl]).astype(bf16)
    o_ref[...] = x + _dot(a_ref[...], wo_ref[...])


def _cross_prompt(x, g, wq, mk, mv, wo, seq):
    m, d = x.shape
    tm = ROW_TILE
    per = seq // tm
    mem = mk.shape[0] // (m // seq)
    return pl.pallas_call(
        _cross_prompt_kernel,
        grid=(m // tm,),
        in_specs=[pl.BlockSpec((tm, d), lambda i: (i, 0)), _resident((1, d)), _resident((d, d)),
                  pl.BlockSpec((mem, d), lambda i: (i // per, 0)), pl.BlockSpec((mem, d), lambda i: (i // per, 0)),
                  _resident((d, d))],
        out_specs=pl.BlockSpec((tm, d), lambda i: (i, 0)),
        out_shape=jax.ShapeDtypeStruct((m, d), f32),
        scratch_shapes=[pltpu.VMEM((tm, d), bf16)],
        compiler_params=_params("arbitrary"),
        name="cross_prompt",
    )(x, g.reshape(1, d), wq, mk, mv, wo)


def _cross_sample_kernel(q_ref, mk_ref, mv_ref, mask_ref, o_ref):
    nh, hd = q_ref.shape[1], q_ref.shape[2]
    q = jnp.concatenate([q_ref[0] * (hd ** -0.5), jnp.zeros((SUBLANES - nh, hd), f32)], axis=0)
    s = _dot_nt(q.astype(bf16), mk_ref[...].astype(bf16)) + mask_ref[...]
    o = _dot(_softmax_rows(s).astype(bf16), mv_ref[...].astype(bf16))
    o_ref[0] = o[0:nh]


def _cross_sample(q, cache_mk, cache_mv, layer):
    nb, nh, hd = q.shape
    rows = cache_mk.shape[2] * nh
    head_of_row = jnp.arange(rows, dtype=i32) % nh
    mask = jnp.where(head_of_row[None, :] == jnp.arange(SUBLANES, dtype=i32)[:, None], 0.0, NEG_INF).astype(f32)
    mask = mask.at[nh:].set(0.0)
    mem_spec = pl.BlockSpec((rows, hd), lambda b: (layer * nb + b, 0))
    return pl.pallas_call(
        _cross_sample_kernel,
        grid=(nb,),
        in_specs=[pl.BlockSpec((1, nh, hd), lambda b: (b, 0, 0)), mem_spec, mem_spec, _resident((SUBLANES, rows))],
        out_specs=pl.BlockSpec((1, nh, hd), lambda b: (b, 0, 0)),
        out_shape=jax.ShapeDtypeStruct((nb, nh, hd), f32),
        compiler_params=_params("arbitrary"),
        name="cross_sample",
    )(q, cache_mk.reshape(-1, hd), cache_mv.reshape(-1, hd), mask)


def _rotary_tables(pos, half):
    inv = 1.0 / (10000.0 ** jnp.linspace(0.0, 1.0, half, dtype=f32))
    ang = pos.astype(f32)[:, None] * inv[None, :]
    cos, sin = jnp.cos(ang), jnp.sin(ang)
    return jnp.concatenate([cos, cos], axis=-1), jnp.concatenate([-sin, sin], axis=-1)


def kernel(x_prompt, x_sample, state_ret, state_pool, cache_k_diff, cache_v_diff, cache_mem_k, cache_mem_v, page_table, mem_prompt, norms, final_norm, ffn_w_gate, ffn_w_up, ffn_w_down, w_in_even, ret_gain, pool_w, pool_scale, w_out_even, w_qkv_odd, lambda_q1, lambda_k1, lambda_q2, lambda_k2, subln_gain, w_out_odd, rel_bias, w_cq, w_ckv, w_co):
    batch, seq, d = x_prompt.shape
    nb = x_sample.shape[0]
    depth = norms.shape[0]
    past = page_table.shape[1] * PAGE_SIZE
    mem_len = mem_prompt.shape[1]
    rw = RET_HEADS * RET_DK
    dw = DIFF_HEADS * DIFF_HD

    xp = x_prompt.reshape(batch * seq, d)
    xs = x_sample.reshape(nb, d)
    mem2d = mem_prompt.reshape(batch * mem_len, d)

    cos_p, sin_p = _rotary_tables(jnp.arange(seq, dtype=i32), RET_DK // 2)
    cos_s, sin_s = _rotary_tables(jnp.full((1,), past, dtype=i32), RET_DK // 2)

    wg, wu, wd = ffn_w_gate.astype(bf16), ffn_w_up.astype(bf16), ffn_w_down.astype(bf16)
    w_in, w_oe = w_in_even.astype(bf16), w_out_even.astype(bf16)
    w_qkv, w_oo = w_qkv_odd.astype(bf16), w_out_odd.astype(bf16)
    wcq, wckv, wco = w_cq.astype(bf16), w_ckv.astype(bf16), w_co.astype(bf16)
    pw = pool_w.astype(bf16)

    p_ret, p_pool, p_k, p_v, p_mk, p_mv = [], [], [], [], [], []
    s_ret, s_pool, s_k, s_v = [], [], [], []
    for l in range(depth):
        last = l == depth - 1
        xp = _ffn(xp, norms[l, 0], wg[l, 0], wu[l, 0], wd[l, 0])
        xs = _ffn(xs, norms[l, 0], wg[l, 0], wu[l, 0], wd[l, 0])
        if l % 2 == 0:
            e = l // 2
            zw = w_in.shape[2]
            (zp,) = _proj(xp, norms[l, 1], w_in[e], [(0, zw, [(0, 1.0)])], [(zw, f32)])
            (zs,) = _proj(xs, norms[l, 1], w_in[e], [(0, zw, [(0, 1.0)])], [(zw, f32)])
            mixp, rp, pp = _even_prompt(zp, cos_p, sin_p, ret_gain[e], pw[e], pool_scale[e], batch, seq)
            mixs, rs, ps = _even_sample(zs, cos_s, sin_s, state_ret[e:e + 1], state_pool[e:e + 1], ret_gain[e],
                                        pool_w[e], pool_scale[e], past)
            xp = _mmres(xp, mixp, w_oe[e])
            xs = _mmres(xs, mixs.reshape(nb, rw + pool_scale.shape[1]), w_oe[e])
            p_ret.append(rp[0])
            p_pool.append(pp[0])
            s_ret.append(rs[0])
            s_pool.append(ps[0])
        else:
            o = l // 2
            lam_init = 0.8 - 0.6 * math.exp(-0.3 * l)
            scale = DIFF_D ** -0.5
            tiles, tab, lam_t = _bias_tables(rel_bias, lambda_q1[o], lambda_k1[o], lambda_q2[o], lambda_k2[o],
                                             lam_init, ATT_TILE)
            lam = lam_t[0, 0:1]
            tab = tab[:, 0, :]
            far = tab[:, REL_MAX_DIST]
            qkv_groups = [(0, dw, [(0, scale * LOG2E)]), (dw, dw, [(1, 1.0), (3, 1.0)]),
                          (2 * dw, dw, [(2, 1.0), (4, 1.0)])]
            qb, kp, vp, kb, vb = _proj(xp, norms[l, 1], w_qkv[o], qkv_groups,
                                       [(dw, bf16), (dw, f32), (dw, f32), (dw, bf16), (dw, bf16)])
            vtb = vb.reshape(batch, seq, dw).transpose(0, 2, 1).reshape(batch * dw, seq)
            attp = _attn_prompt(qb, kb, vtb, tiles, far, lam, subln_gain[o], batch, seq, lam_init)
            xp = _mmres(xp, attp, w_oo[o])
            qs, ks_, vs = _proj(xs, norms[l, 1], w_qkv[o],
                                [(0, dw, [(0, scale)]), (dw, dw, [(1, 1.0)]), (2 * dw, dw, [(2, 1.0)])],
                                [(dw, f32), (dw, f32), (dw, f32)])
            nh = DIFF_HEADS
            own = jnp.arange(nh)[:, None, None] == jnp.arange(nh)[None, None, :]
            far_page = jnp.where(own, far[:, None, None], NEG_INF) + jnp.zeros((1, PAGE_SIZE, 1), f32)
            last_page = jnp.where(own, tab[:, PAGE_SIZE:0:-1][:, :, None], NEG_INF)
            far_page, last_page = far_page.reshape(nh, -1), last_page.reshape(nh, -1)
            far_step = jnp.tile(far_page, (2, PAGES_PER_STEP))
            last_step = jnp.concatenate([far_step[:, :-PAGE_SIZE * nh], jnp.tile(last_page, (2, 1))], axis=1)
            bias2 = jnp.stack([far_step, last_step])
            b0 = jnp.tile(tab[:, 0], 2)[:, None]
            heads = lambda a: a.reshape(nb, nh, DIFF_HD)
            atts = _attn_decode(heads(qs), heads(ks_), heads(vs), cache_k_diff, cache_v_diff, o, page_table,
                                bias2, b0, lam, subln_gain[o], lam_init)
            xs = _mmres(xs, atts.reshape(nb, dw), w_oo[o])
            p_k.append(kp.reshape(batch, seq, DIFF_HEADS, DIFF_HD))
            p_v.append(vp.reshape(batch, seq, DIFF_HEADS, DIFF_HD))
            s_k.append(ks_.reshape(nb, 1, DIFF_HEADS, DIFF_HD))
            s_v.append(vs.reshape(nb, 1, DIFF_HEADS, DIFF_HD))
        mk, mv, mkb, mvb = _proj(mem2d, None, wckv[l], [(0, d, [(0, 1.0), (2, 1.0)]), (d, d, [(1, 1.0), (3, 1.0)])],
                                 [(d, f32), (d, f32), (d, bf16), (d, bf16)])
        p_mk.append(mk.reshape(batch, mem_len, MEM_HEADS, d // MEM_HEADS))
        p_mv.append(mv.reshape(batch, mem_len, MEM_HEADS, d // MEM_HEADS))
        xp = _cross_prompt(xp, norms[l, 2], wcq[l], mkb, mvb, wco[l], seq)
        (qcs,) = _proj(xs, norms[l, 2], wcq[l], [(0, d, [(0, 1.0)])], [(d, f32)])
        acs = _cross_sample(qcs.reshape(nb, MEM_HEADS, d // MEM_HEADS), cache_mem_k, cache_mem_v, l)
        xs = _mmres(xs, acs.reshape(nb, d), wco[l])
        fin = final_norm if last else None
        xp = _ffn(xp, norms[l, 3], wg[l, 1], wu[l, 1], wd[l, 1], fin)
        xs = _ffn(xs, norms[l, 3], wg[l, 1], wu[l, 1], wd[l, 1], fin)

    return (xp.reshape(batch, seq, d), xs.reshape(nb, 1, d),
            jnp.stack(p_ret), jnp.stack(p_pool), jnp.stack(p_k), jnp.stack(p_v), jnp.stack(p_mk), jnp.stack(p_mv),
            jnp.stack(s_ret), jnp.stack(s_pool), jnp.stack(s_k), jnp.stack(s_v))
```

```python
import functools
import math

import jax
import jax.numpy as jnp
from jax import lax
from jax.experimental import pallas as pl
from jax.experimental.pallas import tpu as pltpu

f32 = jnp.float32
bf16 = jnp.bfloat16
i32 = jnp.int32

EPS = 1e-6
NEG_INF = -1e30
LOG2E = math.log2(math.e)

RET_HEADS = 4
RET_DK = 128
POOL_WINDOWS = (2, 4, 8, 16)
POOL_KEEP = max(POOL_WINDOWS) - 1
DIFF_HEADS = 8
DIFF_D = 64
DIFF_HD = 2 * DIFF_D
REL_BUCKETS = 32
REL_MAX_DIST = 128
MEM_HEADS = 4
PAGE_SIZE = 128

VMEM_LIMIT_BYTES = 56 * 1024 * 1024
LANES = 128
SUBLANES = 8

ROW_TILE = 512
ATT_TILE = 512
ATT_QUERY_BLOCK = 512
PAGES_PER_STEP = 16


def _params(*sem):
    return pltpu.CompilerParams(dimension_semantics=sem, vmem_limit_bytes=VMEM_LIMIT_BYTES)


def _rms(x, g):
    return x * lax.rsqrt(jnp.mean(x * x, axis=-1, keepdims=True) + EPS) * g


def _dot(a, b):
    return jnp.dot(a, b, preferred_element_type=f32)


def _dot_nt(a, b):
    return lax.dot_general(a, b, (((1,), (1,)), ((), ())), preferred_element_type=f32)


def _dot_tn(a, b):
    return lax.dot_general(a, b, (((0,), (0,)), ((), ())), preferred_element_type=f32)


def _resident(shape):
    nd = len(shape)
    return pl.BlockSpec(shape, lambda *_: (0,) * nd, pipeline_mode=pl.Buffered(1))


def _ffn_kernel(x_ref, g_ref, wg_ref, wu_ref, wd_ref, *rest, fchunk, final):
    o_ref = rest[-1]
    x = x_ref[...]
    h = _rms(x, g_ref[...]).astype(bf16)
    acc = jnp.zeros(x.shape, f32)
    for c0 in range(0, wg_ref.shape[1], fchunk):
        gate = _dot(h, wg_ref[:, c0:c0 + fchunk])
        up = _dot(h, wu_ref[:, c0:c0 + fchunk])
        a = (jax.nn.silu(gate) * up).astype(bf16)
        acc = acc + _dot(a, wd_ref[c0:c0 + fchunk, :])
    y = x + 0.5 * acc
    if final:
        y = _rms(y, rest[0][...])
    o_ref[...] = y


def _ffn(x, g, wg, wu, wd, final_g=None):
    m, d = x.shape
    f = wg.shape[1]
    tm = min(ROW_TILE, m)
    fchunk = f // 2 if (f // 2) % LANES == 0 else f
    final = final_g is not None
    in_specs = [pl.BlockSpec((tm, d), lambda i: (i, 0)), _resident((1, d)),
                _resident((d, f)), _resident((d, f)), _resident((f, d))]
    args = [x, g.reshape(1, d), wg, wu, wd]
    if final:
        in_specs.append(_resident((1, d)))
        args.append(final_g.reshape(1, d))
    return pl.pallas_call(
        functools.partial(_ffn_kernel, fchunk=fchunk, final=final),
        grid=(m // tm,),
        in_specs=in_specs,
        out_specs=pl.BlockSpec((tm, d), lambda i: (i, 0)),
        out_shape=jax.ShapeDtypeStruct((m, d), f32),
        compiler_params=_params("arbitrary"),
        name="ffn",
    )(*args)


def _proj_kernel(*refs, norm, groups, transposed, nchunk):
    x_ref = refs[0]
    g_ref = refs[1] if norm else None
    w_ref = refs[2] if norm else refs[1]
    o_refs = refs[(3 if norm else 2):]
    x = x_ref[...]
    h = (_rms(x, g_ref[...]) if norm else x).astype(bf16)
    for off, width, targets in groups:
        for c0 in range(0, width, nchunk):
            n = min(nchunk, width - c0)
            r = _dot(h, w_ref[:, off + c0:off + c0 + n])
            for oi, scale in targets:
                o_ref = o_refs[oi]
                val = (r if scale == 1.0 else r * scale)
                if transposed[oi]:
                    o_ref[c0:c0 + n, :] = val.T.astype(o_ref.dtype)
                else:
                    o_ref[:, c0:c0 + n] = val.astype(o_ref.dtype)


def _proj(x, g, w, groups, outs, seq=None):
    m, d = x.shape
    n = w.shape[1]
    tm = min(ROW_TILE, m)
    norm = g is not None
    in_specs = [pl.BlockSpec((tm, d), lambda i: (i, 0))]
    args = [x]
    if norm:
        in_specs.append(_resident((1, d)))
        args.append(g.reshape(1, d))
    in_specs.append(_resident((d, n)))
    args.append(w)
    out_specs, out_shape = [], []
    for wd, dt, *flag in outs:
        if flag:
            per = seq // tm
            out_specs.append(pl.BlockSpec((wd, tm), lambda i: (i // per, i % per)))
            out_shape.append(jax.ShapeDtypeStruct((m // seq * wd, seq), dt))
        else:
            out_specs.append(pl.BlockSpec((tm, wd), lambda i: (i, 0)))
            out_shape.append(jax.ShapeDtypeStruct((m, wd), dt))
    return pl.pallas_call(
        functools.partial(_proj_kernel, norm=norm, groups=groups, transposed=tuple(len(o) == 3 for o in outs),
                          nchunk=512),
        grid=(m // tm,),
        in_specs=in_specs,
        out_specs=out_specs,
        out_shape=out_shape,
        compiler_params=_params("arbitrary"),
        name="proj",
    )(*args)


def _mmres_kernel(x_ref, a_ref, w_ref, o_ref):
    o_ref[...] = x_ref[...] + _dot(a_ref[...].astype(bf16), w_ref[...])


def _mmres(x, a, w):
    m, d = x.shape
    k = a.shape[1]
    tm = min(ROW_TILE, m)
    return pl.pallas_call(
        _mmres_kernel,
        grid=(m // tm,),
        in_specs=[pl.BlockSpec((tm, d), lambda i: (i, 0)), pl.BlockSpec((tm, k), lambda i: (i, 0)),
                  _resident((k, d))],
        out_specs=pl.BlockSpec((tm, d), lambda i: (i, 0)),
        out_shape=jax.ShapeDtypeStruct((m, d), f32),
        compiler_params=_params("arbitrary"),
        name="mmres",
    )(x, a, w)


def _ret_log_gamma(h):
    return math.log1p(-(2.0 ** (-5.0 - h)))


def _rotate(x, cosf, sinf):
    return x * cosf + pltpu.roll(x, x.shape[-1] // 2, axis=1) * sinf


def _group_norm_gate(o, gain, gate):
    mu = jnp.mean(o, axis=-1, keepdims=True)
    var = jnp.mean(jnp.square(o - mu), axis=-1, keepdims=True)
    return jax.nn.silu(gate) * ((o - mu) * lax.rsqrt(var + EPS) * gain)


def _even_prompt_kernel(zq_ref, zk_ref, zv_ref, zg_ref, zp_ref, cos_ref, sin_ref, gain_ref, pw_ref, ps_ref,
                        mix_ref, sret_ref, spool_ref, state_ref, ext_ref, intra_ref, *, t):
    b = pl.program_id(0)
    c = pl.program_id(1)
    nc = pl.num_programs(1)
    halo = POOL_KEEP + 1
    dk = RET_DK

    @pl.when(jnp.logical_and(b == 0, c == 0))
    def _():
        rel = lax.broadcasted_iota(i32, (t, t), 0) - lax.broadcasted_iota(i32, (t, t), 1)
        relf = jnp.maximum(rel, 0).astype(f32)
        for h in range(RET_HEADS):
            intra_ref[h] = jnp.where(rel >= 0, jnp.exp(_ret_log_gamma(h) * relf), 0.0)

    @pl.when(c == 0)
    def _():
        state_ref[...] = jnp.zeros(state_ref.shape, f32)
        ext_ref[0:halo, :] = jnp.zeros((halo, ext_ref.shape[1]), f32)

    cosf = cos_ref[...]
    sinf = sin_ref[...]
    row = lax.broadcasted_iota(i32, (t, 1), 0).astype(f32)
    for h in range(RET_HEADS):
        sl = slice(h * dk, (h + 1) * dk)
        lg = _ret_log_gamma(h)
        qr = _rotate(zq_ref[:, sl], cosf, sinf)
        kr = _rotate(zk_ref[:, sl], cosf, sinf) * (dk ** -0.5)
        qb = qr.astype(bf16)
        vb = zv_ref[:, sl].astype(bf16)
        att = _dot_nt(qb, kr.astype(bf16)) * intra_ref[h]
        s_old = state_ref[h]
        o = _dot(att.astype(bf16), vb) + _dot(qb, s_old.astype(bf16)) * jnp.exp(lg * (row + 1.0))
        k_dec = jnp.exp(lg * (float(t - 1) - row))
        state_ref[h] = s_old * math.exp(lg * t) + _dot_tn((kr * k_dec).astype(bf16), vb)
        mix_ref[:, sl] = _group_norm_gate(o, gain_ref[:, sl], zg_ref[:, sl]).astype(mix_ref.dtype)

    p = zp_ref[...]
    ext_ref[halo:halo + t, :] = p
    pos = c * t + lax.broadcasted_iota(i32, (t, 1), 0)
    gw = p.shape[1] // len(POOL_WINDOWS)
    ret_w = RET_HEADS * dk
    for gi, w in enumerate(POOL_WINDOWS):
        sl = slice(gi * gw, (gi + 1) * gw)
        win = p[:, sl]
        for back in range(1, w):
            win = win + ext_ref[halo - back:halo - back + t, sl]
        cnt = jnp.minimum(w, pos + 1).astype(f32)
        pooled = win / cnt - p[:, sl]
        mixed = _dot(pooled.astype(bf16), pw_ref[gi]) * ps_ref[:, sl]
        mix_ref[:, ret_w + gi * gw:ret_w + (gi + 1) * gw] = mixed.astype(mix_ref.dtype)

    @pl.when(c == nc - 1)
    def _():
        sret_ref[0, 0] = state_ref[...]
        spool_ref[0, 0] = ext_ref[t + 1:t + halo, :]

    ext_ref[0:halo, :] = ext_ref[t:t + halo, :]


def _even_prompt(z, cosf, sinf, gain, pw, ps, batch, seq):
    t = ROW_TILE
    nc = seq // t
    rw = RET_HEADS * RET_DK
    pwid = ps.shape[0]
    zspec = lambda j: pl.BlockSpec((t, rw), lambda b, c: (b * nc + c, j))
    return pl.pallas_call(
        functools.partial(_even_prompt_kernel, t=t),
        grid=(batch, nc),
        in_specs=[zspec(0), zspec(1), zspec(2), zspec(3), zspec(4),
                  pl.BlockSpec((t, RET_DK), lambda b, c: (c, 0)), pl.BlockSpec((t, RET_DK), lambda b, c: (c, 0)),
                  _resident((1, rw)), _resident(pw.shape), _resident((1, pwid))],
        out_specs=[pl.BlockSpec((t, rw + pwid), lambda b, c: (b * nc + c, 0)),
                   pl.BlockSpec((1, 1, RET_HEADS, RET_DK, RET_DK), lambda b, c: (0, b, 0, 0, 0)),
                   pl.BlockSpec((1, 1, POOL_KEEP, pwid), lambda b, c: (0, b, 0, 0))],
        out_shape=[jax.ShapeDtypeStruct((batch * seq, rw + pwid), bf16),
                   jax.ShapeDtypeStruct((1, batch, RET_HEADS, RET_DK, RET_DK), f32),
                   jax.ShapeDtypeStruct((1, batch, POOL_KEEP, pwid), f32)],
        scratch_shapes=[pltpu.VMEM((RET_HEADS, RET_DK, RET_DK), f32),
                        pltpu.VMEM((POOL_KEEP + 1 + t, pwid), f32),
                        pltpu.VMEM((RET_HEADS, t, t), f32)],
        compiler_params=_params("arbitrary", "arbitrary"),
        name="even_prompt",
    )(z, z, z, z, z, cosf, sinf, gain.reshape(1, rw), pw, ps.reshape(1, pwid))


def _even_sample_kernel(z_ref, cos_ref, sin_ref, s0_ref, rows_ref, gain_ref, pw_ref, ps_ref,
                        mix_ref, sret_ref, spool_ref, ext_ref, *, pos):
    dk = RET_DK
    rw = RET_HEADS * dk
    z = jnp.broadcast_to(z_ref[0], (SUBLANES, z_ref.shape[2]))
    cosf = jnp.broadcast_to(cos_ref[...], (SUBLANES, dk))
    sinf = jnp.broadcast_to(sin_ref[...], (SUBLANES, dk))
    first = lax.broadcasted_iota(i32, (SUBLANES, dk), 0) == 0
    for h in range(RET_HEADS):
        sl = slice(h * dk, (h + 1) * dk)
        gamma = math.exp(_ret_log_gamma(h))
        qr = _rotate(z[:, sl], cosf, sinf)
        kr = _rotate(z[:, rw + h * dk:rw + (h + 1) * dk], cosf, sinf) * (dk ** -0.5)
        v = z[:, 2 * rw + h * dk:2 * rw + (h + 1) * dk]
        s_new = s0_ref[0, 0, h] * gamma + _dot_tn(jnp.where(first, kr, 0.0), v)
        sret_ref[0, 0, h] = s_new
        o = _dot(qr, s_new)
        gate = z[:, 3 * rw + h * dk:3 * rw + (h + 1) * dk]
        mix_ref[0, :, sl] = _group_norm_gate(o, gain_ref[:, sl], gate)[0:1]

    keep = POOL_KEEP
    pwid = ps_ref.shape[1]
    p = z[0:1, 4 * rw:4 * rw + pwid]
    ext_ref[0:keep, :] = rows_ref[0, 0]
    ext_ref[keep:keep + 1, :] = p
    spool_ref[0, 0] = ext_ref[1:keep + 1, :]
    gw = pwid // len(POOL_WINDOWS)
    for gi, w in enumerate(POOL_WINDOWS):
        sl = slice(gi * gw, (gi + 1) * gw)
        win = jnp.sum(ext_ref[keep + 1 - w:keep + 1, sl], axis=0, keepdims=True)
        pooled = win / float(min(w, pos + 1)) - p[:, sl]
        mixed = _dot(jnp.broadcast_to(pooled, (SUBLANES, gw)), pw_ref[gi])[0:1] * ps_ref[:, sl]
        mix_ref[0, :, rw + gi * gw:rw + (gi + 1) * gw] = mixed


def _even_sample(z, cosf, sinf, state_ret, state_pool, gain, pw, ps, pos):
    nb, zw = z.shape
    rw = RET_HEADS * RET_DK
    pwid = ps.shape[0]
    return pl.pallas_call(
        functools.partial(_even_sample_kernel, pos=pos),
        grid=(nb,),
        in_specs=[pl.BlockSpec((1, 1, zw), lambda b: (b, 0, 0)),
                  _resident((1, RET_DK)), _resident((1, RET_DK)),
                  pl.BlockSpec((1, 1, RET_HEADS, RET_DK, RET_DK), lambda b: (0, b, 0, 0, 0)),
                  pl.BlockSpec((1, 1, POOL_KEEP, pwid), lambda b: (0, b, 0, 0)),
                  _resident((1, rw)), _resident(pw.shape), _resident((1, pwid))],
        out_specs=[pl.BlockSpec((1, 1, rw + pwid), lambda b: (b, 0, 0)),
                   pl.BlockSpec((1, 1, RET_HEADS, RET_DK, RET_DK), lambda b: (0, b, 0, 0, 0)),
                   pl.BlockSpec((1, 1, POOL_KEEP, pwid), lambda b: (0, b, 0, 0))],
        out_shape=[jax.ShapeDtypeStruct((nb, 1, rw + pwid), f32),
                   jax.ShapeDtypeStruct((1, nb, RET_HEADS, RET_DK, RET_DK), f32),
                   jax.ShapeDtypeStruct((1, nb, POOL_KEEP, pwid), f32)],
        scratch_shapes=[pltpu.VMEM((POOL_KEEP + 1, pwid), f32)],
        compiler_params=_params("arbitrary"),
        name="even_sample",
    )(z.reshape(nb, 1, zw), cosf, sinf, state_ret, state_pool, gain.reshape(1, rw), pw, ps.reshape(1, pwid))


def _rel_bias_of(dist, rb_ref, h):
    n = jnp.maximum(dist, 0)
    max_exact = REL_BUCKETS // 2
    nf = jnp.maximum(n, 1).astype(f32)
    large = max_exact + (jnp.log(nf / max_exact) / math.log(REL_MAX_DIST / max_exact)
                         * (REL_BUCKETS - max_exact)).astype(i32)
    large = jnp.minimum(large, REL_BUCKETS - 1)
    bucket = jnp.where(n < max_exact, n, large)
    out = jnp.zeros(dist.shape, f32)
    for bk in range(REL_BUCKETS):
        out = jnp.where(bucket == bk, rb_ref[bk, h], out)
    return out


def _bias_kernel(rb_ref, lq1_ref, lk1_ref, lq2_ref, lk2_ref, tile_ref, tab_ref, lam_ref, *, t, lam_init):
    h = pl.program_id(0)
    rel = lax.broadcasted_iota(i32, (t, t), 1) - lax.broadcasted_iota(i32, (t, t), 0)
    diag = jnp.where(rel >= 0, _rel_bias_of(rel, rb_ref, h) * LOG2E, NEG_INF)
    below = _rel_bias_of(rel + t, rb_ref, h) * LOG2E
    tile_ref[0, 0, :, 0:t] = diag
    tile_ref[0, 0, :, t:2 * t] = diag
    tile_ref[0, 1, :, 0:t] = below
    tile_ref[0, 1, :, t:2 * t] = below
    tab_ref[0] = _rel_bias_of(lax.broadcasted_iota(i32, (1, 2 * REL_MAX_DIST), 1), rb_ref, h)
    lam = (jnp.exp(jnp.sum(lq1_ref[...] * lk1_ref[...], axis=-1, keepdims=True))
           - jnp.exp(jnp.sum(lq2_ref[...] * lk2_ref[...], axis=-1, keepdims=True)) + lam_init)
    lam_ref[...] = jnp.broadcast_to(lam, lam_ref.shape)


def _bias_tables(rel_bias, lq1, lk1, lq2, lk2, lam_init, t):
    nh = rel_bias.shape[1]
    d = lq1.shape[0]
    vec = lambda a: a.reshape(1, d)
    return pl.pallas_call(
        functools.partial(_bias_kernel, t=t, lam_init=lam_init),
        grid=(nh,),
        in_specs=[pl.BlockSpec(memory_space=pltpu.SMEM)] + [_resident((1, d))] * 4,
        out_specs=[pl.BlockSpec((1, 2, t, 2 * t), lambda h: (h, 0, 0, 0)),
                   pl.BlockSpec((1, 1, 2 * REL_MAX_DIST), lambda h: (h, 0, 0)),
                   pl.BlockSpec((SUBLANES, LANES), lambda h: (0, 0))],
        out_shape=[jax.ShapeDtypeStruct((nh, 2, t, 2 * t), f32),
                   jax.ShapeDtypeStruct((nh, 1, 2 * REL_MAX_DIST), f32),
                   jax.ShapeDtypeStruct((SUBLANES, LANES), f32)],
        compiler_params=_params("arbitrary"),
        name="bias_tables",
    )(rel_bias, vec(lq1), vec(lk1), vec(lq2), vec(lk2))


def _sub_ln(o, gain, lam_init):
    return o * lax.rsqrt(jnp.mean(o * o, axis=-1, keepdims=True) + EPS) * gain * (1.0 - lam_init)


def _attn_prompt_kernel(lam_ref, far_ref, q_ref, k_ref, vt_ref, tile_ref, gain_ref, o_ref,
                        qq_ref, m_ref, l_ref, acc_ref, s0_ref, smax0_ref, s1_ref, smax1_ref, *, t, cb, lam_init):
    h = pl.program_id(1)
    qi = pl.program_id(2)
    q = q_ref[...]
    lane = lax.broadcasted_iota(i32, q.shape, 1)
    zero = jnp.zeros_like(q)
    qq_ref[0:t, :] = jnp.where(lane < DIFF_D, q, zero)
    qq_ref[t:2 * t, :] = jnp.where(lane >= DIFF_D, q, zero)
    m_ref[...] = jnp.full(m_ref.shape, NEG_INF, f32)
    l_ref[...] = jnp.zeros(l_ref.shape, f32)
    acc_ref[...] = jnp.zeros(acc_ref.shape, f32)

    far = far_ref[h] * LOG2E

    blocks = [slice(c0, c0 + cb) for c0 in range(0, 2 * t, cb)]

    parked = ((s0_ref, smax0_ref), (s1_ref, smax1_ref))

    def scores(ki, slot):
        s_ref, smax_ref = parked[slot]
        kt = k_ref[pl.ds(pl.multiple_of(ki * t, t), t), :]
        for cols in blocks:
            s = _dot_nt(kt, qq_ref[cols, :])
            s_ref[:, cols] = s
            smax_ref[:, cols] = jnp.max(s, axis=0, keepdims=True)

    def absorb(ki, slot, tile):
        s_ref, smax_ref = parked[slot]
        vt = vt_ref[:, pl.ds(pl.multiple_of(ki * t, t), t)]
        for cols in blocks:
            m_old = m_ref[:, cols]
            if tile is None:
                m_new = jnp.maximum(m_old, smax_ref[:, cols] + far)
                p = jnp.exp2(s_ref[:, cols] - (m_new - far))
            else:
                s = s_ref[:, cols] + tile_ref[0, tile, :, cols]
                m_new = jnp.maximum(m_old, jnp.max(s, axis=0, keepdims=True))
                p = jnp.exp2(s - m_new)
            alpha = jnp.exp2(m_old - m_new)
            l_ref[:, cols] = alpha * l_ref[:, cols] + jnp.sum(p, axis=0, keepdims=True)
            acc_ref[:, cols] = alpha * acc_ref[:, cols] + _dot(vt, p.astype(bf16))
            m_ref[:, cols] = m_new

    def stage(ki, slot, tile):
        scores(ki + 1, 1 - slot)
        absorb(ki, slot, tile)

    n_far = jnp.maximum(qi - 1, 0)
    scores(0, 0)

    def far_pair(j, carry):
        stage(2 * j, 0, None)
        stage(2 * j + 1, 1, None)
        return carry

    lax.fori_loop(0, n_far // 2, far_pair, 0)

    @pl.when(n_far % 2 == 1)
    def _():
        stage(n_far - 1, 0, None)

    for slot in (0, 1):
        @pl.when(jnp.logical_and(qi >= 1, n_far % 2 == slot))
        def _():
            stage(qi - 1, slot, 1)

    for slot in (0, 1):
        @pl.when(qi % 2 == slot)
        def _():
            absorb(qi, slot, 0)

    o = acc_ref[...] / l_ref[...]
    o = o[:, 0:t] - lam_ref[0] * o[:, t:2 * t]
    y = o * lax.rsqrt(jnp.mean(o * o, axis=0, keepdims=True) + EPS) * gain_ref[...] * (1.0 - lam_init)
    o_ref[...] = y.T.astype(o_ref.dtype)


def _attn_prompt(q, k, vt, tiles, far, lam, gain, batch, seq, lam_init):
    t = ATT_TILE
    nq = seq // t
    hd = DIFF_HD
    nh = DIFF_HEADS
    smem = pl.BlockSpec(memory_space=pltpu.SMEM)
    return pl.pallas_call(
        functools.partial(_attn_prompt_kernel, t=t, cb=ATT_QUERY_BLOCK, lam_init=lam_init),
        grid=(batch, nh, nq),
        in_specs=[smem, smem,
                  pl.BlockSpec((t, hd), lambda b, h, i: (b * nq + i, h)),
                  pl.BlockSpec((seq, hd), lambda b, h, i: (b, h)),
                  pl.BlockSpec((hd, seq), lambda b, h, i: (b * nh + h, 0)),
                  pl.BlockSpec((1, 2, t, 2 * t), lambda b, h, i: (h, 0, 0, 0)),
                  _resident((hd, 1))],
        out_specs=pl.BlockSpec((t, hd), lambda b, h, i: (b * nq + i, h)),
        out_shape=jax.ShapeDtypeStruct((batch * seq, nh * hd), bf16),
        scratch_shapes=[pltpu.VMEM((2 * t, hd), bf16), pltpu.VMEM((1, 2 * t), f32),
                        pltpu.VMEM((1, 2 * t), f32), pltpu.VMEM((hd, 2 * t), f32),
                        pltpu.VMEM((t, 2 * t), f32), pltpu.VMEM((1, 2 * t), f32),
                        pltpu.VMEM((t, 2 * t), f32), pltpu.VMEM((1, 2 * t), f32)],
        compiler_params=_params("arbitrary", "arbitrary", "arbitrary"),
        name="attn_prompt",
    )(lam, far, q, k, vt, tiles, gain.reshape(hd, 1))


def _attn_decode_kernel(pt_ref, lam_ref, q_ref, kn_ref, vn_ref, bias_ref, b0_ref, gain_ref, *rest,
                        npg, lam_init):
    k_refs = rest[:npg]
    v_refs = rest[npg:2 * npg]
    o_ref = rest[2 * npg]
    qh_ref, kbuf_ref, vbuf_ref, m_ref, l_ref, acc_ref = rest[2 * npg + 1:]
    c = pl.program_id(1)
    nc = pl.num_programs(1)
    nh = DIFF_HEADS
    prow = PAGE_SIZE * nh

    def both_streams(x):
        return jnp.concatenate([x, x], axis=0)

    @pl.when(c == 0)
    def _():
        q = q_ref[0]
        lane = lax.broadcasted_iota(i32, q.shape, 1)
        qh = jnp.concatenate([jnp.where(lane < DIFF_D, q, 0.0), jnp.where(lane >= DIFF_D, q, 0.0)],
                             axis=0).astype(bf16)
        qh_ref[...] = qh
        kn = both_streams(kn_ref[0]).astype(bf16).astype(f32)
        m_ref[...] = jnp.sum(qh.astype(f32) * kn, axis=-1, keepdims=True) + b0_ref[...]
        l_ref[...] = jnp.ones(l_ref.shape, f32)
        acc_ref[...] = both_streams(vn_ref[0]).astype(bf16).astype(f32)

    for i in range(npg):
        kbuf_ref[i * prow:(i + 1) * prow, :] = k_refs[i][...].astype(bf16)
        vbuf_ref[i * prow:(i + 1) * prow, :] = v_refs[i][...].astype(bf16)

    s = _dot_nt(qh_ref[...], kbuf_ref[...]) + bias_ref[(c == nc - 1).astype(i32)]
    m_old = m_ref[...]
    m_new = jnp.maximum(m_old, jnp.max(s, axis=-1, keepdims=True))
    alpha = jnp.exp(m_old - m_new)
    p = jnp.exp(s - m_new)
    l_ref[...] = alpha * l_ref[...] + jnp.sum(p, axis=-1, keepdims=True)
    acc_ref[...] = alpha * acc_ref[...] + _dot(p.astype(bf16), vbuf_ref[...])
    m_ref[...] = m_new

    @pl.when(c == nc - 1)
    def _():
        on = acc_ref[...] / l_ref[...]
        o_ref[0] = _sub_ln(on[0:nh] - lam_ref[0] * on[nh:2 * nh], gain_ref[...], lam_init)


def _attn_decode(q, kn, vn, cache_k, cache_v, layer, page_table, bias2, b0, lam, gain, lam_init):
    nb, nh, hd = q.shape
    pool = cache_k.shape[1]
    npages = page_table.shape[1]
    npg = PAGES_PER_STEP
    nc = npages // npg
    nrow = 2 * nh
    prow = PAGE_SIZE * nh
    page_spec = lambda i: pl.BlockSpec((prow, hd), lambda b, c, pt: (layer * pool + pt[b, c * npg + i], 0))
    rowspec = pl.BlockSpec((1, nh, hd), lambda b, c, pt: (b, 0, 0))
    const = lambda shape: pl.BlockSpec(shape, lambda b, c, pt: (0,) * len(shape))
    grid_spec = pltpu.PrefetchScalarGridSpec(
        num_scalar_prefetch=1,
        grid=(nb, nc),
        in_specs=[pl.BlockSpec(memory_space=pltpu.SMEM), rowspec, rowspec, rowspec,
                  const((2, nrow, npg * prow)), const((nrow, 1)), const((1, hd))]
                 + [page_spec(i) for i in range(npg)] * 2,
        out_specs=rowspec,
        scratch_shapes=[pltpu.VMEM((nrow, hd), bf16), pltpu.VMEM((npg * prow, hd), bf16),
                        pltpu.VMEM((npg * prow, hd), bf16), pltpu.VMEM((nrow, 1), f32),
                        pltpu.VMEM((nrow, 1), f32), pltpu.VMEM((nrow, hd), f32)],
    )
    ck = cache_k.reshape(-1, hd)
    cv = cache_v.reshape(-1, hd)
    return pl.pallas_call(
        functools.partial(_attn_decode_kernel, npg=npg, lam_init=lam_init),
        grid_spec=grid_spec,
        out_shape=jax.ShapeDtypeStruct((nb, nh, hd), f32),
        compiler_params=_params("arbitrary", "arbitrary"),
        name="attn_decode",
    )(page_table, lam, q, kn, vn, bias2, b0, gain.reshape(1, hd), *([ck] * npg), *([cv] * npg))


def _softmax_rows(s):
    e = jnp.exp(s - jnp.max(s, axis=-1, keepdims=True))
    return e / jnp.sum(e, axis=-1, keepdims=True)


def _cross_prompt_kernel(x_ref, g_ref, wq_ref, mk_ref, mv_ref, wo_ref, o_ref, a_ref):
    x = x_ref[...]
    d = x.shape[1]
    hd = d // MEM_HEADS
    h = _rms(x, g_ref[...]).astype(bf16)
    q = (_dot(h, wq_ref[...]) * (hd ** -0.5)).astype(bf16)
    for hh in range(MEM_HEADS):
        sl = slice(hh * hd, (hh + 1) * hd)
        a = _softmax_rows(_dot_nt(q[:, sl], mk_ref[:, sl])).astype(bf16)
        a_ref[:, sl] = _dot(a, mv_ref[:, sl]).astype(bf16)
    o_ref[...] = x + _dot(a_ref[...], wo_ref[...])


def _cross_prompt(x, g, wq, mk, mv, wo, seq):
    m, d = x.shape
    tm = ROW_TILE
    per = seq // tm
    mem = mk.shape[0] // (m // seq)
    return pl.pallas_call(
        _cross_prompt_kernel,
        grid=(m // tm,),
        in_specs=[pl.BlockSpec((tm, d), lambda i: (i, 0)), _resident((1, d)), _resident((d, d)),
                  pl.BlockSpec((mem, d), lambda i: (i // per, 0)), pl.BlockSpec((mem, d), lambda i: (i // per, 0)),
                  _resident((d, d))],
        out_specs=pl.BlockSpec((tm, d), lambda i: (i, 0)),
        out_shape=jax.ShapeDtypeStruct((m, d), f32),
        scratch_shapes=[pltpu.VMEM((tm, d), bf16)],
        compiler_params=_params("arbitrary"),
        name="cross_prompt",
    )(x, g.reshape(1, d), wq, mk, mv, wo)


def _cross_sample_kernel(q_ref, mk_ref, mv_ref, mask_ref, o_ref):
    nh, hd = q_ref.shape[1], q_ref.shape[2]
    q = jnp.concatenate([q_ref[0] * (hd ** -0.5), jnp.zeros((SUBLANES - nh, hd), f32)], axis=0)
    s = _dot_nt(q.astype(bf16), mk_ref[...].astype(bf16)) + mask_ref[...]
    o = _dot(_softmax_rows(s).astype(bf16), mv_ref[...].astype(bf16))
    o_ref[0] = o[0:nh]


def _cross_sample(q, cache_mk, cache_mv, layer):
    nb, nh, hd = q.shape
    rows = cache_mk.shape[2] * nh
    head_of_row = jnp.arange(rows, dtype=i32) % nh
    mask = jnp.where(head_of_row[None, :] == jnp.arange(SUBLANES, dtype=i32)[:, None], 0.0, NEG_INF).astype(f32)
    mask = mask.at[nh:].set(0.0)
    mem_spec = pl.BlockSpec((rows, hd), lambda b: (layer * nb + b, 0))
    return pl.pallas_call(
        _cross_sample_kernel,
        grid=(nb,),
        in_specs=[pl.BlockSpec((1, nh, hd), lambda b: (b, 0, 0)), mem_spec, mem_spec, _resident((SUBLANES, rows))],
        out_specs=pl.BlockSpec((1, nh, hd), lambda b: (b, 0, 0)),
        out_shape=jax.ShapeDtypeStruct((nb, nh, hd), f32),
        compiler_params=_params("arbitrary"),
        name="cross_sample",
    )(q, cache_mk.reshape(-1, hd), cache_mv.reshape(-1, hd), mask)


def _rotary_tables(pos, half):
    inv = 1.0 / (10000.0 ** jnp.linspace(0.0, 1.0, half, dtype=f32))
    ang = pos.astype(f32)[:, None] * inv[None, :]
    cos, sin = jnp.cos(ang), jnp.sin(ang)
    return jnp.concatenate([cos, cos], axis=-1), jnp.concatenate([-sin, sin], axis=-1)


def kernel(x_prompt, x_sample, state_ret, state_pool, cache_k_diff, cache_v_diff, cache_mem_k, cache_mem_v, page_table, mem_prompt, norms, final_norm, ffn_w_gate, ffn_w_up, ffn_w_down, w_in_even, ret_gain, pool_w, pool_scale, w_out_even, w_qkv_odd, lambda_q1, lambda_k1, lambda_q2, lambda_k2, subln_gain, w_out_odd, rel_bias, w_cq, w_ckv, w_co):
    batch, seq, d = x_prompt.shape
    nb = x_sample.shape[0]
    depth = norms.shape[0]
    past = page_table.shape[1] * PAGE_SIZE
    mem_len = mem_prompt.shape[1]
    rw = RET_HEADS * RET_DK
    dw = DIFF_HEADS * DIFF_HD

    xp = x_prompt.reshape(batch * seq, d)
    xs = x_sample.reshape(nb, d)
    mem2d = mem_prompt.reshape(batch * mem_len, d)

    cos_p, sin_p = _rotary_tables(jnp.arange(seq, dtype=i32), RET_DK // 2)
    cos_s, sin_s = _rotary_tables(jnp.full((1,), past, dtype=i32), RET_DK // 2)

    wg, wu, wd = ffn_w_gate.astype(bf16), ffn_w_up.astype(bf16), ffn_w_down.astype(bf16)
    w_in, w_oe = w_in_even.astype(bf16), w_out_even.astype(bf16)
    w_qkv, w_oo = w_qkv_odd.astype(bf16), w_out_odd.astype(bf16)
    wcq, wckv, wco = w_cq.astype(bf16), w_ckv.astype(bf16), w_co.astype(bf16)
    pw = pool_w.astype(bf16)

    p_ret, p_pool, p_k, p_v, p_mk, p_mv = [], [], [], [], [], []
    s_ret, s_pool, s_k, s_v = [], [], [], []
    for l in range(depth):
        last = l == depth - 1
        xp = _ffn(xp, norms[l, 0], wg[l, 0], wu[l, 0], wd[l, 0])
        xs = _ffn(xs, norms[l, 0], wg[l, 0], wu[l, 0], wd[l, 0])
        if l % 2 == 0:
            e = l // 2
            zw = w_in.shape[2]
            (zp,) = _proj(xp, norms[l, 1], w_in[e], [(0, zw, [(0, 1.0)])], [(zw, f32)])
            (zs,) = _proj(xs, norms[l, 1], w_in[e], [(0, zw, [(0, 1.0)])], [(zw, f32)])
            mixp, rp, pp = _even_prompt(zp, cos_p, sin_p, ret_gain[e], pw[e], pool_scale[e], batch, seq)
            mixs, rs, ps = _even_sample(zs, cos_s, sin_s, state_ret[e:e + 1], state_pool[e:e + 1], ret_gain[e],
                                        pool_w[e], pool_scale[e], past)
            xp = _mmres(xp, mixp, w_oe[e])
            xs = _mmres(xs, mixs.reshape(nb, rw + pool_scale.shape[1]), w_oe[e])
            p_ret.append(rp[0])
            p_pool.append(pp[0])
            s_ret.append(rs[0])
            s_pool.append(ps[0])
        else:
            o = l // 2
            lam_init = 0.8 - 0.6 * math.exp(-0.3 * l)
            scale = DIFF_D ** -0.5
            tiles, tab, lam_t = _bias_tables(rel_bias, lambda_q1[o], lambda_k1[o], lambda_q2[o], lambda_k2[o],
                                             lam_init, ATT_TILE)
            lam = lam_t[0, 0:1]
            tab = tab[:, 0, :]
            far = tab[:, REL_MAX_DIST]
            qkv_groups = [(0, dw, [(0, scale * LOG2E)]), (dw, dw, [(1, 1.0), (3, 1.0)]),
                          (2 * dw, dw, [(2, 1.0), (4, 1.0)])]
            qb, kp, vp, kb, vtb = _proj(xp, norms[l, 1], w_qkv[o], qkv_groups,
                                        [(dw, bf16), (dw, f32), (dw, f32), (dw, bf16), (dw, bf16, "T")], seq=seq)
            attp = _attn_prompt(qb, kb, vtb, tiles, far, lam, subln_gain[o], batch, seq, lam_init)
            xp = _mmres(xp, attp, w_oo[o])
            qs, ks_, vs = _proj(xs, norms[l, 1], w_qkv[o],
                                [(0, dw, [(0, scale)]), (dw, dw, [(1, 1.0)]), (2 * dw, dw, [(2, 1.0)])],
                                [(dw, f32), (dw, f32), (dw, f32)])
            nh = DIFF_HEADS
            own = jnp.arange(nh)[:, None, None] == jnp.arange(nh)[None, None, :]
            far_page = jnp.where(own, far[:, None, None], NEG_INF) + jnp.zeros((1, PAGE_SIZE, 1), f32)
            last_page = jnp.where(own, tab[:, PAGE_SIZE:0:-1][:, :, None], NEG_INF)
            far_page, last_page = far_page.reshape(nh, -1), last_page.reshape(nh, -1)
            far_step = jnp.tile(far_page, (2, PAGES_PER_STEP))
            last_step = jnp.concatenate([far_step[:, :-PAGE_SIZE * nh], jnp.tile(last_page, (2, 1))], axis=1)
            bias2 = jnp.stack([far_step, last_step])
            b0 = jnp.tile(tab[:, 0], 2)[:, None]
            heads = lambda a: a.reshape(nb, nh, DIFF_HD)
            atts = _attn_decode(heads(qs), heads(ks_), heads(vs), cache_k_diff, cache_v_diff, o, page_table,
                                bias2, b0, lam, subln_gain[o], lam_init)
            xs = _mmres(xs, atts.reshape(nb, dw), w_oo[o])
            p_k.append(kp.reshape(batch, seq, DIFF_HEADS, DIFF_HD))
            p_v.append(vp.reshape(batch, seq, DIFF_HEADS, DIFF_HD))
            s_k.append(ks_.reshape(nb, 1, DIFF_HEADS, DIFF_HD))
            s_v.append(vs.reshape(nb, 1, DIFF_HEADS, DIFF_HD))
        mk, mv, mkb, mvb = _proj(mem2d, None, wckv[l], [(0, d, [(0, 1.0), (2, 1.0)]), (d, d, [(1, 1.0), (3, 1.0)])],
                                 [(d, f32), (d, f32), (d, bf16), (d, bf16)])
        p_mk.append(mk.reshape(batch, mem_len, MEM_HEADS, d // MEM_HEADS))
        p_mv.append(mv.reshape(batch, mem_len, MEM_HEADS, d // MEM_HEADS))
        xp = _cross_prompt(xp, norms[l, 2], wcq[l], mkb, mvb, wco[l], seq)
        (qcs,) = _proj(xs, norms[l, 2], wcq[l], [(0, d, [(0, 1.0)])], [(d, f32)])
        acs = _cross_sample(qcs.reshape(nb, MEM_HEADS, d // MEM_HEADS), cache_mem_k, cache_mem_v, l)
        xs = _mmres(xs, acs.reshape(nb, d), wco[l])
        fin = final_norm if last else None
        xp = _ffn(xp, norms[l, 3], wg[l, 1], wu[l, 1], wd[l, 1], fin)
        xs = _ffn(xs, norms[l, 3], wg[l, 1], wu[l, 1], wd[l, 1], fin)

    return (xp.reshape(batch, seq, d), xs.reshape(nb, 1, d),
            jnp.stack(p_ret), jnp.stack(p_pool), jnp.stack(p_k), jnp.stack(p_v), jnp.stack(p_mk), jnp.stack(p_mv),
            jnp.stack(s_ret), jnp.stack(s_pool), jnp.stack(s_k), jnp.stack(s_v))
```

```python
import functools
import math

import jax
import jax.numpy as jnp
from jax import lax
from jax.experimental import pallas as pl
from jax.experimental.pallas import tpu as pltpu

f32 = jnp.float32
bf16 = jnp.bfloat16
i32 = jnp.int32

EPS = 1e-6
NEG_INF = -1e30
LOG2E = math.log2(math.e)

RET_HEADS = 4
RET_DK = 128
POOL_WINDOWS = (2, 4, 8, 16)
POOL_KEEP = max(POOL_WINDOWS) - 1
DIFF_HEADS = 8
DIFF_D = 64
DIFF_HD = 2 * DIFF_D
REL_BUCKETS = 32
REL_MAX_DIST = 128
MEM_HEADS = 4
PAGE_SIZE = 128

VMEM_LIMIT_BYTES = 56 * 1024 * 1024
LANES = 128
SUBLANES = 8

ROW_TILE = 512
ATT_TILE = 512
ATT_QUERY_BLOCK = 512
PAGES_PER_STEP = 16


def _params(*sem):
    return pltpu.CompilerParams(dimension_semantics=sem, vmem_limit_bytes=VMEM_LIMIT_BYTES)


def _rms(x, g):
    return x * lax.rsqrt(jnp.mean(x * x, axis=-1, keepdims=True) + EPS) * g


def _dot(a, b):
    return jnp.dot(a, b, preferred_element_type=f32)


def _dot_nt(a, b):
    return lax.dot_general(a, b, (((1,), (1,)), ((), ())), preferred_element_type=f32)


def _dot_tn(a, b):
    return lax.dot_general(a, b, (((0,), (0,)), ((), ())), preferred_element_type=f32)


def _resident(shape, lead=()):
    nd = len(shape)
    return pl.BlockSpec((None,) * len(lead) + tuple(shape), lambda *_: tuple(lead) + (0,) * nd,
                        pipeline_mode=pl.Buffered(1))


def _stacked(w):
    arr, lead = w
    shape = arr.shape[len(lead):]
    return shape, _resident(shape, lead), arr


def _ffn_kernel(x_ref, g_ref, wg_ref, wu_ref, wd_ref, *rest, fchunk, final):
    o_ref = rest[-1]
    x = x_ref[...]
    h = _rms(x, g_ref[...]).astype(bf16)
    acc = jnp.zeros(x.shape, f32)
    for c0 in range(0, wg_ref.shape[1], fchunk):
        gate = _dot(h, wg_ref[:, c0:c0 + fchunk])
        up = _dot(h, wu_ref[:, c0:c0 + fchunk])
        a = (jax.nn.silu(gate) * up).astype(bf16)
        acc = acc + _dot(a, wd_ref[c0:c0 + fchunk, :])
    y = x + 0.5 * acc
    if final:
        y = _rms(y, rest[0][...])
    o_ref[...] = y


def _ffn(x, g, wg, wu, wd, final_g=None):
    m, d = x.shape
    (_, f), wg_spec, wg = _stacked(wg)
    _, wu_spec, wu = _stacked(wu)
    _, wd_spec, wd = _stacked(wd)
    tm = min(ROW_TILE, m)
    fchunk = f // 2 if (f // 2) % LANES == 0 else f
    final = final_g is not None
    in_specs = [pl.BlockSpec((tm, d), lambda i: (i, 0)), _resident((1, d)), wg_spec, wu_spec, wd_spec]
    args = [x, g.reshape(1, d), wg, wu, wd]
    if final:
        in_specs.append(_resident((1, d)))
        args.append(final_g.reshape(1, d))
    return pl.pallas_call(
        functools.partial(_ffn_kernel, fchunk=fchunk, final=final),
        grid=(m // tm,),
        in_specs=in_specs,
        out_specs=pl.BlockSpec((tm, d), lambda i: (i, 0)),
        out_shape=jax.ShapeDtypeStruct((m, d), f32),
        compiler_params=_params("arbitrary"),
        name="ffn",
    )(*args)


def _proj_kernel(*refs, norm, groups, transposed, nchunk):
    x_ref = refs[0]
    g_ref = refs[1] if norm else None
    w_ref = refs[2] if norm else refs[1]
    o_refs = refs[(3 if norm else 2):]
    x = x_ref[...]
    h = (_rms(x, g_ref[...]) if norm else x).astype(bf16)
    for off, width, targets in groups:
        for c0 in range(0, width, nchunk):
            n = min(nchunk, width - c0)
            r = _dot(h, w_ref[:, off + c0:off + c0 + n])
            for oi, scale in targets:
                o_ref = o_refs[oi]
                val = (r if scale == 1.0 else r * scale)
                if transposed[oi]:
                    o_ref[c0:c0 + n, :] = val.T.astype(o_ref.dtype)
                elif len(o_ref.shape) == 3:
                    hd = o_ref.shape[2]
                    o_ref[:, c0 // hd:(c0 + n) // hd, :] = val.reshape(val.shape[0], n // hd, hd).astype(o_ref.dtype)
                else:
                    o_ref[:, c0:c0 + n] = val.astype(o_ref.dtype)


def _proj(x, g, w, groups, outs, seq=None):
    m, d = x.shape
    _, w_spec, w = _stacked(w)
    tm = min(ROW_TILE, m)
    norm = g is not None
    in_specs = [pl.BlockSpec((tm, d), lambda i: (i, 0))]
    args = [x]
    if norm:
        in_specs.append(_resident((1, d)))
        args.append(g.reshape(1, d))
    in_specs.append(w_spec)
    args.append(w)
    out_specs, out_shape = [], []
    for wd, dt, *flag in outs:
        if flag and flag[0] != "T":
            hd = flag[0]
            out_specs.append(pl.BlockSpec((tm, wd // hd, hd), lambda i: (i, 0, 0)))
            out_shape.append(jax.ShapeDtypeStruct((m, wd // hd, hd), dt))
        elif flag:
            per = seq // tm
            out_specs.append(pl.BlockSpec((wd, tm), lambda i: (i // per, i % per)))
            out_shape.append(jax.ShapeDtypeStruct((m // seq * wd, seq), dt))
        else:
            out_specs.append(pl.BlockSpec((tm, wd), lambda i: (i, 0)))
            out_shape.append(jax.ShapeDtypeStruct((m, wd), dt))
    return pl.pallas_call(
        functools.partial(_proj_kernel, norm=norm, groups=groups, transposed=tuple(o[2:] == ("T",) for o in outs),
                          nchunk=512),
        grid=(m // tm,),
        in_specs=in_specs,
        out_specs=out_specs,
        out_shape=out_shape,
        compiler_params=_params("arbitrary"),
        name="proj",
    )(*args)


def _mmres_kernel(x_ref, a_ref, w_ref, o_ref):
    o_ref[...] = x_ref[...] + _dot(a_ref[...].astype(bf16), w_ref[...])


def _mmres(x, a, w):
    m, d = x.shape
    k = a.shape[1]
    _, w_spec, w = _stacked(w)
    tm = min(ROW_TILE, m)
    return pl.pallas_call(
        _mmres_kernel,
        grid=(m // tm,),
        in_specs=[pl.BlockSpec((tm, d), lambda i: (i, 0)), pl.BlockSpec((tm, k), lambda i: (i, 0)), w_spec],
        out_specs=pl.BlockSpec((tm, d), lambda i: (i, 0)),
        out_shape=jax.ShapeDtypeStruct((m, d), f32),
        compiler_params=_params("arbitrary"),
        name="mmres",
    )(x, a, w)


def _ret_log_gamma(h):
    return math.log1p(-(2.0 ** (-5.0 - h)))


def _rotate(x, cosf, sinf):
    return x * cosf + pltpu.roll(x, x.shape[-1] // 2, axis=1) * sinf


def _group_norm_gate(o, gain, gate):
    mu = jnp.mean(o, axis=-1, keepdims=True)
    var = jnp.mean(jnp.square(o - mu), axis=-1, keepdims=True)
    return jax.nn.silu(gate) * ((o - mu) * lax.rsqrt(var + EPS) * gain)


def _even_prompt_kernel(zq_ref, zk_ref, zv_ref, zg_ref, zp_ref, cos_ref, sin_ref, gain_ref, pw_ref, ps_ref,
                        mix_ref, sret_ref, spool_ref, state_ref, ext_ref, intra_ref, *, t):
    b = pl.program_id(0)
    c = pl.program_id(1)
    nc = pl.num_programs(1)
    halo = POOL_KEEP + 1
    dk = RET_DK

    @pl.when(jnp.logical_and(b == 0, c == 0))
    def _():
        rel = lax.broadcasted_iota(i32, (t, t), 0) - lax.broadcasted_iota(i32, (t, t), 1)
        relf = jnp.maximum(rel, 0).astype(f32)
        for h in range(RET_HEADS):
            intra_ref[h] = jnp.where(rel >= 0, jnp.exp(_ret_log_gamma(h) * relf), 0.0)

    @pl.when(c == 0)
    def _():
        state_ref[...] = jnp.zeros(state_ref.shape, f32)
        ext_ref[0:halo, :] = jnp.zeros((halo, ext_ref.shape[1]), f32)

    cosf = cos_ref[...]
    sinf = sin_ref[...]
    row = lax.broadcasted_iota(i32, (t, 1), 0).astype(f32)
    for h in range(RET_HEADS):
        sl = slice(h * dk, (h + 1) * dk)
        lg = _ret_log_gamma(h)
        qr = _rotate(zq_ref[:, sl], cosf, sinf)
        kr = _rotate(zk_ref[:, sl], cosf, sinf) * (dk ** -0.5)
        qb = qr.astype(bf16)
        vb = zv_ref[:, sl].astype(bf16)
        att = _dot_nt(qb, kr.astype(bf16)) * intra_ref[h]
        s_old = state_ref[h]
        o = _dot(att.astype(bf16), vb) + _dot(qb, s_old.astype(bf16)) * jnp.exp(lg * (row + 1.0))
        k_dec = jnp.exp(lg * (float(t - 1) - row))
        state_ref[h] = s_old * math.exp(lg * t) + _dot_tn((kr * k_dec).astype(bf16), vb)
        mix_ref[:, sl] = _group_norm_gate(o, gain_ref[:, sl], zg_ref[:, sl]).astype(mix_ref.dtype)

    p = zp_ref[...]
    ext_ref[halo:halo + t, :] = p
    pos = c * t + lax.broadcasted_iota(i32, (t, 1), 0)
    gw = p.shape[1] // len(POOL_WINDOWS)
    ret_w = RET_HEADS * dk
    for gi, w in enumerate(POOL_WINDOWS):
        sl = slice(gi * gw, (gi + 1) * gw)
        win = p[:, sl]
        for back in range(1, w):
            win = win + ext_ref[halo - back:halo - back + t, sl]
        cnt = jnp.minimum(w, pos + 1).astype(f32)
        pooled = win / cnt - p[:, sl]
        mixed = _dot(pooled.astype(bf16), pw_ref[gi]) * ps_ref[:, sl]
        mix_ref[:, ret_w + gi * gw:ret_w + (gi + 1) * gw] = mixed.astype(mix_ref.dtype)

    @pl.when(c == nc - 1)
    def _():
        sret_ref[0, 0] = state_ref[...]
        spool_ref[0, 0] = ext_ref[t + 1:t + halo, :]

    ext_ref[0:halo, :] = ext_ref[t:t + halo, :]


def _even_prompt(z, cosf, sinf, gain, pw, ps, batch, seq):
    t = ROW_TILE
    nc = seq // t
    rw = RET_HEADS * RET_DK
    pwid = ps.shape[0]
    zspec = lambda j: pl.BlockSpec((t, rw), lambda b, c: (b * nc + c, j))
    return pl.pallas_call(
        functools.partial(_even_prompt_kernel, t=t),
        grid=(batch, nc),
        in_specs=[zspec(0), zspec(1), zspec(2), zspec(3), zspec(4),
                  pl.BlockSpec((t, RET_DK), lambda b, c: (c, 0)), pl.BlockSpec((t, RET_DK), lambda b, c: (c, 0)),
                  _resident((1, rw)), _resident(pw.shape), _resident((1, pwid))],
        out_specs=[pl.BlockSpec((t, rw + pwid), lambda b, c: (b * nc + c, 0)),
                   pl.BlockSpec((1, 1, RET_HEADS, RET_DK, RET_DK), lambda b, c: (0, b, 0, 0, 0)),
                   pl.BlockSpec((1, 1, POOL_KEEP, pwid), lambda b, c: (0, b, 0, 0))],
        out_shape=[jax.ShapeDtypeStruct((batch * seq, rw + pwid), bf16),
                   jax.ShapeDtypeStruct((1, batch, RET_HEADS, RET_DK, RET_DK), f32),
                   jax.ShapeDtypeStruct((1, batch, POOL_KEEP, pwid), f32)],
        scratch_shapes=[pltpu.VMEM((RET_HEADS, RET_DK, RET_DK), f32),
                        pltpu.VMEM((POOL_KEEP + 1 + t, pwid), f32),
                        pltpu.VMEM((RET_HEADS, t, t), f32)],
        compiler_params=_params("arbitrary", "arbitrary"),
        name="even_prompt",
    )(z, z, z, z, z, cosf, sinf, gain.reshape(1, rw), pw, ps.reshape(1, pwid))


def _even_sample_kernel(z_ref, cos_ref, sin_ref, s0_ref, rows_ref, gain_ref, pw_ref, ps_ref,
                        mix_ref, sret_ref, spool_ref, ext_ref, *, pos):
    dk = RET_DK
    rw = RET_HEADS * dk
    z = jnp.broadcast_to(z_ref[0], (SUBLANES, z_ref.shape[2]))
    cosf = jnp.broadcast_to(cos_ref[...], (SUBLANES, dk))
    sinf = jnp.broadcast_to(sin_ref[...], (SUBLANES, dk))
    first = lax.broadcasted_iota(i32, (SUBLANES, dk), 0) == 0
    for h in range(RET_HEADS):
        sl = slice(h * dk, (h + 1) * dk)
        gamma = math.exp(_ret_log_gamma(h))
        qr = _rotate(z[:, sl], cosf, sinf)
        kr = _rotate(z[:, rw + h * dk:rw + (h + 1) * dk], cosf, sinf) * (dk ** -0.5)
        v = z[:, 2 * rw + h * dk:2 * rw + (h + 1) * dk]
        s_new = s0_ref[0, 0, h] * gamma + _dot_tn(jnp.where(first, kr, 0.0), v)
        sret_ref[0, 0, h] = s_new
        o = _dot(qr, s_new)
        gate = z[:, 3 * rw + h * dk:3 * rw + (h + 1) * dk]
        mix_ref[0, :, sl] = _group_norm_gate(o, gain_ref[:, sl], gate)[0:1]

    keep = POOL_KEEP
    pwid = ps_ref.shape[1]
    p = z[0:1, 4 * rw:4 * rw + pwid]
    ext_ref[0:keep, :] = rows_ref[0, 0]
    ext_ref[keep:keep + 1, :] = p
    spool_ref[0, 0] = ext_ref[1:keep + 1, :]
    gw = pwid // len(POOL_WINDOWS)
    for gi, w in enumerate(POOL_WINDOWS):
        sl = slice(gi * gw, (gi + 1) * gw)
        win = jnp.sum(ext_ref[keep + 1 - w:keep + 1, sl], axis=0, keepdims=True)
        pooled = win / float(min(w, pos + 1)) - p[:, sl]
        mixed = _dot(jnp.broadcast_to(pooled, (SUBLANES, gw)), pw_ref[gi])[0:1] * ps_ref[:, sl]
        mix_ref[0, :, rw + gi * gw:rw + (gi + 1) * gw] = mixed


def _even_sample(z, cosf, sinf, state_ret, state_pool, gain, pw, ps, pos):
    nb, zw = z.shape
    rw = RET_HEADS * RET_DK
    pwid = ps.shape[0]
    return pl.pallas_call(
        functools.partial(_even_sample_kernel, pos=pos),
        grid=(nb,),
        in_specs=[pl.BlockSpec((1, 1, zw), lambda b: (b, 0, 0)),
                  _resident((1, RET_DK)), _resident((1, RET_DK)),
                  pl.BlockSpec((1, 1, RET_HEADS, RET_DK, RET_DK), lambda b: (0, b, 0, 0, 0)),
                  pl.BlockSpec((1, 1, POOL_KEEP, pwid), lambda b: (0, b, 0, 0)),
                  _resident((1, rw)), _resident(pw.shape), _resident((1, pwid))],
        out_specs=[pl.BlockSpec((1, 1, rw + pwid), lambda b: (b, 0, 0)),
                   pl.BlockSpec((1, 1, RET_HEADS, RET_DK, RET_DK), lambda b: (0, b, 0, 0, 0)),
                   pl.BlockSpec((1, 1, POOL_KEEP, pwid), lambda b: (0, b, 0, 0))],
        out_shape=[jax.ShapeDtypeStruct((nb, 1, rw + pwid), f32),
                   jax.ShapeDtypeStruct((1, nb, RET_HEADS, RET_DK, RET_DK), f32),
                   jax.ShapeDtypeStruct((1, nb, POOL_KEEP, pwid), f32)],
        scratch_shapes=[pltpu.VMEM((POOL_KEEP + 1, pwid), f32)],
        compiler_params=_params("arbitrary"),
        name="even_sample",
    )(z.reshape(nb, 1, zw), cosf, sinf, state_ret, state_pool, gain.reshape(1, rw), pw, ps.reshape(1, pwid))


def _rel_bias_of(dist, rb_ref, h):
    n = jnp.maximum(dist, 0)
    max_exact = REL_BUCKETS // 2
    nf = jnp.maximum(n, 1).astype(f32)
    large = max_exact + (jnp.log(nf / max_exact) / math.log(REL_MAX_DIST / max_exact)
                         * (REL_BUCKETS - max_exact)).astype(i32)
    large = jnp.minimum(large, REL_BUCKETS - 1)
    bucket = jnp.where(n < max_exact, n, large)
    out = jnp.zeros(dist.shape, f32)
    for bk in range(REL_BUCKETS):
        out = jnp.where(bucket == bk, rb_ref[bk, h], out)
    return out


def _bias_kernel(rb_ref, lq1_ref, lk1_ref, lq2_ref, lk2_ref, tile_ref, tab_ref, lam_ref, *, t, lam_init):
    h = pl.program_id(0)
    nd = REL_MAX_DIST
    rel = lax.broadcasted_iota(i32, (nd, nd), 1) - lax.broadcasted_iota(i32, (nd, nd), 0)
    on_diag = jnp.where(rel >= 0, _rel_bias_of(rel, rb_ref, h) * LOG2E, NEG_INF)
    next_diag = _rel_bias_of(rel + nd, rb_ref, h) * LOG2E
    far = _rel_bias_of(jnp.full((nd, nd), nd, i32), rb_ref, h) * LOG2E
    masked = jnp.full((nd, nd), NEG_INF, f32)
    nblk = t // nd
    for bk in range(nblk):
        for bq in range(nblk):
            diag_blk = on_diag if bq == bk else next_diag if bq == bk + 1 else far if bq > bk else masked
            below_blk = next_diag if (bk == nblk - 1 and bq == 0) else far
            for c0 in (bq * nd, t + bq * nd):
                tile_ref[0, 0, bk * nd:(bk + 1) * nd, c0:c0 + nd] = diag_blk
                tile_ref[0, 1, bk * nd:(bk + 1) * nd, c0:c0 + nd] = below_blk
    tab_ref[0] = _rel_bias_of(lax.broadcasted_iota(i32, (1, 2 * REL_MAX_DIST), 1), rb_ref, h)
    lam = (jnp.exp(jnp.sum(lq1_ref[...] * lk1_ref[...], axis=-1, keepdims=True))
           - jnp.exp(jnp.sum(lq2_ref[...] * lk2_ref[...], axis=-1, keepdims=True)) + lam_init)
    lam_ref[...] = jnp.broadcast_to(lam, lam_ref.shape)


def _bias_tables(rel_bias, lq1, lk1, lq2, lk2, lam_init, t):
    nh = rel_bias.shape[1]
    d = lq1.shape[0]
    vec = lambda a: a.reshape(1, d)
    return pl.pallas_call(
        functools.partial(_bias_kernel, t=t, lam_init=lam_init),
        grid=(nh,),
        in_specs=[pl.BlockSpec(memory_space=pltpu.SMEM)] + [_resident((1, d))] * 4,
        out_specs=[pl.BlockSpec((1, 2, t, 2 * t), lambda h: (h, 0, 0, 0)),
                   pl.BlockSpec((1, 1, 2 * REL_MAX_DIST), lambda h: (h, 0, 0)),
                   pl.BlockSpec((SUBLANES, LANES), lambda h: (0, 0))],
        out_shape=[jax.ShapeDtypeStruct((nh, 2, t, 2 * t), f32),
                   jax.ShapeDtypeStruct((nh, 1, 2 * REL_MAX_DIST), f32),
                   jax.ShapeDtypeStruct((SUBLANES, LANES), f32)],
        compiler_params=_params("arbitrary"),
        name="bias_tables",
    )(rel_bias, vec(lq1), vec(lk1), vec(lq2), vec(lk2))


def _sub_ln(o, gain, lam_init):
    return o * lax.rsqrt(jnp.mean(o * o, axis=-1, keepdims=True) + EPS) * gain * (1.0 - lam_init)


def _attn_prompt_kernel(lam_ref, far_ref, q_ref, k_ref, vt_ref, tile_ref, gain_ref, o_ref,
                        qq_ref, m_ref, l_ref, acc_ref, s0_ref, smax0_ref, s1_ref, smax1_ref, *, t, cb, lam_init):
    h = pl.program_id(1)
    qi = pl.program_id(2)
    q = q_ref[...]
    lane = lax.broadcasted_iota(i32, q.shape, 1)
    zero = jnp.zeros_like(q)
    qq_ref[0:t, :] = jnp.where(lane < DIFF_D, q, zero)
    qq_ref[t:2 * t, :] = jnp.where(lane >= DIFF_D, q, zero)
    m_ref[...] = jnp.full(m_ref.shape, NEG_INF, f32)
    l_ref[...] = jnp.zeros(l_ref.shape, f32)
    acc_ref[...] = jnp.zeros(acc_ref.shape, f32)

    far = far_ref[h] * LOG2E

    blocks = [slice(c0, c0 + cb) for c0 in range(0, 2 * t, cb)]

    parked = ((s0_ref, smax0_ref), (s1_ref, smax1_ref))

    def scores(ki, slot):
        s_ref, smax_ref = parked[slot]
        kt = k_ref[pl.ds(pl.multiple_of(ki * t, t), t), :]
        for cols in blocks:
            s = _dot_nt(kt, qq_ref[cols, :])
            s_ref[:, cols] = s
            smax_ref[:, cols] = jnp.max(s, axis=0, keepdims=True)

    def absorb(ki, slot, tile):
        s_ref, smax_ref = parked[slot]
        vt = vt_ref[:, pl.ds(pl.multiple_of(ki * t, t), t)]
        for cols in blocks:
            m_old = m_ref[:, cols]
            if tile is None:
                m_new = jnp.maximum(m_old, smax_ref[:, cols] + far)
                p = jnp.exp2(s_ref[:, cols] - (m_new - far))
            else:
                s = s_ref[:, cols] + tile_ref[0, tile, :, cols]
                m_new = jnp.maximum(m_old, jnp.max(s, axis=0, keepdims=True))
                p = jnp.exp2(s - m_new)
            alpha = jnp.exp2(m_old - m_new)
            l_ref[:, cols] = alpha * l_ref[:, cols] + jnp.sum(p, axis=0, keepdims=True)
            acc_ref[:, cols] = alpha * acc_ref[:, cols] + _dot(vt, p.astype(bf16))
            m_ref[:, cols] = m_new

    def stage(ki, slot, tile):
        scores(ki + 1, 1 - slot)
        absorb(ki, slot, tile)

    n_far = jnp.maximum(qi - 1, 0)
    scores(0, 0)

    def far_pair(j, carry):
        stage(2 * j, 0, None)
        stage(2 * j + 1, 1, None)
        return carry

    lax.fori_loop(0, n_far // 2, far_pair, 0)

    @pl.when(n_far % 2 == 1)
    def _():
        stage(n_far - 1, 0, None)

    for slot in (0, 1):
        @pl.when(jnp.logical_and(qi >= 1, n_far % 2 == slot))
        def _():
            stage(qi - 1, slot, 1)

    for slot in (0, 1):
        @pl.when(qi % 2 == slot)
        def _():
            absorb(qi, slot, 0)

    o = acc_ref[...] / l_ref[...]
    o = o[:, 0:t] - lam_ref[0] * o[:, t:2 * t]
    y = o * lax.rsqrt(jnp.mean(o * o, axis=0, keepdims=True) + EPS) * gain_ref[...] * (1.0 - lam_init)
    o_ref[...] = y.T.astype(o_ref.dtype)


def _attn_prompt(q, k, vt, tiles, far, lam, gain, batch, seq, lam_init):
    t = ATT_TILE
    nq = seq // t
    hd = DIFF_HD
    nh = DIFF_HEADS
    smem = pl.BlockSpec(memory_space=pltpu.SMEM)
    return pl.pallas_call(
        functools.partial(_attn_prompt_kernel, t=t, cb=ATT_QUERY_BLOCK, lam_init=lam_init),
        grid=(batch, nh, nq),
        in_specs=[smem, smem,
                  pl.BlockSpec((t, hd), lambda b, h, i: (b * nq + i, h)),
                  pl.BlockSpec((seq, hd), lambda b, h, i: (b, h)),
                  pl.BlockSpec((hd, seq), lambda b, h, i: (b * nh + h, 0)),
                  pl.BlockSpec((1, 2, t, 2 * t), lambda b, h, i: (h, 0, 0, 0)),
                  _resident((hd, 1))],
        out_specs=pl.BlockSpec((t, hd), lambda b, h, i: (b * nq + i, h)),
        out_shape=jax.ShapeDtypeStruct((batch * seq, nh * hd), bf16),
        scratch_shapes=[pltpu.VMEM((2 * t, hd), bf16), pltpu.VMEM((1, 2 * t), f32),
                        pltpu.VMEM((1, 2 * t), f32), pltpu.VMEM((hd, 2 * t), f32),
                        pltpu.VMEM((t, 2 * t), f32), pltpu.VMEM((1, 2 * t), f32),
                        pltpu.VMEM((t, 2 * t), f32), pltpu.VMEM((1, 2 * t), f32)],
        compiler_params=_params("arbitrary", "arbitrary", "arbitrary"),
        name="attn_prompt",
    )(lam, far, q, k, vt, tiles, gain.reshape(hd, 1))


def _attn_decode_kernel(pt_ref, lam_ref, q_ref, kn_ref, vn_ref, bias_ref, b0_ref, gain_ref, *rest,
                        npg, lam_init):
    k_refs = rest[:npg]
    v_refs = rest[npg:2 * npg]
    o_ref = rest[2 * npg]
    qh_ref, kbuf_ref, vbuf_ref, m_ref, l_ref, acc_ref = rest[2 * npg + 1:]
    c = pl.program_id(1)
    nc = pl.num_programs(1)
    nh = DIFF_HEADS
    prow = PAGE_SIZE * nh

    def both_streams(x):
        return jnp.concatenate([x, x], axis=0)

    @pl.when(c == 0)
    def _():
        q = q_ref[0]
        lane = lax.broadcasted_iota(i32, q.shape, 1)
        qh = jnp.concatenate([jnp.where(lane < DIFF_D, q, 0.0), jnp.where(lane >= DIFF_D, q, 0.0)],
                             axis=0).astype(bf16)
        qh_ref[...] = qh
        kn = both_streams(kn_ref[0]).astype(bf16).astype(f32)
        m_ref[...] = jnp.sum(qh.astype(f32) * kn, axis=-1, keepdims=True) + b0_ref[...]
        l_ref[...] = jnp.ones(l_ref.shape, f32)
        acc_ref[...] = both_streams(vn_ref[0]).astype(bf16).astype(f32)

    for i in range(npg):
        kbuf_ref[i * prow:(i + 1) * prow, :] = k_refs[i][...].astype(bf16)
        vbuf_ref[i * prow:(i + 1) * prow, :] = v_refs[i][...].astype(bf16)

    s = _dot_nt(qh_ref[...], kbuf_ref[...]) + bias_ref[(c == nc - 1).astype(i32)]
    m_old = m_ref[...]
    m_new = jnp.maximum(m_old, jnp.max(s, axis=-1, keepdims=True))
    alpha = jnp.exp(m_old - m_new)
    p = jnp.exp(s - m_new)
    l_ref[...] = alpha * l_ref[...] + jnp.sum(p, axis=-1, keepdims=True)
    acc_ref[...] = alpha * acc_ref[...] + _dot(p.astype(bf16), vbuf_ref[...])
    m_ref[...] = m_new

    @pl.when(c == nc - 1)
    def _():
        on = acc_ref[...] / l_ref[...]
        o_ref[0] = _sub_ln(on[0:nh] - lam_ref[0] * on[nh:2 * nh], gain_ref[...], lam_init)


def _attn_decode(q, kn, vn, cache_k, cache_v, layer, page_table, bias2, b0, lam, gain, lam_init):
    nb, nh, hd = q.shape
    pool = cache_k.shape[1]
    npages = page_table.shape[1]
    npg = PAGES_PER_STEP
    nc = npages // npg
    nrow = 2 * nh
    prow = PAGE_SIZE * nh
    page_spec = lambda i: pl.BlockSpec((prow, hd), lambda b, c, pt: (layer * pool + pt[b, c * npg + i], 0))
    rowspec = pl.BlockSpec((1, nh, hd), lambda b, c, pt: (b, 0, 0))
    const = lambda shape: pl.BlockSpec(shape, lambda b, c, pt: (0,) * len(shape))
    grid_spec = pltpu.PrefetchScalarGridSpec(
        num_scalar_prefetch=1,
        grid=(nb, nc),
        in_specs=[pl.BlockSpec(memory_space=pltpu.SMEM), rowspec, rowspec, rowspec,
                  const((2, nrow, npg * prow)), const((nrow, 1)), const((1, hd))]
                 + [page_spec(i) for i in range(npg)] * 2,
        out_specs=rowspec,
        scratch_shapes=[pltpu.VMEM((nrow, hd), bf16), pltpu.VMEM((npg * prow, hd), bf16),
                        pltpu.VMEM((npg * prow, hd), bf16), pltpu.VMEM((nrow, 1), f32),
                        pltpu.VMEM((nrow, 1), f32), pltpu.VMEM((nrow, hd), f32)],
    )
    ck = cache_k.reshape(-1, hd)
    cv = cache_v.reshape(-1, hd)
    return pl.pallas_call(
        functools.partial(_attn_decode_kernel, npg=npg, lam_init=lam_init),
        grid_spec=grid_spec,
        out_shape=jax.ShapeDtypeStruct((nb, nh, hd), f32),
        compiler_params=_params("arbitrary", "arbitrary"),
        name="attn_decode",
    )(page_table, lam, q, kn, vn, bias2, b0, gain.reshape(1, hd), *([ck] * npg), *([cv] * npg))


def _softmax_rows(s):
    e = jnp.exp(s - jnp.max(s, axis=-1, keepdims=True))
    return e / jnp.sum(e, axis=-1, keepdims=True)


def _cross_prompt_kernel(x_ref, g_ref, wq_ref, mk_ref, mv_ref, wo_ref, o_ref, a_ref):
    x = x_ref[...]
    d = x.shape[1]
    hd = d // MEM_HEADS
    h = _rms(x, g_ref[...]).astype(bf16)
    q = (_dot(h, wq_ref[...]) * (hd ** -0.5)).astype(bf16)
    for hh in range(MEM_HEADS):
        sl = slice(hh * hd, (hh + 1) * hd)
        a = _softmax_rows(_dot_nt(q[:, sl], mk_ref[:, sl])).astype(bf16)
        a_ref[:, sl] = _dot(a, mv_ref[:, sl]).astype(bf16)
    o_ref[...] = x + _dot(a_ref[...], wo_ref[...])


def _cross_prompt(x, g, wq, mk, mv, wo, seq):
    m, d = x.shape
    _, wq_spec, wq = _stacked(wq)
    _, wo_spec, wo = _stacked(wo)
    tm = ROW_TILE
    per = seq // tm
    mem = mk.shape[0] // (m // seq)
    return pl.pallas_call(
        _cross_prompt_kernel,
        grid=(m // tm,),
        in_specs=[pl.BlockSpec((tm, d), lambda i: (i, 0)), _resident((1, d)), wq_spec,
                  pl.BlockSpec((mem, d), lambda i: (i // per, 0)), pl.BlockSpec((mem, d), lambda i: (i // per, 0)),
                  wo_spec],
        out_specs=pl.BlockSpec((tm, d), lambda i: (i, 0)),
        out_shape=jax.ShapeDtypeStruct((m, d), f32),
        scratch_shapes=[pltpu.VMEM((tm, d), bf16)],
        compiler_params=_params("arbitrary"),
        name="cross_prompt",
    )(x, g.reshape(1, d), wq, mk, mv, wo)


def _cross_sample_kernel(q_ref, mk_ref, mv_ref, mask_ref, o_ref):
    nh, hd = q_ref.shape[1], q_ref.shape[2]
    q = jnp.concatenate([q_ref[0] * (hd ** -0.5), jnp.zeros((SUBLANES - nh, hd), f32)], axis=0)
    s = _dot_nt(q.astype(bf16), mk_ref[...].astype(bf16)) + mask_ref[...]
    o = _dot(_softmax_rows(s).astype(bf16), mv_ref[...].astype(bf16))
    o_ref[0] = o[0:nh]


def _cross_sample(q, cache_mk, cache_mv, layer):
    nb, nh, hd = q.shape
    rows = cache_mk.shape[2] * nh
    head_of_row = jnp.arange(rows, dtype=i32) % nh
    mask = jnp.where(head_of_row[None, :] == jnp.arange(SUBLANES, dtype=i32)[:, None], 0.0, NEG_INF).astype(f32)
    mask = mask.at[nh:].set(0.0)
    mem_spec = pl.BlockSpec((rows, hd), lambda b: (layer * nb + b, 0))
    return pl.pallas_call(
        _cross_sample_kernel,
        grid=(nb,),
        in_specs=[pl.BlockSpec((1, nh, hd), lambda b: (b, 0, 0)), mem_spec, mem_spec, _resident((SUBLANES, rows))],
        out_specs=pl.BlockSpec((1, nh, hd), lambda b: (b, 0, 0)),
        out_shape=jax.ShapeDtypeStruct((nb, nh, hd), f32),
        compiler_params=_params("arbitrary"),
        name="cross_sample",
    )(q, cache_mk.reshape(-1, hd), cache_mv.reshape(-1, hd), mask)


def _rotary_tables(pos, half):
    inv = 1.0 / (10000.0 ** jnp.linspace(0.0, 1.0, half, dtype=f32))
    ang = pos.astype(f32)[:, None] * inv[None, :]
    cos, sin = jnp.cos(ang), jnp.sin(ang)
    return jnp.concatenate([cos, cos], axis=-1), jnp.concatenate([-sin, sin], axis=-1)


def kernel(x_prompt, x_sample, state_ret, state_pool, cache_k_diff, cache_v_diff, cache_mem_k, cache_mem_v, page_table, mem_prompt, norms, final_norm, ffn_w_gate, ffn_w_up, ffn_w_down, w_in_even, ret_gain, pool_w, pool_scale, w_out_even, w_qkv_odd, lambda_q1, lambda_k1, lambda_q2, lambda_k2, subln_gain, w_out_odd, rel_bias, w_cq, w_ckv, w_co):
    batch, seq, d = x_prompt.shape
    nb = x_sample.shape[0]
    depth = norms.shape[0]
    past = page_table.shape[1] * PAGE_SIZE
    mem_len = mem_prompt.shape[1]
    rw = RET_HEADS * RET_DK
    dw = DIFF_HEADS * DIFF_HD

    xp = x_prompt.reshape(batch * seq, d)
    xs = x_sample.reshape(nb, d)
    mem2d = mem_prompt.reshape(batch * mem_len, d)

    cos_p, sin_p = _rotary_tables(jnp.arange(seq, dtype=i32), RET_DK // 2)
    cos_s, sin_s = _rotary_tables(jnp.full((1,), past, dtype=i32), RET_DK // 2)

    wg, wu, wd = ffn_w_gate.astype(bf16), ffn_w_up.astype(bf16), ffn_w_down.astype(bf16)
    w_in, w_oe = w_in_even.astype(bf16), w_out_even.astype(bf16)
    w_qkv, w_oo = w_qkv_odd.astype(bf16), w_out_odd.astype(bf16)
    wcq, wckv, wco = w_cq.astype(bf16), w_ckv.astype(bf16), w_co.astype(bf16)
    pw = pool_w.astype(bf16)

    p_ret, p_pool, p_k, p_v, p_mk, p_mv = [], [], [], [], [], []
    s_ret, s_pool, s_k, s_v = [], [], [], []
    for l in range(depth):
        last = l == depth - 1
        ffn_w = lambda i: ((wg, (l, i)), (wu, (l, i)), (wd, (l, i)))
        xp = _ffn(xp, norms[l, 0], *ffn_w(0))
        xs = _ffn(xs, norms[l, 0], *ffn_w(0))
        if l % 2 == 0:
            e = l // 2
            zw = w_in.shape[2]
            (zp,) = _proj(xp, norms[l, 1], (w_in, (e,)), [(0, zw, [(0, 1.0)])], [(zw, f32)])
            (zs,) = _proj(xs, norms[l, 1], (w_in, (e,)), [(0, zw, [(0, 1.0)])], [(zw, f32)])
            mixp, rp, pp = _even_prompt(zp, cos_p, sin_p, ret_gain[e], pw[e], pool_scale[e], batch, seq)
            mixs, rs, ps = _even_sample(zs, cos_s, sin_s, state_ret[e:e + 1], state_pool[e:e + 1], ret_gain[e],
                                        pool_w[e], pool_scale[e], past)
            xp = _mmres(xp, mixp, (w_oe, (e,)))
            xs = _mmres(xs, mixs.reshape(nb, rw + pool_scale.shape[1]), (w_oe, (e,)))
            p_ret.append(rp[0])
            p_pool.append(pp[0])
            s_ret.append(rs[0])
            s_pool.append(ps[0])
        else:
            o = l // 2
            lam_init = 0.8 - 0.6 * math.exp(-0.3 * l)
            scale = DIFF_D ** -0.5
            tiles, tab, lam_t = _bias_tables(rel_bias, lambda_q1[o], lambda_k1[o], lambda_q2[o], lambda_k2[o],
                                             lam_init, ATT_TILE)
            lam = lam_t[0, 0:1]
            tab = tab[:, 0, :]
            far = tab[:, REL_MAX_DIST]
            qkv_groups = [(0, dw, [(0, scale * LOG2E)]), (dw, dw, [(1, 1.0), (3, 1.0)]),
                          (2 * dw, dw, [(2, 1.0), (4, 1.0)])]
            qb, kp, vp, kb, vtb = _proj(xp, norms[l, 1], (w_qkv, (o,)), qkv_groups,
                                        [(dw, bf16), (dw, f32, DIFF_HD), (dw, f32, DIFF_HD), (dw, bf16),
                                         (dw, bf16, "T")], seq=seq)
            attp = _attn_prompt(qb, kb, vtb, tiles, far, lam, subln_gain[o], batch, seq, lam_init)
            xp = _mmres(xp, attp, (w_oo, (o,)))
            qs, ks_, vs = _proj(xs, norms[l, 1], (w_qkv, (o,)),
                                [(0, dw, [(0, scale)]), (dw, dw, [(1, 1.0)]), (2 * dw, dw, [(2, 1.0)])],
                                [(dw, f32), (dw, f32), (dw, f32)])
            nh = DIFF_HEADS
            own = jnp.arange(nh)[:, None, None] == jnp.arange(nh)[None, None, :]
            far_page = jnp.where(own, far[:, None, None], NEG_INF) + jnp.zeros((1, PAGE_SIZE, 1), f32)
            last_page = jnp.where(own, tab[:, PAGE_SIZE:0:-1][:, :, None], NEG_INF)
            far_page, last_page = far_page.reshape(nh, -1), last_page.reshape(nh, -1)
            far_step = jnp.tile(far_page, (2, PAGES_PER_STEP))
            last_step = jnp.concatenate([far_step[:, :-PAGE_SIZE * nh], jnp.tile(last_page, (2, 1))], axis=1)
            bias2 = jnp.stack([far_step, last_step])
            b0 = jnp.tile(tab[:, 0], 2)[:, None]
            heads = lambda a: a.reshape(nb, nh, DIFF_HD)
            atts = _attn_decode(heads(qs), heads(ks_), heads(vs), cache_k_diff, cache_v_diff, o, page_table,
                                bias2, b0, lam, subln_gain[o], lam_init)
            xs = _mmres(xs, atts.reshape(nb, dw), (w_oo, (o,)))
            p_k.append(kp.reshape(batch, seq, DIFF_HEADS, DIFF_HD))
            p_v.append(vp.reshape(batch, seq, DIFF_HEADS, DIFF_HD))
            s_k.append(ks_.reshape(nb, 1, DIFF_HEADS, DIFF_HD))
            s_v.append(vs.reshape(nb, 1, DIFF_HEADS, DIFF_HD))
        mk, mv, mkb, mvb = _proj(mem2d, None, (wckv, (l,)),
                                 [(0, d, [(0, 1.0), (2, 1.0)]), (d, d, [(1, 1.0), (3, 1.0)])],
                                 [(d, f32), (d, f32), (d, bf16), (d, bf16)])
        p_mk.append(mk.reshape(batch, mem_len, MEM_HEADS, d // MEM_HEADS))
        p_mv.append(mv.reshape(batch, mem_len, MEM_HEADS, d // MEM_HEADS))
        xp = _cross_prompt(xp, norms[l, 2], (wcq, (l,)), mkb, mvb, (wco, (l,)), seq)
        (qcs,) = _proj(xs, norms[l, 2], (wcq, (l,)), [(0, d, [(0, 1.0)])], [(d, f32)])
        acs = _cross_sample(qcs.reshape(nb, MEM_HEADS, d // MEM_HEADS), cache_mem_k, cache_mem_v, l)
        xs = _mmres(xs, acs.reshape(nb, d), (wco, (l,)))
        fin = final_norm if last else None
        xp = _ffn(xp, norms[l, 3], *ffn_w(1), fin)
        xs = _ffn(xs, norms[l, 3], *ffn_w(1), fin)

    return (xp.reshape(batch, seq, d), xs.reshape(nb, 1, d),
            jnp.stack(p_ret), jnp.stack(p_pool), jnp.stack(p_k), jnp.stack(p_v), jnp.stack(p_mk), jnp.stack(p_mv),
            jnp.stack(s_ret), jnp.stack(s_pool), jnp.stack(s_k), jnp.stack(s_v))
```

```python
import functools
import math

import jax
import jax.numpy as jnp
from jax import lax
from jax.experimental import pallas as pl
from jax.experimental.pallas import tpu as pltpu

f32 = jnp.float32
bf16 = jnp.bfloat16
i32 = jnp.int32

EPS = 1e-6
NEG_INF = -1e30
LOG2E = math.log2(math.e)

RET_HEADS = 4
RET_DK = 128
POOL_WINDOWS = (2, 4, 8, 16)
POOL_KEEP = max(POOL_WINDOWS) - 1
DIFF_HEADS = 8
DIFF_D = 64
DIFF_HD = 2 * DIFF_D
REL_BUCKETS = 32
REL_MAX_DIST = 128
MEM_HEADS = 4
PAGE_SIZE = 128

VMEM_LIMIT_BYTES = 56 * 1024 * 1024
LANES = 128
SUBLANES = 8

ROW_TILE = 512
ATT_TILE = 512
ATT_QUERY_BLOCK = 512
ATTN_VMEM_LIMIT_BYTES = 60 * 1024 * 1024


def _params(*sem):
    return pltpu.CompilerParams(dimension_semantics=sem, vmem_limit_bytes=VMEM_LIMIT_BYTES)


def _rms(x, g):
    return x * lax.rsqrt(jnp.mean(x * x, axis=-1, keepdims=True) + EPS) * g


def _dot(a, b):
    return jnp.dot(a, b, preferred_element_type=f32)


def _dot_nt(a, b):
    return lax.dot_general(a, b, (((1,), (1,)), ((), ())), preferred_element_type=f32)


def _dot_tn(a, b):
    return lax.dot_general(a, b, (((0,), (0,)), ((), ())), preferred_element_type=f32)


def _resident(shape, lead=()):
    nd = len(shape)
    return pl.BlockSpec((None,) * len(lead) + tuple(shape), lambda *_: tuple(lead) + (0,) * nd,
                        pipeline_mode=pl.Buffered(1))


def _stacked(w):
    arr, lead = w
    shape = arr.shape[len(lead):]
    return shape, _resident(shape, lead), arr


def _ffn_kernel(x_ref, g_ref, wg_ref, wu_ref, wd_ref, *rest, fchunk, final):
    o_ref = rest[-1]
    x = x_ref[...]
    h = _rms(x, g_ref[...]).astype(bf16)
    acc = jnp.zeros(x.shape, f32)
    for c0 in range(0, wg_ref.shape[1], fchunk):
        gate = _dot(h, wg_ref[:, c0:c0 + fchunk])
        up = _dot(h, wu_ref[:, c0:c0 + fchunk])
        a = (jax.nn.silu(gate) * up).astype(bf16)
        acc = acc + _dot(a, wd_ref[c0:c0 + fchunk, :])
    y = x + 0.5 * acc
    if final:
        y = _rms(y, rest[0][...])
    o_ref[...] = y


def _ffn(x, g, wg, wu, wd, final_g=None):
    m, d = x.shape
    (_, f), wg_spec, wg = _stacked(wg)
    _, wu_spec, wu = _stacked(wu)
    _, wd_spec, wd = _stacked(wd)
    tm = min(ROW_TILE, m)
    fchunk = f // 2 if (f // 2) % LANES == 0 else f
    final = final_g is not None
    in_specs = [pl.BlockSpec((tm, d), lambda i: (i, 0)), _resident((1, d)), wg_spec, wu_spec, wd_spec]
    args = [x, g.reshape(1, d), wg, wu, wd]
    if final:
        in_specs.append(_resident((1, d)))
        args.append(final_g.reshape(1, d))
    return pl.pallas_call(
        functools.partial(_ffn_kernel, fchunk=fchunk, final=final),
        grid=(m // tm,),
        in_specs=in_specs,
        out_specs=pl.BlockSpec((tm, d), lambda i: (i, 0)),
        out_shape=jax.ShapeDtypeStruct((m, d), f32),
        compiler_params=_params("arbitrary"),
        name="ffn",
    )(*args)


def _proj_kernel(*refs, norm, groups, transposed, nchunk):
    x_ref = refs[0]
    g_ref = refs[1] if norm else None
    w_ref = refs[2] if norm else refs[1]
    o_refs = refs[(3 if norm else 2):]
    x = x_ref[...]
    h = (_rms(x, g_ref[...]) if norm else x).astype(bf16)
    for off, width, targets in groups:
        for c0 in range(0, width, nchunk):
            n = min(nchunk, width - c0)
            r = _dot(h, w_ref[:, off + c0:off + c0 + n])
            for oi, scale in targets:
                o_ref = o_refs[oi]
                val = (r if scale == 1.0 else r * scale)
                if transposed[oi]:
                    o_ref[c0:c0 + n, :] = val.T.astype(o_ref.dtype)
                else:
                    o_ref[:, c0:c0 + n] = val.astype(o_ref.dtype)


def _proj(x, g, w, groups, outs, seq=None):
    m, d = x.shape
    _, w_spec, w = _stacked(w)
    tm = min(ROW_TILE, m)
    norm = g is not None
    in_specs = [pl.BlockSpec((tm, d), lambda i: (i, 0))]
    args = [x]
    if norm:
        in_specs.append(_resident((1, d)))
        args.append(g.reshape(1, d))
    in_specs.append(w_spec)
    args.append(w)
    out_specs, out_shape = [], []
    for wd, dt, *flag in outs:
        if flag:
            per = seq // tm
            out_specs.append(pl.BlockSpec((wd, tm), lambda i: (i // per, i % per)))
            out_shape.append(jax.ShapeDtypeStruct((m // seq * wd, seq), dt))
        else:
            out_specs.append(pl.BlockSpec((tm, wd), lambda i: (i, 0)))
            out_shape.append(jax.ShapeDtypeStruct((m, wd), dt))
    return pl.pallas_call(
        functools.partial(_proj_kernel, norm=norm, groups=groups, transposed=tuple(o[2:] == ("T",) for o in outs),
                          nchunk=512),
        grid=(m // tm,),
        in_specs=in_specs,
        out_specs=out_specs,
        out_shape=out_shape,
        compiler_params=_params("arbitrary"),
        name="proj",
    )(*args)


def _mmres_kernel(x_ref, a_ref, w_ref, o_ref):
    o_ref[...] = x_ref[...] + _dot(a_ref[...].astype(bf16), w_ref[...])


def _mmres(x, a, w):
    m, d = x.shape
    k = a.shape[1]
    _, w_spec, w = _stacked(w)
    tm = min(ROW_TILE, m)
    return pl.pallas_call(
        _mmres_kernel,
        grid=(m // tm,),
        in_specs=[pl.BlockSpec((tm, d), lambda i: (i, 0)), pl.BlockSpec((tm, k), lambda i: (i, 0)), w_spec],
        out_specs=pl.BlockSpec((tm, d), lambda i: (i, 0)),
        out_shape=jax.ShapeDtypeStruct((m, d), f32),
        compiler_params=_params("arbitrary"),
        name="mmres",
    )(x, a, w)


def _ret_log_gamma(h):
    return math.log1p(-(2.0 ** (-5.0 - h)))


def _rotate(x, cosf, sinf):
    return x * cosf + pltpu.roll(x, x.shape[-1] // 2, axis=1) * sinf


def _group_norm_gate(o, gain, gate):
    mu = jnp.mean(o, axis=-1, keepdims=True)
    var = jnp.mean(jnp.square(o - mu), axis=-1, keepdims=True)
    return jax.nn.silu(gate) * ((o - mu) * lax.rsqrt(var + EPS) * gain)


def _even_prompt_kernel(zq_ref, zk_ref, zv_ref, zg_ref, zp_ref, cos_ref, sin_ref, gain_ref, pw_ref, ps_ref,
                        mix_ref, sret_ref, spool_ref, state_ref, ext_ref, intra_ref, *, t):
    b = pl.program_id(0)
    c = pl.program_id(1)
    nc = pl.num_programs(1)
    halo = POOL_KEEP + 1
    dk = RET_DK

    @pl.when(jnp.logical_and(b == 0, c == 0))
    def _():
        rel = lax.broadcasted_iota(i32, (t, t), 0) - lax.broadcasted_iota(i32, (t, t), 1)
        relf = jnp.maximum(rel, 0).astype(f32)
        for h in range(RET_HEADS):
            intra_ref[h] = jnp.where(rel >= 0, jnp.exp(_ret_log_gamma(h) * relf), 0.0)

    @pl.when(c == 0)
    def _():
        state_ref[...] = jnp.zeros(state_ref.shape, f32)
        ext_ref[0:halo, :] = jnp.zeros((halo, ext_ref.shape[1]), f32)

    cosf = cos_ref[...]
    sinf = sin_ref[...]
    row = lax.broadcasted_iota(i32, (t, 1), 0).astype(f32)
    for h in range(RET_HEADS):
        sl = slice(h * dk, (h + 1) * dk)
        lg = _ret_log_gamma(h)
        qr = _rotate(zq_ref[:, sl], cosf, sinf)
        kr = _rotate(zk_ref[:, sl], cosf, sinf) * (dk ** -0.5)
        qb = qr.astype(bf16)
        vb = zv_ref[:, sl].astype(bf16)
        att = _dot_nt(qb, kr.astype(bf16)) * intra_ref[h]
        s_old = state_ref[h]
        o = _dot(att.astype(bf16), vb) + _dot(qb, s_old.astype(bf16)) * jnp.exp(lg * (row + 1.0))
        k_dec = jnp.exp(lg * (float(t - 1) - row))
        state_ref[h] = s_old * math.exp(lg * t) + _dot_tn((kr * k_dec).astype(bf16), vb)
        mix_ref[:, sl] = _group_norm_gate(o, gain_ref[:, sl], zg_ref[:, sl]).astype(mix_ref.dtype)

    p = zp_ref[...]
    ext_ref[halo:halo + t, :] = p
    pos = c * t + lax.broadcasted_iota(i32, (t, 1), 0)
    gw = p.shape[1] // len(POOL_WINDOWS)
    ret_w = RET_HEADS * dk
    for gi, w in enumerate(POOL_WINDOWS):
        sl = slice(gi * gw, (gi + 1) * gw)
        win = p[:, sl]
        for back in range(1, w):
            win = win + ext_ref[halo - back:halo - back + t, sl]
        cnt = jnp.minimum(w, pos + 1).astype(f32)
        pooled = win / cnt - p[:, sl]
        mixed = _dot(pooled.astype(bf16), pw_ref[gi]) * ps_ref[:, sl]
        mix_ref[:, ret_w + gi * gw:ret_w + (gi + 1) * gw] = mixed.astype(mix_ref.dtype)

    @pl.when(c == nc - 1)
    def _():
        sret_ref[0, 0] = state_ref[...]
        spool_ref[0, 0] = ext_ref[t + 1:t + halo, :]

    ext_ref[0:halo, :] = ext_ref[t:t + halo, :]


def _even_prompt(z, cosf, sinf, gain, pw, ps, batch, seq):
    t = ROW_TILE
    nc = seq // t
    rw = RET_HEADS * RET_DK
    pwid = ps.shape[0]
    zspec = lambda j: pl.BlockSpec((t, rw), lambda b, c: (b * nc + c, j))
    return pl.pallas_call(
        functools.partial(_even_prompt_kernel, t=t),
        grid=(batch, nc),
        in_specs=[zspec(0), zspec(1), zspec(2), zspec(3), zspec(4),
                  pl.BlockSpec((t, RET_DK), lambda b, c: (c, 0)), pl.BlockSpec((t, RET_DK), lambda b, c: (c, 0)),
                  _resident((1, rw)), _resident(pw.shape), _resident((1, pwid))],
        out_specs=[pl.BlockSpec((t, rw + pwid), lambda b, c: (b * nc + c, 0)),
                   pl.BlockSpec((1, 1, RET_HEADS, RET_DK, RET_DK), lambda b, c: (0, b, 0, 0, 0)),
                   pl.BlockSpec((1, 1, POOL_KEEP, pwid), lambda b, c: (0, b, 0, 0))],
        out_shape=[jax.ShapeDtypeStruct((batch * seq, rw + pwid), bf16),
                   jax.ShapeDtypeStruct((1, batch, RET_HEADS, RET_DK, RET_DK), f32),
                   jax.ShapeDtypeStruct((1, batch, POOL_KEEP, pwid), f32)],
        scratch_shapes=[pltpu.VMEM((RET_HEADS, RET_DK, RET_DK), f32),
                        pltpu.VMEM((POOL_KEEP + 1 + t, pwid), f32),
                        pltpu.VMEM((RET_HEADS, t, t), f32)],
        compiler_params=_params("arbitrary", "arbitrary"),
        name="even_prompt",
    )(z, z, z, z, z, cosf, sinf, gain.reshape(1, rw), pw, ps.reshape(1, pwid))


def _even_sample_kernel(z_ref, cos_ref, sin_ref, s0_ref, rows_ref, gain_ref, pw_ref, ps_ref,
                        mix_ref, sret_ref, spool_ref, ext_ref, *, pos):
    dk = RET_DK
    rw = RET_HEADS * dk
    z = jnp.broadcast_to(z_ref[0], (SUBLANES, z_ref.shape[2]))
    cosf = jnp.broadcast_to(cos_ref[...], (SUBLANES, dk))
    sinf = jnp.broadcast_to(sin_ref[...], (SUBLANES, dk))
    first = lax.broadcasted_iota(i32, (SUBLANES, dk), 0) == 0
    for h in range(RET_HEADS):
        sl = slice(h * dk, (h + 1) * dk)
        gamma = math.exp(_ret_log_gamma(h))
        qr = _rotate(z[:, sl], cosf, sinf)
        kr = _rotate(z[:, rw + h * dk:rw + (h + 1) * dk], cosf, sinf) * (dk ** -0.5)
        v = z[:, 2 * rw + h * dk:2 * rw + (h + 1) * dk]
        s_new = s0_ref[0, 0, h] * gamma + _dot_tn(jnp.where(first, kr, 0.0), v)
        sret_ref[0, 0, h] = s_new
        o = _dot(qr, s_new)
        gate = z[:, 3 * rw + h * dk:3 * rw + (h + 1) * dk]
        mix_ref[0, :, sl] = _group_norm_gate(o, gain_ref[:, sl], gate)[0:1]

    keep = POOL_KEEP
    pwid = ps_ref.shape[1]
    p = z[0:1, 4 * rw:4 * rw + pwid]
    ext_ref[0:keep, :] = rows_ref[0, 0]
    ext_ref[keep:keep + 1, :] = p
    spool_ref[0, 0] = ext_ref[1:keep + 1, :]
    gw = pwid // len(POOL_WINDOWS)
    for gi, w in enumerate(POOL_WINDOWS):
        sl = slice(gi * gw, (gi + 1) * gw)
        win = jnp.sum(ext_ref[keep + 1 - w:keep + 1, sl], axis=0, keepdims=True)
        pooled = win / float(min(w, pos + 1)) - p[:, sl]
        mixed = _dot(jnp.broadcast_to(pooled, (SUBLANES, gw)), pw_ref[gi])[0:1] * ps_ref[:, sl]
        mix_ref[0, :, rw + gi * gw:rw + (gi + 1) * gw] = mixed


def _even_sample(z, cosf, sinf, state_ret, state_pool, gain, pw, ps, pos):
    nb, zw = z.shape
    rw = RET_HEADS * RET_DK
    pwid = ps.shape[0]
    return pl.pallas_call(
        functools.partial(_even_sample_kernel, pos=pos),
        grid=(nb,),
        in_specs=[pl.BlockSpec((1, 1, zw), lambda b: (b, 0, 0)),
                  _resident((1, RET_DK)), _resident((1, RET_DK)),
                  pl.BlockSpec((1, 1, RET_HEADS, RET_DK, RET_DK), lambda b: (0, b, 0, 0, 0)),
                  pl.BlockSpec((1, 1, POOL_KEEP, pwid), lambda b: (0, b, 0, 0)),
                  _resident((1, rw)), _resident(pw.shape), _resident((1, pwid))],
        out_specs=[pl.BlockSpec((1, 1, rw + pwid), lambda b: (b, 0, 0)),
                   pl.BlockSpec((1, 1, RET_HEADS, RET_DK, RET_DK), lambda b: (0, b, 0, 0, 0)),
                   pl.BlockSpec((1, 1, POOL_KEEP, pwid), lambda b: (0, b, 0, 0))],
        out_shape=[jax.ShapeDtypeStruct((nb, 1, rw + pwid), f32),
                   jax.ShapeDtypeStruct((1, nb, RET_HEADS, RET_DK, RET_DK), f32),
                   jax.ShapeDtypeStruct((1, nb, POOL_KEEP, pwid), f32)],
        scratch_shapes=[pltpu.VMEM((POOL_KEEP + 1, pwid), f32)],
        compiler_params=_params("arbitrary"),
        name="even_sample",
    )(z.reshape(nb, 1, zw), cosf, sinf, state_ret, state_pool, gain.reshape(1, rw), pw, ps.reshape(1, pwid))


def _rel_bias_of(dist, rb_ref, h):
    n = jnp.maximum(dist, 0)
    max_exact = REL_BUCKETS // 2
    nf = jnp.maximum(n, 1).astype(f32)
    large = max_exact + (jnp.log(nf / max_exact) / math.log(REL_MAX_DIST / max_exact)
                         * (REL_BUCKETS - max_exact)).astype(i32)
    large = jnp.minimum(large, REL_BUCKETS - 1)
    bucket = jnp.where(n < max_exact, n, large)
    out = jnp.zeros(dist.shape, f32)
    for bk in range(REL_BUCKETS):
        out = jnp.where(bucket == bk, rb_ref[bk, h], out)
    return out


def _bias_kernel(rb_ref, lq1_ref, lk1_ref, lq2_ref, lk2_ref, tile_ref, tab_ref, lam_ref, *, t, lam_init):
    h = pl.program_id(0)
    nd = REL_MAX_DIST
    rel = lax.broadcasted_iota(i32, (nd, nd), 1) - lax.broadcasted_iota(i32, (nd, nd), 0)
    on_diag = jnp.where(rel >= 0, _rel_bias_of(rel, rb_ref, h) * LOG2E, NEG_INF)
    next_diag = _rel_bias_of(rel + nd, rb_ref, h) * LOG2E
    far = _rel_bias_of(jnp.full((nd, nd), nd, i32), rb_ref, h) * LOG2E
    masked = jnp.full((nd, nd), NEG_INF, f32)
    nblk = t // nd
    for bk in range(nblk):
        for bq in range(nblk):
            diag_blk = on_diag if bq == bk else next_diag if bq == bk + 1 else far if bq > bk else masked
            below_blk = next_diag if (bk == nblk - 1 and bq == 0) else far
            tile_ref[0, 0, bk * nd:(bk + 1) * nd, bq * nd:(bq + 1) * nd] = diag_blk
            tile_ref[0, 1, bk * nd:(bk + 1) * nd, bq * nd:(bq + 1) * nd] = below_blk
    tab_ref[0] = _rel_bias_of(lax.broadcasted_iota(i32, (1, 2 * REL_MAX_DIST), 1), rb_ref, h)
    lam = (jnp.exp(jnp.sum(lq1_ref[...] * lk1_ref[...], axis=-1, keepdims=True))
           - jnp.exp(jnp.sum(lq2_ref[...] * lk2_ref[...], axis=-1, keepdims=True)) + lam_init)
    lam_ref[...] = jnp.broadcast_to(lam, lam_ref.shape)


def _bias_tables(rel_bias, lq1, lk1, lq2, lk2, lam_init, t):
    nh = rel_bias.shape[1]
    d = lq1.shape[0]
    vec = lambda a: a.reshape(1, d)
    return pl.pallas_call(
        functools.partial(_bias_kernel, t=t, lam_init=lam_init),
        grid=(nh,),
        in_specs=[pl.BlockSpec(memory_space=pltpu.SMEM)] + [_resident((1, d))] * 4,
        out_specs=[pl.BlockSpec((1, 2, t, t), lambda h: (h, 0, 0, 0)),
                   pl.BlockSpec((1, 1, 2 * REL_MAX_DIST), lambda h: (h, 0, 0)),
                   pl.BlockSpec((SUBLANES, LANES), lambda h: (0, 0))],
        out_shape=[jax.ShapeDtypeStruct((nh, 2, t, t), f32),
                   jax.ShapeDtypeStruct((nh, 1, 2 * REL_MAX_DIST), f32),
                   jax.ShapeDtypeStruct((SUBLANES, LANES), f32)],
        compiler_params=_params("arbitrary"),
        name="bias_tables",
    )(rel_bias, vec(lq1), vec(lk1), vec(lq2), vec(lk2))


def _sub_ln(o, gain, lam_init):
    return o * lax.rsqrt(jnp.mean(o * o, axis=-1, keepdims=True) + EPS) * gain * (1.0 - lam_init)


def _attn_prompt_tile(lam_ref, far_ref, q_ref, k_ref, vt_ref, tile_ref, gain_ref, o_ref,
                      qq_ref, m_ref, l_ref, acc_ref, s0_ref, smax0_ref, s1_ref, smax1_ref, *, t, cb, lam_init):
    h = pl.program_id(1)
    qi = pl.program_id(2)
    q = q_ref[...]
    lane = lax.broadcasted_iota(i32, q.shape, 1)
    zero = jnp.zeros_like(q)
    qq_ref[0:t, :] = jnp.where(lane < DIFF_D, q, zero)
    qq_ref[t:2 * t, :] = jnp.where(lane >= DIFF_D, q, zero)
    m_ref[...] = jnp.full(m_ref.shape, NEG_INF, f32)
    l_ref[...] = jnp.zeros(l_ref.shape, f32)
    acc_ref[...] = jnp.zeros(acc_ref.shape, f32)

    far = far_ref[h] * LOG2E

    blocks = [slice(c0, c0 + cb) for c0 in range(0, 2 * t, cb)]

    parked = ((s0_ref, smax0_ref), (s1_ref, smax1_ref))

    def scores(ki, slot):
        s_ref, smax_ref = parked[slot]
        kt = k_ref[pl.ds(pl.multiple_of(ki * t, t), t), :]
        for cols in blocks:
            s = _dot_nt(kt, qq_ref[cols, :])
            s_ref[:, cols] = s
            smax_ref[:, cols] = jnp.max(s, axis=0, keepdims=True)

    def absorb(ki, slot, tile):
        s_ref, smax_ref = parked[slot]
        vt = vt_ref[:, pl.ds(pl.multiple_of(ki * t, t), t)]
        for cols in blocks:
            m_old = m_ref[:, cols]
            if tile is None:
                m_new = jnp.maximum(m_old, smax_ref[:, cols] + far)
                p = jnp.exp2(s_ref[:, cols] - (m_new - far))
            else:
                s = s_ref[:, cols] + tile_ref[0, tile, :, cols.start % t:cols.start % t + cb]
                m_new = jnp.maximum(m_old, jnp.max(s, axis=0, keepdims=True))
                p = jnp.exp2(s - m_new)
            alpha = jnp.exp2(m_old - m_new)
            l_ref[:, cols] = alpha * l_ref[:, cols] + jnp.sum(p, axis=0, keepdims=True)
            acc_ref[:, cols] = alpha * acc_ref[:, cols] + _dot(vt, p.astype(bf16))
            m_ref[:, cols] = m_new

    def stage(ki, slot, tile):
        scores(ki + 1, 1 - slot)
        absorb(ki, slot, tile)

    n_far = jnp.maximum(qi - 1, 0)
    scores(0, 0)

    def far_pair(j, carry):
        stage(2 * j, 0, None)
        stage(2 * j + 1, 1, None)
        return carry

    lax.fori_loop(0, n_far // 2, far_pair, 0)

    @pl.when(n_far % 2 == 1)
    def _():
        stage(n_far - 1, 0, None)

    for slot in (0, 1):
        @pl.when(jnp.logical_and(qi >= 1, n_far % 2 == slot))
        def _():
            stage(qi - 1, slot, 1)

    for slot in (0, 1):
        @pl.when(qi % 2 == slot)
        def _():
            absorb(qi, slot, 0)

    o = acc_ref[...] / l_ref[...]
    o = o[:, 0:t] - lam_ref[0] * o[:, t:2 * t]
    y = o * lax.rsqrt(jnp.mean(o * o, axis=0, keepdims=True) + EPS) * gain_ref[...] * (1.0 - lam_init)
    o_ref[...] = y.T.astype(o_ref.dtype)


def _attn_decode_chunk(pt_ref, lam_ref, q_ref, kn_ref, vn_ref, bias_ref, b0_ref, gain_ref, k_refs, v_refs, o_ref,
                       qh_ref, m_ref, l_ref, acc_ref, *, chunk_axis, lam_init):
    del pt_ref
    c = pl.program_id(chunk_axis)
    nc = pl.num_programs(chunk_axis)
    nh = DIFF_HEADS
    prow = PAGE_SIZE * nh

    def both_streams(x):
        return jnp.concatenate([x, x], axis=0)

    @pl.when(c == 0)
    def _():
        q = q_ref[0]
        lane = lax.broadcasted_iota(i32, q.shape, 1)
        qh = jnp.concatenate([jnp.where(lane < DIFF_D, q, 0.0), jnp.where(lane >= DIFF_D, q, 0.0)],
                             axis=0).astype(bf16)
        qh_ref[...] = qh
        kn = both_streams(kn_ref[0]).astype(bf16).astype(f32)
        m_ref[...] = jnp.sum(qh.astype(f32) * kn, axis=-1, keepdims=True) + b0_ref[...]
        l_ref[...] = jnp.ones(l_ref.shape, f32)
        acc_ref[...] = both_streams(vn_ref[0]).astype(bf16).astype(f32)

    qh = qh_ref[...]
    s = jnp.concatenate([_dot_nt(qh, k_ref[...].astype(bf16)) for k_ref in k_refs], axis=1)
    s = s + bias_ref[(c == nc - 1).astype(i32)]
    m_old = m_ref[...]
    m_new = jnp.maximum(m_old, jnp.max(s, axis=-1, keepdims=True))
    alpha = jnp.exp(m_old - m_new)
    p = jnp.exp(s - m_new)
    l_ref[...] = alpha * l_ref[...] + jnp.sum(p, axis=-1, keepdims=True)
    pv = alpha * acc_ref[...]
    for i, v_ref in enumerate(v_refs):
        pv = pv + _dot(p[:, i * prow:(i + 1) * prow].astype(bf16), v_ref[...].astype(bf16))
    acc_ref[...] = pv
    m_ref[...] = m_new

    @pl.when(c == nc - 1)
    def _():
        on = acc_ref[...] / l_ref[...]
        o_ref[0] = _sub_ln(on[0:nh] - lam_ref[0] * on[nh:2 * nh], gain_ref[...], lam_init)


N_PROMPT_SCRATCH = 8


def _attn_kernel(pt_ref, lam_ref, far_ref, q_ref, k_ref, vt_ref, tile_ref, gain_col_ref,
                 qs_ref, kn_ref, vn_ref, bias_ref, b0_ref, gain_row_ref, *rest, t, cb, npg, lam_init):
    k_pages, v_pages = rest[:npg], rest[npg:2 * npg]
    o_ref, os_ref = rest[2 * npg:2 * npg + 2]
    scratch = rest[2 * npg + 2:]
    _attn_decode_chunk(pt_ref, lam_ref, qs_ref, kn_ref, vn_ref, bias_ref, b0_ref, gain_row_ref, k_pages, v_pages,
                       os_ref, *scratch[N_PROMPT_SCRATCH:], chunk_axis=2, lam_init=lam_init)
    _attn_prompt_tile(lam_ref, far_ref, q_ref, k_ref, vt_ref, tile_ref, gain_col_ref, o_ref,
                      *scratch[:N_PROMPT_SCRATCH], t=t, cb=cb, lam_init=lam_init)


def _attn(q, k, vt, tiles, far, lam, gain, batch, seq, qs, kn, vn, cache_k, cache_v, layer, page_table, bias2, b0,
          lam_init):
    t = ATT_TILE
    nq = seq // t
    hd = DIFF_HD
    nh = DIFF_HEADS
    nb = qs.shape[0]
    pool = cache_k.shape[1]
    npg = page_table.shape[1] // nq
    assert nb == batch * nh and npg * nq == page_table.shape[1], "one sample sequence per prompt (batch, head)"
    nrow = 2 * nh
    prow = PAGE_SIZE * nh
    smem = pl.BlockSpec(memory_space=pltpu.SMEM)
    page_spec = lambda j: pl.BlockSpec(
        (prow, hd), lambda b, h, i, pt: (layer * pool + pt[b * nh + h, i * npg + j], 0))
    rowspec = pl.BlockSpec((1, nh, hd), lambda b, h, i, pt: (b * nh + h, 0, 0))
    const = lambda shape: pl.BlockSpec(shape, lambda b, h, i, pt: (0,) * len(shape), pipeline_mode=pl.Buffered(1))
    grid_spec = pltpu.PrefetchScalarGridSpec(
        num_scalar_prefetch=1,
        grid=(batch, nh, nq),
        in_specs=[smem, smem,
                  pl.BlockSpec((t, hd), lambda b, h, i, pt: (b * nq + i, h)),
                  pl.BlockSpec((seq, hd), lambda b, h, i, pt: (b, h)),
                  pl.BlockSpec((hd, seq), lambda b, h, i, pt: (b * nh + h, 0)),
                  pl.BlockSpec((1, 2, t, t), lambda b, h, i, pt: (h, 0, 0, 0)),
                  const((hd, 1)),
                  rowspec, rowspec, rowspec, const((2, nrow, npg * prow)), const((nrow, 1)), const((1, hd))]
                 + [page_spec(j) for j in range(npg)] * 2,
        out_specs=[pl.BlockSpec((t, hd), lambda b, h, i, pt: (b * nq + i, h)), rowspec],
        scratch_shapes=[pltpu.VMEM((2 * t, hd), bf16), pltpu.VMEM((1, 2 * t), f32),
                        pltpu.VMEM((1, 2 * t), f32), pltpu.VMEM((hd, 2 * t), f32),
                        pltpu.VMEM((t, 2 * t), f32), pltpu.VMEM((1, 2 * t), f32),
                        pltpu.VMEM((t, 2 * t), f32), pltpu.VMEM((1, 2 * t), f32),
                        pltpu.VMEM((nrow, hd), bf16), pltpu.VMEM((nrow, 1), f32),
                        pltpu.VMEM((nrow, 1), f32), pltpu.VMEM((nrow, hd), f32)],
    )
    ck = cache_k.reshape(-1, hd)
    cv = cache_v.reshape(-1, hd)
    return pl.pallas_call(
        functools.partial(_attn_kernel, t=t, cb=ATT_QUERY_BLOCK, npg=npg, lam_init=lam_init),
        grid_spec=grid_spec,
        out_shape=[jax.ShapeDtypeStruct((batch * seq, nh * hd), bf16), jax.ShapeDtypeStruct((nb, nh, hd), f32)],
        compiler_params=pltpu.CompilerParams(dimension_semantics=("arbitrary",) * 3,
                                             vmem_limit_bytes=ATTN_VMEM_LIMIT_BYTES),
        name="attn",
    )(page_table, lam, far, q, k, vt, tiles, gain.reshape(hd, 1), qs, kn, vn, bias2, b0, gain.reshape(1, hd),
      *([ck] * npg), *([cv] * npg))


def _softmax_rows(s):
    e = jnp.exp(s - jnp.max(s, axis=-1, keepdims=True))
    return e / jnp.sum(e, axis=-1, keepdims=True)


def _cross_prompt_kernel(x_ref, g_ref, wq_ref, mk_ref, mv_ref, wo_ref, o_ref, a_ref):
    x = x_ref[...]
    d = x.shape[1]
    hd = d // MEM_HEADS
    h = _rms(x, g_ref[...]).astype(bf16)
    q = (_dot(h, wq_ref[...]) * (hd ** -0.5)).astype(bf16)
    for hh in range(MEM_HEADS):
        sl = slice(hh * hd, (hh + 1) * hd)
        a = _softmax_rows(_dot_nt(q[:, sl], mk_ref[:, sl])).astype(bf16)
        a_ref[:, sl] = _dot(a, mv_ref[:, sl]).astype(bf16)
    o_ref[...] = x + _dot(a_ref[...], wo_ref[...])


def _cross_prompt(x, g, wq, mk, mv, wo, seq):
    m, d = x.shape
    _, wq_spec, wq = _stacked(wq)
    _, wo_spec, wo = _stacked(wo)
    tm = ROW_TILE
    per = seq // tm
    mem = mk.shape[0] // (m // seq)
    return pl.pallas_call(
        _cross_prompt_kernel,
        grid=(m // tm,),
        in_specs=[pl.BlockSpec((tm, d), lambda i: (i, 0)), _resident((1, d)), wq_spec,
                  pl.BlockSpec((mem, d), lambda i: (i // per, 0)), pl.BlockSpec((mem, d), lambda i: (i // per, 0)),
                  wo_spec],
        out_specs=pl.BlockSpec((tm, d), lambda i: (i, 0)),
        out_shape=jax.ShapeDtypeStruct((m, d), f32),
        scratch_shapes=[pltpu.VMEM((tm, d), bf16)],
        compiler_params=_params("arbitrary"),
        name="cross_prompt",
    )(x, g.reshape(1, d), wq, mk, mv, wo)


def _cross_sample_kernel(q_ref, mk_ref, mv_ref, mask_ref, o_ref):
    nh, hd = q_ref.shape[1], q_ref.shape[2]
    q = jnp.concatenate([q_ref[0] * (hd ** -0.5), jnp.zeros((SUBLANES - nh, hd), f32)], axis=0)
    s = _dot_nt(q.astype(bf16), mk_ref[...].astype(bf16)) + mask_ref[...]
    o = _dot(_softmax_rows(s).astype(bf16), mv_ref[...].astype(bf16))
    o_ref[0] = o[0:nh]


def _cross_sample(q, cache_mk, cache_mv, layer):
    nb, nh, hd = q.shape
    rows = cache_mk.shape[2] * nh
    head_of_row = jnp.arange(rows, dtype=i32) % nh
    mask = jnp.where(head_of_row[None, :] == jnp.arange(SUBLANES, dtype=i32)[:, None], 0.0, NEG_INF).astype(f32)
    mask = mask.at[nh:].set(0.0)
    mem_spec = pl.BlockSpec((rows, hd), lambda b: (layer * nb + b, 0))
    return pl.pallas_call(
        _cross_sample_kernel,
        grid=(nb,),
        in_specs=[pl.BlockSpec((1, nh, hd), lambda b: (b, 0, 0)), mem_spec, mem_spec, _resident((SUBLANES, rows))],
        out_specs=pl.BlockSpec((1, nh, hd), lambda b: (b, 0, 0)),
        out_shape=jax.ShapeDtypeStruct((nb, nh, hd), f32),
        compiler_params=_params("arbitrary"),
        name="cross_sample",
    )(q, cache_mk.reshape(-1, hd), cache_mv.reshape(-1, hd), mask)


def _rotary_tables(pos, half):
    inv = 1.0 / (10000.0 ** jnp.linspace(0.0, 1.0, half, dtype=f32))
    ang = pos.astype(f32)[:, None] * inv[None, :]
    cos, sin = jnp.cos(ang), jnp.sin(ang)
    return jnp.concatenate([cos, cos], axis=-1), jnp.concatenate([-sin, sin], axis=-1)


def kernel(x_prompt, x_sample, state_ret, state_pool, cache_k_diff, cache_v_diff, cache_mem_k, cache_mem_v, page_table, mem_prompt, norms, final_norm, ffn_w_gate, ffn_w_up, ffn_w_down, w_in_even, ret_gain, pool_w, pool_scale, w_out_even, w_qkv_odd, lambda_q1, lambda_k1, lambda_q2, lambda_k2, subln_gain, w_out_odd, rel_bias, w_cq, w_ckv, w_co):
    batch, seq, d = x_prompt.shape
    nb = x_sample.shape[0]
    depth = norms.shape[0]
    past = page_table.shape[1] * PAGE_SIZE
    mem_len = mem_prompt.shape[1]
    rw = RET_HEADS * RET_DK
    dw = DIFF_HEADS * DIFF_HD

    xp = x_prompt.reshape(batch * seq, d)
    xs = x_sample.reshape(nb, d)
    mem2d = mem_prompt.reshape(batch * mem_len, d)

    cos_p, sin_p = _rotary_tables(jnp.arange(seq, dtype=i32), RET_DK // 2)
    cos_s, sin_s = _rotary_tables(jnp.full((1,), past, dtype=i32), RET_DK // 2)

    wg, wu, wd = ffn_w_gate.astype(bf16), ffn_w_up.astype(bf16), ffn_w_down.astype(bf16)
    w_in, w_oe = w_in_even.astype(bf16), w_out_even.astype(bf16)
    w_qkv, w_oo = w_qkv_odd.astype(bf16), w_out_odd.astype(bf16)
    wcq, wckv, wco = w_cq.astype(bf16), w_ckv.astype(bf16), w_co.astype(bf16)
    pw = pool_w.astype(bf16)

    p_ret, p_pool, p_k, p_v, p_mk, p_mv = [], [], [], [], [], []
    s_ret, s_pool, s_k, s_v = [], [], [], []
    for l in range(depth):
        last = l == depth - 1
        ffn_w = lambda i: ((wg, (l, i)), (wu, (l, i)), (wd, (l, i)))
        xp = _ffn(xp, norms[l, 0], *ffn_w(0))
        xs = _ffn(xs, norms[l, 0], *ffn_w(0))
        if l % 2 == 0:
            e = l // 2
            zw = w_in.shape[2]
            (zp,) = _proj(xp, norms[l, 1], (w_in, (e,)), [(0, zw, [(0, 1.0)])], [(zw, f32)])
            (zs,) = _proj(xs, norms[l, 1], (w_in, (e,)), [(0, zw, [(0, 1.0)])], [(zw, f32)])
            mixp, rp, pp = _even_prompt(zp, cos_p, sin_p, ret_gain[e], pw[e], pool_scale[e], batch, seq)
            mixs, rs, ps = _even_sample(zs, cos_s, sin_s, state_ret[e:e + 1], state_pool[e:e + 1], ret_gain[e],
                                        pool_w[e], pool_scale[e], past)
            xp = _mmres(xp, mixp, (w_oe, (e,)))
            xs = _mmres(xs, mixs.reshape(nb, rw + pool_scale.shape[1]), (w_oe, (e,)))
            p_ret.append(rp[0])
            p_pool.append(pp[0])
            s_ret.append(rs[0])
            s_pool.append(ps[0])
        else:
            o = l // 2
            lam_init = 0.8 - 0.6 * math.exp(-0.3 * l)
            scale = DIFF_D ** -0.5
            tiles, tab, lam_t = _bias_tables(rel_bias, lambda_q1[o], lambda_k1[o], lambda_q2[o], lambda_k2[o],
                                             lam_init, ATT_TILE)
            lam = lam_t[0, 0:1]
            tab = tab[:, 0, :]
            far = tab[:, REL_MAX_DIST]
            qkv_groups = [(0, dw, [(0, scale * LOG2E)]), (dw, dw, [(1, 1.0), (3, 1.0)]),
                          (2 * dw, dw, [(2, 1.0), (4, 1.0)])]
            qb, kp, vp, kb, vtb = _proj(xp, norms[l, 1], (w_qkv, (o,)), qkv_groups,
                                        [(dw, bf16), (dw, f32), (dw, f32), (dw, bf16), (dw, bf16, "T")], seq=seq)
            qs, ks_, vs = _proj(xs, norms[l, 1], (w_qkv, (o,)),
                                [(0, dw, [(0, scale)]), (dw, dw, [(1, 1.0)]), (2 * dw, dw, [(2, 1.0)])],
                                [(dw, f32), (dw, f32), (dw, f32)])
            nh = DIFF_HEADS
            pages_per_step = page_table.shape[1] // (seq // ATT_TILE)
            own = jnp.arange(nh)[:, None, None] == jnp.arange(nh)[None, None, :]
            far_page = jnp.where(own, far[:, None, None], NEG_INF) + jnp.zeros((1, PAGE_SIZE, 1), f32)
            last_page = jnp.where(own, tab[:, PAGE_SIZE:0:-1][:, :, None], NEG_INF)
            far_page, last_page = far_page.reshape(nh, -1), last_page.reshape(nh, -1)
            far_step = jnp.tile(far_page, (2, pages_per_step))
            last_step = jnp.concatenate([far_step[:, :-PAGE_SIZE * nh], jnp.tile(last_page, (2, 1))], axis=1)
            bias2 = jnp.stack([far_step, last_step])
            b0 = jnp.tile(tab[:, 0], 2)[:, None]
            heads = lambda a: a.reshape(nb, nh, DIFF_HD)
            attp, atts = _attn(qb, kb, vtb, tiles, far, lam, subln_gain[o], batch, seq,
                               heads(qs), heads(ks_), heads(vs), cache_k_diff, cache_v_diff, o, page_table,
                               bias2, b0, lam_init)
            xp = _mmres(xp, attp, (w_oo, (o,)))
            xs = _mmres(xs, atts.reshape(nb, dw), (w_oo, (o,)))
            p_k.append(kp.reshape(batch, seq, DIFF_HEADS, DIFF_HD))
            p_v.append(vp.reshape(batch, seq, DIFF_HEADS, DIFF_HD))
            s_k.append(ks_.reshape(nb, 1, DIFF_HEADS, DIFF_HD))
            s_v.append(vs.reshape(nb, 1, DIFF_HEADS, DIFF_HD))
        mk, mv, mkb, mvb = _proj(mem2d, None, (wckv, (l,)),
                                 [(0, d, [(0, 1.0), (2, 1.0)]), (d, d, [(1, 1.0), (3, 1.0)])],
                                 [(d, f32), (d, f32), (d, bf16), (d, bf16)])
        p_mk.append(mk.reshape(batch, mem_len, MEM_HEADS, d // MEM_HEADS))
        p_mv.append(mv.reshape(batch, mem_len, MEM_HEADS, d // MEM_HEADS))
        xp = _cross_prompt(xp, norms[l, 2], (wcq, (l,)), mkb, mvb, (wco, (l,)), seq)
        (qcs,) = _proj(xs, norms[l, 2], (wcq, (l,)), [(0, d, [(0, 1.0)])], [(d, f32)])
        acs = _cross_sample(qcs.reshape(nb, MEM_HEADS, d // MEM_HEADS), cache_mem_k, cache_mem_v, l)
        xs = _mmres(xs, acs.reshape(nb, d), (wco, (l,)))
        fin = final_norm if last else None
        xp = _ffn(xp, norms[l, 3], *ffn_w(1), fin)
        xs = _ffn(xs, norms[l, 3], *ffn_w(1), fin)

    return (xp.reshape(batch, seq, d), xs.reshape(nb, 1, d),
            jnp.stack(p_ret), jnp.stack(p_pool), jnp.stack(p_k), jnp.stack(p_v), jnp.stack(p_mk), jnp.stack(p_mv),
            jnp.stack(s_ret), jnp.stack(s_pool), jnp.stack(s_k), jnp.stack(s_v))
```

```python
import functools
import math

import jax
import jax.numpy as jnp
from jax import lax
from jax.experimental import pallas as pl
from jax.experimental.pallas import tpu as pltpu

f32 = jnp.float32
bf16 = jnp.bfloat16
i32 = jnp.int32

EPS = 1e-6
NEG_INF = -1e30
LOG2E = math.log2(math.e)

RET_HEADS = 4
RET_DK = 128
POOL_WINDOWS = (2, 4, 8, 16)
POOL_KEEP = max(POOL_WINDOWS) - 1
DIFF_HEADS = 8
DIFF_D = 64
DIFF_HD = 2 * DIFF_D
REL_BUCKETS = 32
REL_MAX_DIST = 128
MEM_HEADS = 4
PAGE_SIZE = 128

VMEM_LIMIT_BYTES = 56 * 1024 * 1024
LANES = 128
SUBLANES = 8

ROW_TILE = 512
ATT_TILE = 512
ATT_QUERY_BLOCK = 512
ATTN_VMEM_LIMIT_BYTES = 60 * 1024 * 1024


def _params(*sem):
    return pltpu.CompilerParams(dimension_semantics=sem, vmem_limit_bytes=VMEM_LIMIT_BYTES)


def _rms(x, g):
    return x * lax.rsqrt(jnp.mean(x * x, axis=-1, keepdims=True) + EPS) * g


def _dot(a, b):
    return jnp.dot(a, b, preferred_element_type=f32)


def _dot_nt(a, b):
    return lax.dot_general(a, b, (((1,), (1,)), ((), ())), preferred_element_type=f32)


def _dot_tn(a, b):
    return lax.dot_general(a, b, (((0,), (0,)), ((), ())), preferred_element_type=f32)


def _resident(shape, lead=()):
    nd = len(shape)
    return pl.BlockSpec((None,) * len(lead) + tuple(shape), lambda *_: tuple(lead) + (0,) * nd,
                        pipeline_mode=pl.Buffered(1))


def _stacked(w):
    arr, lead = w
    shape = arr.shape[len(lead):]
    return shape, _resident(shape, lead), arr


def _ffn_kernel(x_ref, g_ref, wg_ref, wu_ref, wd_ref, *rest, fchunk, final):
    o_ref = rest[-1]
    x = x_ref[...]
    h = _rms(x, g_ref[...]).astype(bf16)
    acc = jnp.zeros(x.shape, f32)
    for c0 in range(0, wg_ref.shape[1], fchunk):
        gate = _dot(h, wg_ref[:, c0:c0 + fchunk])
        up = _dot(h, wu_ref[:, c0:c0 + fchunk])
        a = (jax.nn.silu(gate) * up).astype(bf16)
        acc = acc + _dot(a, wd_ref[c0:c0 + fchunk, :])
    y = x + 0.5 * acc
    if final:
        y = _rms(y, rest[0][...])
    o_ref[...] = y


def _ffn(x, g, wg, wu, wd, final_g=None):
    m, d = x.shape
    (_, f), wg_spec, wg = _stacked(wg)
    _, wu_spec, wu = _stacked(wu)
    _, wd_spec, wd = _stacked(wd)
    tm = min(ROW_TILE, m)
    fchunk = f // 2 if (f // 2) % LANES == 0 else f
    final = final_g is not None
    in_specs = [pl.BlockSpec((tm, d), lambda i: (i, 0)), _resident((1, d)), wg_spec, wu_spec, wd_spec]
    args = [x, g.reshape(1, d), wg, wu, wd]
    if final:
        in_specs.append(_resident((1, d)))
        args.append(final_g.reshape(1, d))
    return pl.pallas_call(
        functools.partial(_ffn_kernel, fchunk=fchunk, final=final),
        grid=(m // tm,),
        in_specs=in_specs,
        out_specs=pl.BlockSpec((tm, d), lambda i: (i, 0)),
        out_shape=jax.ShapeDtypeStruct((m, d), f32),
        compiler_params=_params("arbitrary"),
        name="ffn",
    )(*args)


def _proj_kernel(*refs, norm, groups, transposed, nchunk):
    x_ref = refs[0]
    g_ref = refs[1] if norm else None
    w_ref = refs[2] if norm else refs[1]
    o_refs = refs[(3 if norm else 2):]
    x = x_ref[...]
    h = (_rms(x, g_ref[...]) if norm else x).astype(bf16)
    for off, width, targets in groups:
        for c0 in range(0, width, nchunk):
            n = min(nchunk, width - c0)
            r = _dot(h, w_ref[:, off + c0:off + c0 + n])
            for oi, scale in targets:
                o_ref = o_refs[oi]
                val = (r if scale == 1.0 else r * scale)
                if transposed[oi]:
                    o_ref[c0:c0 + n, :] = val.T.astype(o_ref.dtype)
                else:
                    o_ref[:, c0:c0 + n] = val.astype(o_ref.dtype)


def _proj(x, g, w, groups, outs, seq=None):
    m, d = x.shape
    _, w_spec, w = _stacked(w)
    tm = min(ROW_TILE, m)
    norm = g is not None
    in_specs = [pl.BlockSpec((tm, d), lambda i: (i, 0))]
    args = [x]
    if norm:
        in_specs.append(_resident((1, d)))
        args.append(g.reshape(1, d))
    in_specs.append(w_spec)
    args.append(w)
    out_specs, out_shape = [], []
    for wd, dt, *flag in outs:
        if flag:
            per = seq // tm
            out_specs.append(pl.BlockSpec((wd, tm), lambda i: (i // per, i % per)))
            out_shape.append(jax.ShapeDtypeStruct((m // seq * wd, seq), dt))
        else:
            out_specs.append(pl.BlockSpec((tm, wd), lambda i: (i, 0)))
            out_shape.append(jax.ShapeDtypeStruct((m, wd), dt))
    return pl.pallas_call(
        functools.partial(_proj_kernel, norm=norm, groups=groups, transposed=tuple(o[2:] == ("T",) for o in outs),
                          nchunk=512),
        grid=(m // tm,),
        in_specs=in_specs,
        out_specs=out_specs,
        out_shape=out_shape,
        compiler_params=_params("arbitrary"),
        name="proj",
    )(*args)


def _mmres_kernel(x_ref, a_ref, w_ref, o_ref):
    o_ref[...] = x_ref[...] + _dot(a_ref[...].astype(bf16), w_ref[...])


def _mmres(x, a, w):
    m, d = x.shape
    k = a.shape[1]
    _, w_spec, w = _stacked(w)
    tm = min(ROW_TILE, m)
    return pl.pallas_call(
        _mmres_kernel,
        grid=(m // tm,),
        in_specs=[pl.BlockSpec((tm, d), lambda i: (i, 0)), pl.BlockSpec((tm, k), lambda i: (i, 0)), w_spec],
        out_specs=pl.BlockSpec((tm, d), lambda i: (i, 0)),
        out_shape=jax.ShapeDtypeStruct((m, d), f32),
        compiler_params=_params("arbitrary"),
        name="mmres",
    )(x, a, w)


def _ret_log_gamma(h):
    return math.log1p(-(2.0 ** (-5.0 - h)))


def _rotate(x, cosf, sinf):
    return x * cosf + pltpu.roll(x, x.shape[-1] // 2, axis=1) * sinf


def _group_norm_gate(o, gain, gate):
    mu = jnp.mean(o, axis=-1, keepdims=True)
    var = jnp.mean(jnp.square(o - mu), axis=-1, keepdims=True)
    return jax.nn.silu(gate) * ((o - mu) * lax.rsqrt(var + EPS) * gain)


def _even_prompt_kernel(zq_ref, zk_ref, zv_ref, zg_ref, zp_ref, cos_ref, sin_ref, gain_ref, pw_ref, ps_ref,
                        mix_ref, sret_ref, spool_ref, state_ref, ext_ref, intra_ref, *, t):
    b = pl.program_id(0)
    c = pl.program_id(1)
    nc = pl.num_programs(1)
    halo = POOL_KEEP + 1
    dk = RET_DK

    @pl.when(jnp.logical_and(b == 0, c == 0))
    def _():
        rel = lax.broadcasted_iota(i32, (t, t), 0) - lax.broadcasted_iota(i32, (t, t), 1)
        relf = jnp.maximum(rel, 0).astype(f32)
        for h in range(RET_HEADS):
            intra_ref[h] = jnp.where(rel >= 0, jnp.exp(_ret_log_gamma(h) * relf), 0.0)

    @pl.when(c == 0)
    def _():
        state_ref[...] = jnp.zeros(state_ref.shape, f32)
        ext_ref[0:halo, :] = jnp.zeros((halo, ext_ref.shape[1]), f32)

    cosf = cos_ref[...]
    sinf = sin_ref[...]
    row = lax.broadcasted_iota(i32, (t, 1), 0).astype(f32)
    for h in range(RET_HEADS):
        sl = slice(h * dk, (h + 1) * dk)
        lg = _ret_log_gamma(h)
        qr = _rotate(zq_ref[:, sl], cosf, sinf)
        kr = _rotate(zk_ref[:, sl], cosf, sinf) * (dk ** -0.5)
        qb = qr.astype(bf16)
        vb = zv_ref[:, sl].astype(bf16)
        att = _dot_nt(qb, kr.astype(bf16)) * intra_ref[h]
        s_old = state_ref[h]
        o = _dot(att.astype(bf16), vb) + _dot(qb, s_old.astype(bf16)) * jnp.exp(lg * (row + 1.0))
        k_dec = jnp.exp(lg * (float(t - 1) - row))
        state_ref[h] = s_old * math.exp(lg * t) + _dot_tn((kr * k_dec).astype(bf16), vb)
        mix_ref[:, sl] = _group_norm_gate(o, gain_ref[:, sl], zg_ref[:, sl]).astype(mix_ref.dtype)

    p = zp_ref[...]
    ext_ref[halo:halo + t, :] = p
    pos = c * t + lax.broadcasted_iota(i32, (t, 1), 0)
    gw = p.shape[1] // len(POOL_WINDOWS)
    ret_w = RET_HEADS * dk
    for gi, w in enumerate(POOL_WINDOWS):
        sl = slice(gi * gw, (gi + 1) * gw)
        win = p[:, sl]
        for back in range(1, w):
            win = win + ext_ref[halo - back:halo - back + t, sl]
        cnt = jnp.minimum(w, pos + 1).astype(f32)
        pooled = win / cnt - p[:, sl]
        mixed = _dot(pooled.astype(bf16), pw_ref[gi]) * ps_ref[:, sl]
        mix_ref[:, ret_w + gi * gw:ret_w + (gi + 1) * gw] = mixed.astype(mix_ref.dtype)

    @pl.when(c == nc - 1)
    def _():
        sret_ref[0, 0] = state_ref[...]
        spool_ref[0, 0] = ext_ref[t + 1:t + halo, :]

    ext_ref[0:halo, :] = ext_ref[t:t + halo, :]


def _even_prompt(z, cosf, sinf, gain, pw, ps, batch, seq):
    t = ROW_TILE
    nc = seq // t
    rw = RET_HEADS * RET_DK
    pwid = ps.shape[0]
    zspec = lambda j: pl.BlockSpec((t, rw), lambda b, c: (b * nc + c, j))
    return pl.pallas_call(
        functools.partial(_even_prompt_kernel, t=t),
        grid=(batch, nc),
        in_specs=[zspec(0), zspec(1), zspec(2), zspec(3), zspec(4),
                  pl.BlockSpec((t, RET_DK), lambda b, c: (c, 0)), pl.BlockSpec((t, RET_DK), lambda b, c: (c, 0)),
                  _resident((1, rw)), _resident(pw.shape), _resident((1, pwid))],
        out_specs=[pl.BlockSpec((t, rw + pwid), lambda b, c: (b * nc + c, 0)),
                   pl.BlockSpec((1, 1, RET_HEADS, RET_DK, RET_DK), lambda b, c: (0, b, 0, 0, 0)),
                   pl.BlockSpec((1, 1, POOL_KEEP, pwid), lambda b, c: (0, b, 0, 0))],
        out_shape=[jax.ShapeDtypeStruct((batch * seq, rw + pwid), bf16),
                   jax.ShapeDtypeStruct((1, batch, RET_HEADS, RET_DK, RET_DK), f32),
                   jax.ShapeDtypeStruct((1, batch, POOL_KEEP, pwid), f32)],
        scratch_shapes=[pltpu.VMEM((RET_HEADS, RET_DK, RET_DK), f32),
                        pltpu.VMEM((POOL_KEEP + 1 + t, pwid), f32),
                        pltpu.VMEM((RET_HEADS, t, t), f32)],
        compiler_params=_params("arbitrary", "arbitrary"),
        name="even_prompt",
    )(z, z, z, z, z, cosf, sinf, gain.reshape(1, rw), pw, ps.reshape(1, pwid))


def _even_sample_kernel(z_ref, cos_ref, sin_ref, s0_ref, rows_ref, gain_ref, pw_ref, ps_ref,
                        mix_ref, sret_ref, spool_ref, ext_ref, *, pos):
    dk = RET_DK
    rw = RET_HEADS * dk
    z = jnp.broadcast_to(z_ref[0], (SUBLANES, z_ref.shape[2]))
    cosf = jnp.broadcast_to(cos_ref[...], (SUBLANES, dk))
    sinf = jnp.broadcast_to(sin_ref[...], (SUBLANES, dk))
    first = lax.broadcasted_iota(i32, (SUBLANES, dk), 0) == 0
    for h in range(RET_HEADS):
        sl = slice(h * dk, (h + 1) * dk)
        gamma = math.exp(_ret_log_gamma(h))
        qr = _rotate(z[:, sl], cosf, sinf)
        kr = _rotate(z[:, rw + h * dk:rw + (h + 1) * dk], cosf, sinf) * (dk ** -0.5)
        v = z[:, 2 * rw + h * dk:2 * rw + (h + 1) * dk]
        s_new = s0_ref[0, 0, h] * gamma + _dot_tn(jnp.where(first, kr, 0.0), v)
        sret_ref[0, 0, h] = s_new
        o = _dot(qr, s_new)
        gate = z[:, 3 * rw + h * dk:3 * rw + (h + 1) * dk]
        mix_ref[0, :, sl] = _group_norm_gate(o, gain_ref[:, sl], gate)[0:1]

    keep = POOL_KEEP
    pwid = ps_ref.shape[1]
    p = z[0:1, 4 * rw:4 * rw + pwid]
    ext_ref[0:keep, :] = rows_ref[0, 0]
    ext_ref[keep:keep + 1, :] = p
    spool_ref[0, 0] = ext_ref[1:keep + 1, :]
    gw = pwid // len(POOL_WINDOWS)
    for gi, w in enumerate(POOL_WINDOWS):
        sl = slice(gi * gw, (gi + 1) * gw)
        win = jnp.sum(ext_ref[keep + 1 - w:keep + 1, sl], axis=0, keepdims=True)
        pooled = win / float(min(w, pos + 1)) - p[:, sl]
        mixed = _dot(jnp.broadcast_to(pooled, (SUBLANES, gw)), pw_ref[gi])[0:1] * ps_ref[:, sl]
        mix_ref[0, :, rw + gi * gw:rw + (gi + 1) * gw] = mixed


def _even_sample(z, cosf, sinf, state_ret, state_pool, gain, pw, ps, pos):
    nb, zw = z.shape
    rw = RET_HEADS * RET_DK
    pwid = ps.shape[0]
    return pl.pallas_call(
        functools.partial(_even_sample_kernel, pos=pos),
        grid=(nb,),
        in_specs=[pl.BlockSpec((1, 1, zw), lambda b: (b, 0, 0)),
                  _resident((1, RET_DK)), _resident((1, RET_DK)),
                  pl.BlockSpec((1, 1, RET_HEADS, RET_DK, RET_DK), lambda b: (0, b, 0, 0, 0)),
                  pl.BlockSpec((1, 1, POOL_KEEP, pwid), lambda b: (0, b, 0, 0)),
                  _resident((1, rw)), _resident(pw.shape), _resident((1, pwid))],
        out_specs=[pl.BlockSpec((1, 1, rw + pwid), lambda b: (b, 0, 0)),
                   pl.BlockSpec((1, 1, RET_HEADS, RET_DK, RET_DK), lambda b: (0, b, 0, 0, 0)),
                   pl.BlockSpec((1, 1, POOL_KEEP, pwid), lambda b: (0, b, 0, 0))],
        out_shape=[jax.ShapeDtypeStruct((nb, 1, rw + pwid), f32),
                   jax.ShapeDtypeStruct((1, nb, RET_HEADS, RET_DK, RET_DK), f32),
                   jax.ShapeDtypeStruct((1, nb, POOL_KEEP, pwid), f32)],
        scratch_shapes=[pltpu.VMEM((POOL_KEEP + 1, pwid), f32)],
        compiler_params=_params("arbitrary"),
        name="even_sample",
    )(z.reshape(nb, 1, zw), cosf, sinf, state_ret, state_pool, gain.reshape(1, rw), pw, ps.reshape(1, pwid))


def _rel_bias_of(dist, rb_ref, h):
    n = jnp.maximum(dist, 0)
    max_exact = REL_BUCKETS // 2
    nf = jnp.maximum(n, 1).astype(f32)
    large = max_exact + (jnp.log(nf / max_exact) / math.log(REL_MAX_DIST / max_exact)
                         * (REL_BUCKETS - max_exact)).astype(i32)
    large = jnp.minimum(large, REL_BUCKETS - 1)
    bucket = jnp.where(n < max_exact, n, large)
    out = jnp.zeros(dist.shape, f32)
    for bk in range(REL_BUCKETS):
        out = jnp.where(bucket == bk, rb_ref[bk, h], out)
    return out


def _bias_kernel(rb_ref, lq1_ref, lk1_ref, lq2_ref, lk2_ref, tile_ref, tab_ref, lam_ref, *, t, lam_init):
    h = pl.program_id(0)
    nd = REL_MAX_DIST
    rel = lax.broadcasted_iota(i32, (nd, nd), 1) - lax.broadcasted_iota(i32, (nd, nd), 0)
    on_diag = jnp.where(rel >= 0, _rel_bias_of(rel, rb_ref, h) * LOG2E, NEG_INF)
    next_diag = _rel_bias_of(rel + nd, rb_ref, h) * LOG2E
    far = _rel_bias_of(jnp.full((nd, nd), nd, i32), rb_ref, h) * LOG2E
    masked = jnp.full((nd, nd), NEG_INF, f32)
    nblk = t // nd
    for bk in range(nblk):
        for bq in range(nblk):
            diag_blk = on_diag if bq == bk else next_diag if bq == bk + 1 else far if bq > bk else masked
            below_blk = next_diag if (bk == nblk - 1 and bq == 0) else far
            tile_ref[0, 0, bk * nd:(bk + 1) * nd, bq * nd:(bq + 1) * nd] = diag_blk
            tile_ref[0, 1, bk * nd:(bk + 1) * nd, bq * nd:(bq + 1) * nd] = below_blk
    tab_ref[0] = _rel_bias_of(lax.broadcasted_iota(i32, (1, 2 * REL_MAX_DIST), 1), rb_ref, h)
    lam = (jnp.exp(jnp.sum(lq1_ref[...] * lk1_ref[...], axis=-1, keepdims=True))
           - jnp.exp(jnp.sum(lq2_ref[...] * lk2_ref[...], axis=-1, keepdims=True)) + lam_init)
    lam_ref[...] = jnp.broadcast_to(lam, lam_ref.shape)


def _bias_tables(rel_bias, lq1, lk1, lq2, lk2, lam_init, t):
    nh = rel_bias.shape[1]
    d = lq1.shape[0]
    vec = lambda a: a.reshape(1, d)
    return pl.pallas_call(
        functools.partial(_bias_kernel, t=t, lam_init=lam_init),
        grid=(nh,),
        in_specs=[pl.BlockSpec(memory_space=pltpu.SMEM)] + [_resident((1, d))] * 4,
        out_specs=[pl.BlockSpec((1, 2, t, t), lambda h: (h, 0, 0, 0)),
                   pl.BlockSpec((1, 1, 2 * REL_MAX_DIST), lambda h: (h, 0, 0)),
                   pl.BlockSpec((SUBLANES, LANES), lambda h: (0, 0))],
        out_shape=[jax.ShapeDtypeStruct((nh, 2, t, t), f32),
                   jax.ShapeDtypeStruct((nh, 1, 2 * REL_MAX_DIST), f32),
                   jax.ShapeDtypeStruct((SUBLANES, LANES), f32)],
        compiler_params=_params("arbitrary"),
        name="bias_tables",
    )(rel_bias, vec(lq1), vec(lk1), vec(lq2), vec(lk2))


def _sub_ln(o, gain, lam_init):
    return o * lax.rsqrt(jnp.mean(o * o, axis=-1, keepdims=True) + EPS) * gain * (1.0 - lam_init)


def _attn_prompt_tile(lam_ref, far_ref, q_ref, k_ref, vt_ref, tile_ref, gain_ref, o_ref,
                      qq_ref, m_ref, l_ref, acc_ref, s0_ref, smax0_ref, s1_ref, smax1_ref, *, t, cb, lam_init):
    h = pl.program_id(1)
    qi = pl.program_id(2)
    q = q_ref[...]
    lane = lax.broadcasted_iota(i32, q.shape, 1)
    zero = jnp.zeros_like(q)
    qq_ref[0:t, :] = jnp.where(lane < DIFF_D, q, zero)
    qq_ref[t:2 * t, :] = jnp.where(lane >= DIFF_D, q, zero)
    m_ref[...] = jnp.full(m_ref.shape, NEG_INF, f32)
    l_ref[...] = jnp.zeros(l_ref.shape, f32)
    acc_ref[...] = jnp.zeros(acc_ref.shape, f32)

    far = far_ref[h] * LOG2E

    blocks = [slice(c0, c0 + cb) for c0 in range(0, 2 * t, cb)]

    parked = ((s0_ref, smax0_ref), (s1_ref, smax1_ref))

    def scores(ki, slot):
        s_ref, smax_ref = parked[slot]
        kt = k_ref[pl.ds(pl.multiple_of(ki * t, t), t), :]
        for cols in blocks:
            s = _dot_nt(kt, qq_ref[cols, :])
            s_ref[:, cols] = s
            smax_ref[:, cols] = jnp.max(s, axis=0, keepdims=True)

    def scores_own(ki, slot):
        s_ref, _ = parked[slot]
        k0 = pl.multiple_of(ki * t, t)
        for c0 in (0, t):
            for q0, nk in ((0, t // 2), (t // 2, t)):
                cols = slice(c0 + q0, c0 + q0 + t // 2)
                s_ref[0:nk, cols] = _dot_nt(k_ref[pl.ds(k0, nk), :], qq_ref[cols, :])

    def update(cols, m_old, m_new, p, vt):
        alpha = jnp.exp2(m_old - m_new)
        l_ref[:, cols] = alpha * l_ref[:, cols] + jnp.sum(p, axis=0, keepdims=True)
        acc_ref[:, cols] = alpha * acc_ref[:, cols] + _dot(vt, p.astype(bf16))
        m_ref[:, cols] = m_new

    def absorb_far(ki, slot):
        s_ref, smax_ref = parked[slot]
        vt = vt_ref[:, pl.ds(pl.multiple_of(ki * t, t), t)]
        for cols in blocks:
            m_old = m_ref[:, cols]
            m_new = jnp.maximum(m_old, smax_ref[:, cols] + far)
            update(cols, m_old, m_new, jnp.exp2(s_ref[:, cols] - (m_new - far)), vt)

    def absorb_behind(ki, slot):
        s_ref, smax_ref = parked[slot]
        vt = vt_ref[:, pl.ds(pl.multiple_of(ki * t, t), t)]
        corner = tile_ref[0, 1, t - nd:t, 0:nd]
        for c0 in (0, t):
            cols, near, rest = slice(c0, c0 + t), slice(c0, c0 + nd), slice(c0 + nd, c0 + t)
            m_old = m_ref[:, cols]
            s_near = jnp.concatenate([s_ref[0:t - nd, near] + far, s_ref[t - nd:t, near] + corner], axis=0)
            m_near = jnp.maximum(m_old[:, 0:nd], jnp.max(s_near, axis=0, keepdims=True))
            m_rest = jnp.maximum(m_old[:, nd:t], smax_ref[:, rest] + far)
            p = jnp.concatenate([jnp.exp2(s_near - m_near), jnp.exp2(s_ref[:, rest] - (m_rest - far))], axis=1)
            update(cols, m_old, jnp.concatenate([m_near, m_rest], axis=1), p, vt)

    def absorb_own(ki, slot):
        s_ref, _ = parked[slot]
        k0 = pl.multiple_of(ki * t, t)
        for c0 in (0, t):
            for q0, nk in ((0, t // 2), (t // 2, t)):
                cols = slice(c0 + q0, c0 + q0 + t // 2)
                m_old = m_ref[:, cols]
                s = s_ref[0:nk, cols] + tile_ref[0, 0, 0:nk, q0:q0 + t // 2]
                m_new = jnp.maximum(m_old, jnp.max(s, axis=0, keepdims=True))
                update(cols, m_old, m_new, jnp.exp2(s - m_new), vt_ref[:, pl.ds(k0, nk)])

    nd = REL_MAX_DIST
    n_far = jnp.maximum(qi - 1, 0)
    scores(0, 0)

    def far_pair(j, carry):
        for slot in (0, 1):
            scores(2 * j + slot + 1, 1 - slot)
            absorb_far(2 * j + slot, slot)
        return carry

    lax.fori_loop(0, n_far // 2, far_pair, 0)

    @pl.when(n_far % 2 == 1)
    def _():
        scores(n_far, 1)
        absorb_far(n_far - 1, 0)

    for slot in (0, 1):
        @pl.when(jnp.logical_and(qi >= 1, n_far % 2 == slot))
        def _():
            scores_own(qi, 1 - slot)
            absorb_behind(qi - 1, slot)

    for slot in (0, 1):
        @pl.when(qi % 2 == slot)
        def _():
            absorb_own(qi, slot)

    o = acc_ref[...] / l_ref[...]
    o = o[:, 0:t] - lam_ref[0] * o[:, t:2 * t]
    y = o * lax.rsqrt(jnp.mean(o * o, axis=0, keepdims=True) + EPS) * gain_ref[...] * (1.0 - lam_init)
    o_ref[...] = y.T.astype(o_ref.dtype)


def _attn_decode_chunk(pt_ref, lam_ref, q_ref, kn_ref, vn_ref, bias_ref, b0_ref, gain_ref, k_refs, v_refs, o_ref,
                       qh_ref, m_ref, l_ref, acc_ref, *, chunk_axis, lam_init):
    del pt_ref
    c = pl.program_id(chunk_axis)
    nc = pl.num_programs(chunk_axis)
    nh = DIFF_HEADS
    prow = PAGE_SIZE * nh

    def both_streams(x):
        return jnp.concatenate([x, x], axis=0)

    @pl.when(c == 0)
    def _():
        q = q_ref[0]
        lane = lax.broadcasted_iota(i32, q.shape, 1)
        qh = jnp.concatenate([jnp.where(lane < DIFF_D, q, 0.0), jnp.where(lane >= DIFF_D, q, 0.0)],
                             axis=0).astype(bf16)
        qh_ref[...] = qh
        kn = both_streams(kn_ref[0]).astype(bf16).astype(f32)
        m_ref[...] = jnp.sum(qh.astype(f32) * kn, axis=-1, keepdims=True) + b0_ref[...]
        l_ref[...] = jnp.ones(l_ref.shape, f32)
        acc_ref[...] = both_streams(vn_ref[0]).astype(bf16).astype(f32)

    qh = qh_ref[...]
    s = jnp.concatenate([_dot_nt(qh, k_ref[...].astype(bf16)) for k_ref in k_refs], axis=1)
    s = s + bias_ref[(c == nc - 1).astype(i32)]
    m_old = m_ref[...]
    m_new = jnp.maximum(m_old, jnp.max(s, axis=-1, keepdims=True))
    alpha = jnp.exp(m_old - m_new)
    p = jnp.exp(s - m_new)
    l_ref[...] = alpha * l_ref[...] + jnp.sum(p, axis=-1, keepdims=True)
    pv = alpha * acc_ref[...]
    for i, v_ref in enumerate(v_refs):
        pv = pv + _dot(p[:, i * prow:(i + 1) * prow].astype(bf16), v_ref[...].astype(bf16))
    acc_ref[...] = pv
    m_ref[...] = m_new

    @pl.when(c == nc - 1)
    def _():
        on = acc_ref[...] / l_ref[...]
        o_ref[0] = _sub_ln(on[0:nh] - lam_ref[0] * on[nh:2 * nh], gain_ref[...], lam_init)


N_PROMPT_SCRATCH = 8


def _attn_kernel(pt_ref, lam_ref, far_ref, q_ref, k_ref, vt_ref, tile_ref, gain_col_ref,
                 qs_ref, kn_ref, vn_ref, bias_ref, b0_ref, gain_row_ref, *rest, t, cb, npg, lam_init):
    k_pages, v_pages = rest[:npg], rest[npg:2 * npg]
    o_ref, os_ref = rest[2 * npg:2 * npg + 2]
    scratch = rest[2 * npg + 2:]
    _attn_decode_chunk(pt_ref, lam_ref, qs_ref, kn_ref, vn_ref, bias_ref, b0_ref, gain_row_ref, k_pages, v_pages,
                       os_ref, *scratch[N_PROMPT_SCRATCH:], chunk_axis=2, lam_init=lam_init)
    _attn_prompt_tile(lam_ref, far_ref, q_ref, k_ref, vt_ref, tile_ref, gain_col_ref, o_ref,
                      *scratch[:N_PROMPT_SCRATCH], t=t, cb=cb, lam_init=lam_init)


def _attn(q, k, vt, tiles, far, lam, gain, batch, seq, qs, kn, vn, cache_k, cache_v, layer, page_table, bias2, b0,
          lam_init):
    t = ATT_TILE
    nq = seq // t
    hd = DIFF_HD
    nh = DIFF_HEADS
    nb = qs.shape[0]
    pool = cache_k.shape[1]
    npg = page_table.shape[1] // nq
    assert nb == batch * nh and npg * nq == page_table.shape[1], "one sample sequence per prompt (batch, head)"
    nrow = 2 * nh
    prow = PAGE_SIZE * nh
    smem = pl.BlockSpec(memory_space=pltpu.SMEM)
    page_spec = lambda j: pl.BlockSpec(
        (prow, hd), lambda b, h, i, pt: (layer * pool + pt[b * nh + h, i * npg + j], 0))
    rowspec = pl.BlockSpec((1, nh, hd), lambda b, h, i, pt: (b * nh + h, 0, 0))
    const = lambda shape: pl.BlockSpec(shape, lambda b, h, i, pt: (0,) * len(shape), pipeline_mode=pl.Buffered(1))
    grid_spec = pltpu.PrefetchScalarGridSpec(
        num_scalar_prefetch=1,
        grid=(batch, nh, nq),
        in_specs=[smem, smem,
                  pl.BlockSpec((t, hd), lambda b, h, i, pt: (b * nq + i, h)),
                  pl.BlockSpec((seq, hd), lambda b, h, i, pt: (b, h)),
                  pl.BlockSpec((hd, seq), lambda b, h, i, pt: (b * nh + h, 0)),
                  pl.BlockSpec((1, 2, t, t), lambda b, h, i, pt: (h, 0, 0, 0)),
                  const((hd, 1)),
                  rowspec, rowspec, rowspec, const((2, nrow, npg * prow)), const((nrow, 1)), const((1, hd))]
                 + [page_spec(j) for j in range(npg)] * 2,
        out_specs=[pl.BlockSpec((t, hd), lambda b, h, i, pt: (b * nq + i, h)), rowspec],
        scratch_shapes=[pltpu.VMEM((2 * t, hd), bf16), pltpu.VMEM((1, 2 * t), f32),
                        pltpu.VMEM((1, 2 * t), f32), pltpu.VMEM((hd, 2 * t), f32),
                        pltpu.VMEM((t, 2 * t), f32), pltpu.VMEM((1, 2 * t), f32),
                        pltpu.VMEM((t, 2 * t), f32), pltpu.VMEM((1, 2 * t), f32),
                        pltpu.VMEM((nrow, hd), bf16), pltpu.VMEM((nrow, 1), f32),
                        pltpu.VMEM((nrow, 1), f32), pltpu.VMEM((nrow, hd), f32)],
    )
    ck = cache_k.reshape(-1, hd)
    cv = cache_v.reshape(-1, hd)
    return pl.pallas_call(
        functools.partial(_attn_kernel, t=t, cb=ATT_QUERY_BLOCK, npg=npg, lam_init=lam_init),
        grid_spec=grid_spec,
        out_shape=[jax.ShapeDtypeStruct((batch * seq, nh * hd), bf16), jax.ShapeDtypeStruct((nb, nh, hd), f32)],
        compiler_params=pltpu.CompilerParams(dimension_semantics=("arbitrary",) * 3,
                                             vmem_limit_bytes=ATTN_VMEM_LIMIT_BYTES),
        name="attn",
    )(page_table, lam, far, q, k, vt, tiles, gain.reshape(hd, 1), qs, kn, vn, bias2, b0, gain.reshape(1, hd),
      *([ck] * npg), *([cv] * npg))


def _softmax_rows(s):
    e = jnp.exp(s - jnp.max(s, axis=-1, keepdims=True))
    return e / jnp.sum(e, axis=-1, keepdims=True)


def _cross_prompt_kernel(x_ref, g_ref, wq_ref, mk_ref, mv_ref, wo_ref, o_ref, a_ref):
    x = x_ref[...]
    d = x.shape[1]
    hd = d // MEM_HEADS
    h = _rms(x, g_ref[...]).astype(bf16)
    q = (_dot(h, wq_ref[...]) * (hd ** -0.5)).astype(bf16)
    for hh in range(MEM_HEADS):
        sl = slice(hh * hd, (hh + 1) * hd)
        a = _softmax_rows(_dot_nt(q[:, sl], mk_ref[:, sl])).astype(bf16)
        a_ref[:, sl] = _dot(a, mv_ref[:, sl]).astype(bf16)
    o_ref[...] = x + _dot(a_ref[...], wo_ref[...])


def _cross_prompt(x, g, wq, mk, mv, wo, seq):
    m, d = x.shape
    _, wq_spec, wq = _stacked(wq)
    _, wo_spec, wo = _stacked(wo)
    tm = ROW_TILE
    per = seq // tm
    mem = mk.shape[0] // (m // seq)
    return pl.pallas_call(
        _cross_prompt_kernel,
        grid=(m // tm,),
        in_specs=[pl.BlockSpec((tm, d), lambda i: (i, 0)), _resident((1, d)), wq_spec,
                  pl.BlockSpec((mem, d), lambda i: (i // per, 0)), pl.BlockSpec((mem, d), lambda i: (i // per, 0)),
                  wo_spec],
        out_specs=pl.BlockSpec((tm, d), lambda i: (i, 0)),
        out_shape=jax.ShapeDtypeStruct((m, d), f32),
        scratch_shapes=[pltpu.VMEM((tm, d), bf16)],
        compiler_params=_params("arbitrary"),
        name="cross_prompt",
    )(x, g.reshape(1, d), wq, mk, mv, wo)


def _cross_sample_kernel(q_ref, mk_ref, mv_ref, mask_ref, o_ref):
    nh, hd = q_ref.shape[1], q_ref.shape[2]
    q = jnp.concatenate([q_ref[0] * (hd ** -0.5), jnp.zeros((SUBLANES - nh, hd), f32)], axis=0)
    rows = mask_ref.shape[1]
    s = _dot_nt(q.astype(bf16), mk_ref[...].reshape(rows, hd).astype(bf16)) + mask_ref[...]
    o = _dot(_softmax_rows(s).astype(bf16), mv_ref[...].reshape(rows, hd).astype(bf16))
    o_ref[0] = o[0:nh]


def _cross_sample(q, cache_mk, cache_mv, layer):
    nb, nh, hd = q.shape
    rows = cache_mk.shape[2] * nh
    head_of_row = jnp.arange(rows, dtype=i32) % nh
    mask = jnp.where(head_of_row[None, :] == jnp.arange(SUBLANES, dtype=i32)[:, None], 0.0, NEG_INF).astype(f32)
    mask = mask.at[nh:].set(0.0)
    mem_spec = pl.BlockSpec((None, None, rows // nh, nh, hd), lambda b: (layer, b, 0, 0, 0))
    return pl.pallas_call(
        _cross_sample_kernel,
        grid=(nb,),
        in_specs=[pl.BlockSpec((1, nh, hd), lambda b: (b, 0, 0)), mem_spec, mem_spec, _resident((SUBLANES, rows))],
        out_specs=pl.BlockSpec((1, nh, hd), lambda b: (b, 0, 0)),
        out_shape=jax.ShapeDtypeStruct((nb, nh, hd), f32),
        compiler_params=_params("arbitrary"),
        name="cross_sample",
    )(q, cache_mk, cache_mv, mask)


def _rotary_tables(pos, half):
    inv = 1.0 / (10000.0 ** jnp.linspace(0.0, 1.0, half, dtype=f32))
    ang = pos.astype(f32)[:, None] * inv[None, :]
    cos, sin = jnp.cos(ang), jnp.sin(ang)
    return jnp.concatenate([cos, cos], axis=-1), jnp.concatenate([-sin, sin], axis=-1)


def kernel(x_prompt, x_sample, state_ret, state_pool, cache_k_diff, cache_v_diff, cache_mem_k, cache_mem_v, page_table, mem_prompt, norms, final_norm, ffn_w_gate, ffn_w_up, ffn_w_down, w_in_even, ret_gain, pool_w, pool_scale, w_out_even, w_qkv_odd, lambda_q1, lambda_k1, lambda_q2, lambda_k2, subln_gain, w_out_odd, rel_bias, w_cq, w_ckv, w_co):
    batch, seq, d = x_prompt.shape
    nb = x_sample.shape[0]
    depth = norms.shape[0]
    past = page_table.shape[1] * PAGE_SIZE
    mem_len = mem_prompt.shape[1]
    rw = RET_HEADS * RET_DK
    dw = DIFF_HEADS * DIFF_HD

    xp = x_prompt.reshape(batch * seq, d)
    xs = x_sample.reshape(nb, d)
    mem2d = mem_prompt.reshape(batch * mem_len, d)

    cos_p, sin_p = _rotary_tables(jnp.arange(seq, dtype=i32), RET_DK // 2)
    cos_s, sin_s = _rotary_tables(jnp.full((1,), past, dtype=i32), RET_DK // 2)

    wg, wu, wd = ffn_w_gate.astype(bf16), ffn_w_up.astype(bf16), ffn_w_down.astype(bf16)
    w_in, w_oe = w_in_even.astype(bf16), w_out_even.astype(bf16)
    w_qkv, w_oo = w_qkv_odd.astype(bf16), w_out_odd.astype(bf16)
    wcq, wckv, wco = w_cq.astype(bf16), w_ckv.astype(bf16), w_co.astype(bf16)
    pw = pool_w.astype(bf16)

    p_ret, p_pool, p_k, p_v, p_mk, p_mv = [], [], [], [], [], []
    s_ret, s_pool, s_k, s_v = [], [], [], []
    for l in range(depth):
        last = l == depth - 1
        ffn_w = lambda i: ((wg, (l, i)), (wu, (l, i)), (wd, (l, i)))
        xp = _ffn(xp, norms[l, 0], *ffn_w(0))
        xs = _ffn(xs, norms[l, 0], *ffn_w(0))
        if l % 2 == 0:
            e = l // 2
            zw = w_in.shape[2]
            (zp,) = _proj(xp, norms[l, 1], (w_in, (e,)), [(0, zw, [(0, 1.0)])], [(zw, f32)])
            (zs,) = _proj(xs, norms[l, 1], (w_in, (e,)), [(0, zw, [(0, 1.0)])], [(zw, f32)])
            mixp, rp, pp = _even_prompt(zp, cos_p, sin_p, ret_gain[e], pw[e], pool_scale[e], batch, seq)
            mixs, rs, ps = _even_sample(zs, cos_s, sin_s, state_ret[e:e + 1], state_pool[e:e + 1], ret_gain[e],
                                        pool_w[e], pool_scale[e], past)
            xp = _mmres(xp, mixp, (w_oe, (e,)))
            xs = _mmres(xs, mixs.reshape(nb, rw + pool_scale.shape[1]), (w_oe, (e,)))
            p_ret.append(rp[0])
            p_pool.append(pp[0])
            s_ret.append(rs[0])
            s_pool.append(ps[0])
        else:
            o = l // 2
            lam_init = 0.8 - 0.6 * math.exp(-0.3 * l)
            scale = DIFF_D ** -0.5
            tiles, tab, lam_t = _bias_tables(rel_bias, lambda_q1[o], lambda_k1[o], lambda_q2[o], lambda_k2[o],
                                             lam_init, ATT_TILE)
            lam = lam_t[0, 0:1]
            tab = tab[:, 0, :]
            far = tab[:, REL_MAX_DIST]
            qkv_groups = [(0, dw, [(0, scale * LOG2E)]), (dw, dw, [(1, 1.0), (3, 1.0)]),
                          (2 * dw, dw, [(2, 1.0), (4, 1.0)])]
            qb, kp, vp, kb, vtb = _proj(xp, norms[l, 1], (w_qkv, (o,)), qkv_groups,
                                        [(dw, bf16), (dw, f32), (dw, f32), (dw, bf16), (dw, bf16, "T")], seq=seq)
            qs, ks_, vs = _proj(xs, norms[l, 1], (w_qkv, (o,)),
                                [(0, dw, [(0, scale)]), (dw, dw, [(1, 1.0)]), (2 * dw, dw, [(2, 1.0)])],
                                [(dw, f32), (dw, f32), (dw, f32)])
            nh = DIFF_HEADS
            pages_per_step = page_table.shape[1] // (seq // ATT_TILE)
            own = jnp.arange(nh)[:, None, None] == jnp.arange(nh)[None, None, :]
            far_page = jnp.where(own, far[:, None, None], NEG_INF) + jnp.zeros((1, PAGE_SIZE, 1), f32)
            last_page = jnp.where(own, tab[:, PAGE_SIZE:0:-1][:, :, None], NEG_INF)
            far_page, last_page = far_page.reshape(nh, -1), last_page.reshape(nh, -1)
            far_step = jnp.tile(far_page, (2, pages_per_step))
            last_step = jnp.concatenate([far_step[:, :-PAGE_SIZE * nh], jnp.tile(last_page, (2, 1))], axis=1)
            bias2 = jnp.stack([far_step, last_step])
            b0 = jnp.tile(tab[:, 0], 2)[:, None]
            heads = lambda a: a.reshape(nb, nh, DIFF_HD)
            attp, atts = _attn(qb, kb, vtb, tiles, far, lam, subln_gain[o], batch, seq,
                               heads(qs), heads(ks_), heads(vs), cache_k_diff, cache_v_diff, o, page_table,
                               bias2, b0, lam_init)
            xp = _mmres(xp, attp, (w_oo, (o,)))
            xs = _mmres(xs, atts.reshape(nb, dw), (w_oo, (o,)))
            p_k.append(kp.reshape(batch, seq, DIFF_HEADS, DIFF_HD))
            p_v.append(vp.reshape(batch, seq, DIFF_HEADS, DIFF_HD))
            s_k.append(ks_.reshape(nb, 1, DIFF_HEADS, DIFF_HD))
            s_v.append(vs.reshape(nb, 1, DIFF_HEADS, DIFF_HD))
        mk, mv, mkb, mvb = _proj(mem2d, None, (wckv, (l,)),
                                 [(0, d, [(0, 1.0), (2, 1.0)]), (d, d, [(1, 1.0), (3, 1.0)])],
                                 [(d, f32), (d, f32), (d, bf16), (d, bf16)])
        p_mk.append(mk.reshape(batch, mem_len, MEM_HEADS, d // MEM_HEADS))
        p_mv.append(mv.reshape(batch, mem_len, MEM_HEADS, d // MEM_HEADS))
        xp = _cross_prompt(xp, norms[l, 2], (wcq, (l,)), mkb, mvb, (wco, (l,)), seq)
        (qcs,) = _proj(xs, norms[l, 2], (wcq, (l,)), [(0, d, [(0, 1.0)])], [(d, f32)])
        acs = _cross_sample(qcs.reshape(nb, MEM_HEADS, d // MEM_HEADS), cache_mem_k, cache_mem_v, l)
        xs = _mmres(xs, acs.reshape(nb, d), (wco, (l,)))
        fin = final_norm if last else None
        xp = _ffn(xp, norms[l, 3], *ffn_w(1), fin)
        xs = _ffn(xs, norms[l, 3], *ffn_w(1), fin)

    return (xp.reshape(batch, seq, d), xs.reshape(nb, 1, d),
            jnp.stack(p_ret), jnp.stack(p_pool), jnp.stack(p_k), jnp.stack(p_v), jnp.stack(p_mk), jnp.stack(p_mv),
            jnp.stack(s_ret), jnp.stack(s_pool), jnp.stack(s_k), jnp.stack(s_v))
```

```python
import functools
import math

import jax
import jax.numpy as jnp
from jax import lax
from jax.experimental import pallas as pl
from jax.experimental.pallas import tpu as pltpu

f32 = jnp.float32
bf16 = jnp.bfloat16
i32 = jnp.int32

EPS = 1e-6
NEG_INF = -1e30
LOG2E = math.log2(math.e)

RET_HEADS = 4
RET_DK = 128
POOL_WINDOWS = (2, 4, 8, 16)
POOL_KEEP = max(POOL_WINDOWS) - 1
DIFF_HEADS = 8
DIFF_D = 64
DIFF_HD = 2 * DIFF_D
REL_BUCKETS = 32
REL_MAX_DIST = 128
MEM_HEADS = 4
PAGE_SIZE = 128

VMEM_LIMIT_BYTES = 56 * 1024 * 1024
LANES = 128
SUBLANES = 8
MXU_TILE = 256

ROW_TILE = 512
ATT_TILE = 512
ATT_QUERY_BLOCK = 512
ATTN_VMEM_LIMIT_BYTES = 60 * 1024 * 1024


def _params(*sem):
    return pltpu.CompilerParams(dimension_semantics=sem, vmem_limit_bytes=VMEM_LIMIT_BYTES)


def _rms(x, g):
    return x * lax.rsqrt(jnp.mean(x * x, axis=-1, keepdims=True) + EPS) * g


def _dot(a, b):
    return jnp.dot(a, b, preferred_element_type=f32)


def _dot_nt(a, b):
    return lax.dot_general(a, b, (((1,), (1,)), ((), ())), preferred_element_type=f32)


def _dot_tn(a, b):
    return lax.dot_general(a, b, (((0,), (0,)), ((), ())), preferred_element_type=f32)


def _resident(shape, lead=()):
    nd = len(shape)
    return pl.BlockSpec((None,) * len(lead) + tuple(shape), lambda *_: tuple(lead) + (0,) * nd,
                        pipeline_mode=pl.Buffered(1))


def _stacked(w):
    arr, lead = w
    shape = arr.shape[len(lead):]
    return shape, _resident(shape, lead), arr


def _ffn_kernel(x_ref, g_ref, wg_ref, wu_ref, wd_ref, *rest, bounds, final):
    o_ref = rest[-1]
    x = x_ref[...]
    h = _rms(x, g_ref[...]).astype(bf16)
    acc = jnp.zeros(x.shape, f32)
    for c0, c1 in zip(bounds[:-1], bounds[1:]):
        gate = _dot(h, wg_ref[:, c0:c1])
        up = _dot(h, wu_ref[:, c0:c1])
        a = (jax.nn.silu(gate) * up).astype(bf16)
        acc = acc + _dot(a, wd_ref[c0:c1, :])
    y = x + 0.5 * acc
    if final:
        y = _rms(y, rest[0][...])
    o_ref[...] = y


def _ffn(x, g, wg, wu, wd, final_g=None):
    m, d = x.shape
    (_, f), wg_spec, wg = _stacked(wg)
    _, wu_spec, wu = _stacked(wu)
    _, wd_spec, wd = _stacked(wd)
    tm = min(ROW_TILE, m)
    split = -(-f // (2 * MXU_TILE)) * MXU_TILE
    bounds = (0, split, f) if split < f else (0, f)
    final = final_g is not None
    in_specs = [pl.BlockSpec((tm, d), lambda i: (i, 0)), _resident((1, d)), wg_spec, wu_spec, wd_spec]
    args = [x, g.reshape(1, d), wg, wu, wd]
    if final:
        in_specs.append(_resident((1, d)))
        args.append(final_g.reshape(1, d))
    return pl.pallas_call(
        functools.partial(_ffn_kernel, bounds=bounds, final=final),
        grid=(m // tm,),
        in_specs=in_specs,
        out_specs=pl.BlockSpec((tm, d), lambda i: (i, 0)),
        out_shape=jax.ShapeDtypeStruct((m, d), f32),
        compiler_params=_params("arbitrary"),
        name="ffn",
    )(*args)


def _proj_kernel(*refs, norm, groups, transposed, nchunk):
    x_ref = refs[0]
    g_ref = refs[1] if norm else None
    w_ref = refs[2] if norm else refs[1]
    o_refs = refs[(3 if norm else 2):]
    x = x_ref[...]
    h = (_rms(x, g_ref[...]) if norm else x).astype(bf16)
    for off, width, targets in groups:
        for c0 in range(0, width, nchunk):
            n = min(nchunk, width - c0)
            r = _dot(h, w_ref[:, off + c0:off + c0 + n])
            for oi, scale in targets:
                o_ref = o_refs[oi]
                val = (r if scale == 1.0 else r * scale)
                if transposed[oi]:
                    o_ref[c0:c0 + n, :] = val.T.astype(o_ref.dtype)
                else:
                    o_ref[:, c0:c0 + n] = val.astype(o_ref.dtype)


def _proj(x, g, w, groups, outs, seq=None):
    m, d = x.shape
    _, w_spec, w = _stacked(w)
    tm = min(ROW_TILE, m)
    norm = g is not None
    in_specs = [pl.BlockSpec((tm, d), lambda i: (i, 0))]
    args = [x]
    if norm:
        in_specs.append(_resident((1, d)))
        args.append(g.reshape(1, d))
    in_specs.append(w_spec)
    args.append(w)
    out_specs, out_shape = [], []
    for wd, dt, *flag in outs:
        if flag:
            per = seq // tm
            out_specs.append(pl.BlockSpec((wd, tm), lambda i: (i // per, i % per)))
            out_shape.append(jax.ShapeDtypeStruct((m // seq * wd, seq), dt))
        else:
            out_specs.append(pl.BlockSpec((tm, wd), lambda i: (i, 0)))
            out_shape.append(jax.ShapeDtypeStruct((m, wd), dt))
    return pl.pallas_call(
        functools.partial(_proj_kernel, norm=norm, groups=groups, transposed=tuple(o[2:] == ("T",) for o in outs),
                          nchunk=512),
        grid=(m // tm,),
        in_specs=in_specs,
        out_specs=out_specs,
        out_shape=out_shape,
        compiler_params=_params("arbitrary"),
        name="proj",
    )(*args)


def _mmres_kernel(x_ref, a_ref, w_ref, o_ref):
    o_ref[...] = x_ref[...] + _dot(a_ref[...].astype(bf16), w_ref[...])


def _mmres(x, a, w):
    m, d = x.shape
    k = a.shape[1]
    _, w_spec, w = _stacked(w)
    tm = min(ROW_TILE, m)
    return pl.pallas_call(
        _mmres_kernel,
        grid=(m // tm,),
        in_specs=[pl.BlockSpec((tm, d), lambda i: (i, 0)), pl.BlockSpec((tm, k), lambda i: (i, 0)), w_spec],
        out_specs=pl.BlockSpec((tm, d), lambda i: (i, 0)),
        out_shape=jax.ShapeDtypeStruct((m, d), f32),
        compiler_params=_params("arbitrary"),
        name="mmres",
    )(x, a, w)


def _ret_log_gamma(h):
    return math.log1p(-(2.0 ** (-5.0 - h)))


def _rotate(x, cosf, sinf):
    return x * cosf + pltpu.roll(x, x.shape[-1] // 2, axis=1) * sinf


def _group_norm_gate(o, gain, gate):
    mu = jnp.mean(o, axis=-1, keepdims=True)
    var = jnp.mean(jnp.square(o - mu), axis=-1, keepdims=True)
    return jax.nn.silu(gate) * ((o - mu) * lax.rsqrt(var + EPS) * gain)


def _even_prompt_kernel(x_ref, zq_ref, zk_ref, zv_ref, zg_ref, zp_ref, cos_ref, sin_ref, gain_ref, pw_ref, ps_ref,
                        wo_ref, xo_ref, sret_ref, spool_ref, state_ref, ext_ref, intra_ref, mix_ref, *, t):
    b = pl.program_id(0)
    c = pl.program_id(1)
    nc = pl.num_programs(1)
    halo = POOL_KEEP + 1
    dk = RET_DK

    @pl.when(jnp.logical_and(b == 0, c == 0))
    def _():
        rel = lax.broadcasted_iota(i32, (t, t), 0) - lax.broadcasted_iota(i32, (t, t), 1)
        relf = jnp.maximum(rel, 0).astype(f32)
        for h in range(RET_HEADS):
            intra_ref[h] = jnp.where(rel >= 0, jnp.exp(_ret_log_gamma(h) * relf), 0.0)

    @pl.when(c == 0)
    def _():
        state_ref[...] = jnp.zeros(state_ref.shape, f32)
        ext_ref[0:halo, :] = jnp.zeros((halo, ext_ref.shape[1]), f32)

    cosf = cos_ref[...]
    sinf = sin_ref[...]
    row = lax.broadcasted_iota(i32, (t, 1), 0).astype(f32)
    for h in range(RET_HEADS):
        sl = slice(h * dk, (h + 1) * dk)
        lg = _ret_log_gamma(h)
        qr = _rotate(zq_ref[:, sl], cosf, sinf)
        kr = _rotate(zk_ref[:, sl], cosf, sinf) * (dk ** -0.5)
        qb = qr.astype(bf16)
        vb = zv_ref[:, sl].astype(bf16)
        att = _dot_nt(qb, kr.astype(bf16)) * intra_ref[h]
        s_old = state_ref[h]
        o = _dot(att.astype(bf16), vb) + _dot(qb, s_old.astype(bf16)) * jnp.exp(lg * (row + 1.0))
        k_dec = jnp.exp(lg * (float(t - 1) - row))
        state_ref[h] = s_old * math.exp(lg * t) + _dot_tn((kr * k_dec).astype(bf16), vb)
        mix_ref[:, sl] = _group_norm_gate(o, gain_ref[:, sl], zg_ref[:, sl]).astype(mix_ref.dtype)

    p = zp_ref[...]
    ext_ref[halo:halo + t, :] = p
    pos = c * t + lax.broadcasted_iota(i32, (t, 1), 0)
    gw = p.shape[1] // len(POOL_WINDOWS)
    ret_w = RET_HEADS * dk
    for gi, w in enumerate(POOL_WINDOWS):
        sl = slice(gi * gw, (gi + 1) * gw)
        win = p[:, sl]
        for back in range(1, w):
            win = win + ext_ref[halo - back:halo - back + t, sl]
        cnt = jnp.minimum(w, pos + 1).astype(f32)
        pooled = win / cnt - p[:, sl]
        mixed = _dot(pooled.astype(bf16), pw_ref[gi]) * ps_ref[:, sl]
        mix_ref[:, ret_w + gi * gw:ret_w + (gi + 1) * gw] = mixed.astype(mix_ref.dtype)

    xo_ref[...] = x_ref[...] + _dot(mix_ref[...], wo_ref[...])

    @pl.when(c == nc - 1)
    def _():
        sret_ref[0, 0] = state_ref[...]
        spool_ref[0, 0] = ext_ref[t + 1:t + halo, :]

    ext_ref[0:halo, :] = ext_ref[t:t + halo, :]


def _even_prompt(x, z, cosf, sinf, gain, pw, ps, wo, batch, seq):
    t = ROW_TILE
    nc = seq // t
    d = x.shape[1]
    rw = RET_HEADS * RET_DK
    pwid = ps.shape[0]
    _, wo_spec, wo = _stacked(wo)
    row = lambda b, c: (b * nc + c, 0)
    zspec = lambda j: pl.BlockSpec((t, rw), lambda b, c: (b * nc + c, j))
    return pl.pallas_call(
        functools.partial(_even_prompt_kernel, t=t),
        grid=(batch, nc),
        in_specs=[pl.BlockSpec((t, d), row), zspec(0), zspec(1), zspec(2), zspec(3), zspec(4),
                  pl.BlockSpec((t, RET_DK), lambda b, c: (c, 0)), pl.BlockSpec((t, RET_DK), lambda b, c: (c, 0)),
                  _resident((1, rw)), _resident(pw.shape), _resident((1, pwid)), wo_spec],
        out_specs=[pl.BlockSpec((t, d), row),
                   pl.BlockSpec((1, 1, RET_HEADS, RET_DK, RET_DK), lambda b, c: (0, b, 0, 0, 0)),
                   pl.BlockSpec((1, 1, POOL_KEEP, pwid), lambda b, c: (0, b, 0, 0))],
        out_shape=[jax.ShapeDtypeStruct((batch * seq, d), f32),
                   jax.ShapeDtypeStruct((1, batch, RET_HEADS, RET_DK, RET_DK), f32),
                   jax.ShapeDtypeStruct((1, batch, POOL_KEEP, pwid), f32)],
        scratch_shapes=[pltpu.VMEM((RET_HEADS, RET_DK, RET_DK), f32),
                        pltpu.VMEM((POOL_KEEP + 1 + t, pwid), f32),
                        pltpu.VMEM((RET_HEADS, t, t), f32),
                        pltpu.VMEM((t, rw + pwid), bf16)],
        compiler_params=_params("arbitrary", "arbitrary"),
        name="even_prompt",
    )(x, z, z, z, z, z, cosf, sinf, gain.reshape(1, rw), pw, ps.reshape(1, pwid), wo)


def _even_sample_kernel(z_ref, cos_ref, sin_ref, s0_ref, rows_ref, gain_ref, pw_ref, ps_ref,
                        mix_ref, sret_ref, spool_ref, ext_ref, *, pos):
    dk = RET_DK
    rw = RET_HEADS * dk
    z = jnp.broadcast_to(z_ref[0], (SUBLANES, z_ref.shape[2]))
    cosf = jnp.broadcast_to(cos_ref[...], (SUBLANES, dk))
    sinf = jnp.broadcast_to(sin_ref[...], (SUBLANES, dk))
    first = lax.broadcasted_iota(i32, (SUBLANES, dk), 0) == 0
    for h in range(RET_HEADS):
        sl = slice(h * dk, (h + 1) * dk)
        gamma = math.exp(_ret_log_gamma(h))
        qr = _rotate(z[:, sl], cosf, sinf)
        kr = _rotate(z[:, rw + h * dk:rw + (h + 1) * dk], cosf, sinf) * (dk ** -0.5)
        v = z[:, 2 * rw + h * dk:2 * rw + (h + 1) * dk]
        s_new = s0_ref[0, 0, h] * gamma + _dot_tn(jnp.where(first, kr, 0.0), v)
        sret_ref[0, 0, h] = s_new
        o = _dot(qr, s_new)
        gate = z[:, 3 * rw + h * dk:3 * rw + (h + 1) * dk]
        mix_ref[0, :, sl] = _group_norm_gate(o, gain_ref[:, sl], gate)[0:1]

    keep = POOL_KEEP
    pwid = ps_ref.shape[1]
    p = z[0:1, 4 * rw:4 * rw + pwid]
    ext_ref[0:keep, :] = rows_ref[0, 0]
    ext_ref[keep:keep + 1, :] = p
    spool_ref[0, 0] = ext_ref[1:keep + 1, :]
    gw = pwid // len(POOL_WINDOWS)
    for gi, w in enumerate(POOL_WINDOWS):
        sl = slice(gi * gw, (gi + 1) * gw)
        win = jnp.sum(ext_ref[keep + 1 - w:keep + 1, sl], axis=0, keepdims=True)
        pooled = win / float(min(w, pos + 1)) - p[:, sl]
        mixed = _dot(jnp.broadcast_to(pooled, (SUBLANES, gw)), pw_ref[gi])[0:1] * ps_ref[:, sl]
        mix_ref[0, :, rw + gi * gw:rw + (gi + 1) * gw] = mixed


def _even_sample(z, cosf, sinf, state_ret, state_pool, gain, pw, ps, pos):
    nb, zw = z.shape
    rw = RET_HEADS * RET_DK
    pwid = ps.shape[0]
    return pl.pallas_call(
        functools.partial(_even_sample_kernel, pos=pos),
        grid=(nb,),
        in_specs=[pl.BlockSpec((1, 1, zw), lambda b: (b, 0, 0)),
                  _resident((1, RET_DK)), _resident((1, RET_DK)),
                  pl.BlockSpec((1, 1, RET_HEADS, RET_DK, RET_DK), lambda b: (0, b, 0, 0, 0)),
                  pl.BlockSpec((1, 1, POOL_KEEP, pwid), lambda b: (0, b, 0, 0)),
                  _resident((1, rw)), _resident(pw.shape), _resident((1, pwid))],
        out_specs=[pl.BlockSpec((1, 1, rw + pwid), lambda b: (b, 0, 0)),
                   pl.BlockSpec((1, 1, RET_HEADS, RET_DK, RET_DK), lambda b: (0, b, 0, 0, 0)),
                   pl.BlockSpec((1, 1, POOL_KEEP, pwid), lambda b: (0, b, 0, 0))],
        out_shape=[jax.ShapeDtypeStruct((nb, 1, rw + pwid), f32),
                   jax.ShapeDtypeStruct((1, nb, RET_HEADS, RET_DK, RET_DK), f32),
                   jax.ShapeDtypeStruct((1, nb, POOL_KEEP, pwid), f32)],
        scratch_shapes=[pltpu.VMEM((POOL_KEEP + 1, pwid), f32)],
        compiler_params=_params("arbitrary"),
        name="even_sample",
    )(z.reshape(nb, 1, zw), cosf, sinf, state_ret, state_pool, gain.reshape(1, rw), pw, ps.reshape(1, pwid))


def _rel_bias_of(dist, rb_ref, h):
    n = jnp.maximum(dist, 0)
    max_exact = REL_BUCKETS // 2
    nf = jnp.maximum(n, 1).astype(f32)
    large = max_exact + (jnp.log(nf / max_exact) / math.log(REL_MAX_DIST / max_exact)
                         * (REL_BUCKETS - max_exact)).astype(i32)
    large = jnp.minimum(large, REL_BUCKETS - 1)
    bucket = jnp.where(n < max_exact, n, large)
    out = jnp.zeros(dist.shape, f32)
    for bk in range(REL_BUCKETS):
        out = jnp.where(bucket == bk, rb_ref[bk, h], out)
    return out


def _bias_kernel(rb_ref, lq1_ref, lk1_ref, lq2_ref, lk2_ref, tile_ref, tab_ref, lam_ref, *, t, lam_init):
    h = pl.program_id(0)
    nd = REL_MAX_DIST
    rel = lax.broadcasted_iota(i32, (nd, nd), 1) - lax.broadcasted_iota(i32, (nd, nd), 0)
    on_diag = jnp.where(rel >= 0, _rel_bias_of(rel, rb_ref, h) * LOG2E, NEG_INF)
    next_diag = _rel_bias_of(rel + nd, rb_ref, h) * LOG2E
    far = _rel_bias_of(jnp.full((nd, nd), nd, i32), rb_ref, h) * LOG2E
    masked = jnp.full((nd, nd), NEG_INF, f32)
    nblk = t // nd
    for bk in range(nblk):
        for bq in range(nblk):
            diag_blk = on_diag if bq == bk else next_diag if bq == bk + 1 else far if bq > bk else masked
            below_blk = next_diag if (bk == nblk - 1 and bq == 0) else far
            tile_ref[0, 0, bk * nd:(bk + 1) * nd, bq * nd:(bq + 1) * nd] = diag_blk
            tile_ref[0, 1, bk * nd:(bk + 1) * nd, bq * nd:(bq + 1) * nd] = below_blk
    tab_ref[0] = _rel_bias_of(lax.broadcasted_iota(i32, (1, 2 * REL_MAX_DIST), 1), rb_ref, h)
    lam = (jnp.exp(jnp.sum(lq1_ref[...] * lk1_ref[...], axis=-1, keepdims=True))
           - jnp.exp(jnp.sum(lq2_ref[...] * lk2_ref[...], axis=-1, keepdims=True)) + lam_init)
    lam_ref[...] = jnp.broadcast_to(lam, lam_ref.shape)


def _bias_tables(rel_bias, lq1, lk1, lq2, lk2, lam_init, t):
    nh = rel_bias.shape[1]
    d = lq1.shape[0]
    vec = lambda a: a.reshape(1, d)
    return pl.pallas_call(
        functools.partial(_bias_kernel, t=t, lam_init=lam_init),
        grid=(nh,),
        in_specs=[pl.BlockSpec(memory_space=pltpu.SMEM)] + [_resident((1, d))] * 4,
        out_specs=[pl.BlockSpec((1, 2, t, t), lambda h: (h, 0, 0, 0)),
                   pl.BlockSpec((1, 1, 2 * REL_MAX_DIST), lambda h: (h, 0, 0)),
                   pl.BlockSpec((SUBLANES, LANES), lambda h: (0, 0))],
        out_shape=[jax.ShapeDtypeStruct((nh, 2, t, t), f32),
                   jax.ShapeDtypeStruct((nh, 1, 2 * REL_MAX_DIST), f32),
                   jax.ShapeDtypeStruct((SUBLANES, LANES), f32)],
        compiler_params=_params("arbitrary"),
        name="bias_tables",
    )(rel_bias, vec(lq1), vec(lk1), vec(lq2), vec(lk2))


def _sub_ln(o, gain, lam_init):
    return o * lax.rsqrt(jnp.mean(o * o, axis=-1, keepdims=True) + EPS) * gain * (1.0 - lam_init)


def _attn_prompt_tile(lam_ref, far_ref, q_ref, k_ref, vt_ref, tile_ref, gain_ref, o_ref,
                      qq_ref, m_ref, l_ref, acc_ref, s0_ref, smax0_ref, s1_ref, smax1_ref, *, t, cb, lam_init):
    h = pl.program_id(1)
    qi = pl.program_id(2)
    q = q_ref[...]
    lane = lax.broadcasted_iota(i32, q.shape, 1)
    zero = jnp.zeros_like(q)
    qq_ref[0:t, :] = jnp.where(lane < DIFF_D, q, zero)
    qq_ref[t:2 * t, :] = jnp.where(lane >= DIFF_D, q, zero)
    m_ref[...] = jnp.full(m_ref.shape, NEG_INF, f32)
    l_ref[...] = jnp.zeros(l_ref.shape, f32)
    acc_ref[...] = jnp.zeros(acc_ref.shape, f32)

    far = far_ref[h] * LOG2E

    blocks = [slice(c0, c0 + cb) for c0 in range(0, 2 * t, cb)]

    parked = ((s0_ref, smax0_ref), (s1_ref, smax1_ref))

    def scores(ki, slot):
        s_ref, smax_ref = parked[slot]
        kt = k_ref[pl.ds(pl.multiple_of(ki * t, t), t), :]
        for cols in blocks:
            s = _dot_nt(kt, qq_ref[cols, :])
            s_ref[:, cols] = s
            smax_ref[:, cols] = jnp.max(s, axis=0, keepdims=True)

    def scores_own(ki, slot):
        s_ref, _ = parked[slot]
        k0 = pl.multiple_of(ki * t, t)
        for c0 in (0, t):
            for q0, nk in ((0, t // 2), (t // 2, t)):
                cols = slice(c0 + q0, c0 + q0 + t // 2)
                s_ref[0:nk, cols] = _dot_nt(k_ref[pl.ds(k0, nk), :], qq_ref[cols, :])

    def update(cols, m_old, m_new, p, vt):
        alpha = jnp.exp2(m_old - m_new)
        l_ref[:, cols] = alpha * l_ref[:, cols] + jnp.sum(p, axis=0, keepdims=True)
        acc_ref[:, cols] = alpha * acc_ref[:, cols] + _dot(vt, p.astype(bf16))
        m_ref[:, cols] = m_new

    def absorb_far(ki, slot):
        s_ref, smax_ref = parked[slot]
        vt = vt_ref[:, pl.ds(pl.multiple_of(ki * t, t), t)]
        for cols in blocks:
            m_old = m_ref[:, cols]
            m_new = jnp.maximum(m_old, smax_ref[:, cols] + far)
            update(cols, m_old, m_new, jnp.exp2(s_ref[:, cols] - (m_new - far)), vt)

    def absorb_behind(ki, slot):
        s_ref, smax_ref = parked[slot]
        vt = vt_ref[:, pl.ds(pl.multiple_of(ki * t, t), t)]
        corner = tile_ref[0, 1, t - nd:t, 0:nd]
        for c0 in (0, t):
            cols, near, rest = slice(c0, c0 + t), slice(c0, c0 + nd), slice(c0 + nd, c0 + t)
            m_old = m_ref[:, cols]
            s_near = jnp.concatenate([s_ref[0:t - nd, near] + far, s_ref[t - nd:t, near] + corner], axis=0)
            m_near = jnp.maximum(m_old[:, 0:nd], jnp.max(s_near, axis=0, keepdims=True))
            m_rest = jnp.maximum(m_old[:, nd:t], smax_ref[:, rest] + far)
            p = jnp.concatenate([jnp.exp2(s_near - m_near), jnp.exp2(s_ref[:, rest] - (m_rest - far))], axis=1)
            update(cols, m_old, jnp.concatenate([m_near, m_rest], axis=1), p, vt)

    def absorb_own(ki, slot):
        s_ref, _ = parked[slot]
        k0 = pl.multiple_of(ki * t, t)
        for c0 in (0, t):
            for q0, nk in ((0, t // 2), (t // 2, t)):
                cols = slice(c0 + q0, c0 + q0 + t // 2)
                m_old = m_ref[:, cols]
                s = s_ref[0:nk, cols] + tile_ref[0, 0, 0:nk, q0:q0 + t // 2]
                m_new = jnp.maximum(m_old, jnp.max(s, axis=0, keepdims=True))
                update(cols, m_old, m_new, jnp.exp2(s - m_new), vt_ref[:, pl.ds(k0, nk)])

    nd = REL_MAX_DIST
    n_far = jnp.maximum(qi - 1, 0)
    scores(0, 0)

    def far_pair(j, carry):
        for slot in (0, 1):
            scores(2 * j + slot + 1, 1 - slot)
            absorb_far(2 * j + slot, slot)
        return carry

    lax.fori_loop(0, n_far // 2, far_pair, 0)

    @pl.when(n_far % 2 == 1)
    def _():
        scores(n_far, 1)
        absorb_far(n_far - 1, 0)

    for slot in (0, 1):
        @pl.when(jnp.logical_and(qi >= 1, n_far % 2 == slot))
        def _():
            scores_own(qi, 1 - slot)
            absorb_behind(qi - 1, slot)

    for slot in (0, 1):
        @pl.when(qi % 2 == slot)
        def _():
            absorb_own(qi, slot)

    o = acc_ref[...] / l_ref[...]
    o = o[:, 0:t] - lam_ref[0] * o[:, t:2 * t]
    y = o * lax.rsqrt(jnp.mean(o * o, axis=0, keepdims=True) + EPS) * gain_ref[...] * (1.0 - lam_init)
    o_ref[...] = y.T.astype(o_ref.dtype)


def _attn_decode_chunk(pt_ref, lam_ref, q_ref, kn_ref, vn_ref, bias_ref, b0_ref, gain_ref, k_refs, v_refs, o_ref,
                       qh_ref, m_ref, l_ref, acc_ref, *, chunk_axis, lam_init):
    del pt_ref
    c = pl.program_id(chunk_axis)
    nc = pl.num_programs(chunk_axis)
    nh = DIFF_HEADS
    prow = PAGE_SIZE * nh

    def both_streams(x):
        return jnp.concatenate([x, x], axis=0)

    @pl.when(c == 0)
    def _():
        q = q_ref[0]
        lane = lax.broadcasted_iota(i32, q.shape, 1)
        qh = jnp.concatenate([jnp.where(lane < DIFF_D, q, 0.0), jnp.where(lane >= DIFF_D, q, 0.0)],
                             axis=0).astype(bf16)
        qh_ref[...] = qh
        kn = both_streams(kn_ref[0]).astype(bf16).astype(f32)
        m_ref[...] = jnp.sum(qh.astype(f32) * kn, axis=-1, keepdims=True) + b0_ref[...]
        l_ref[...] = jnp.ones(l_ref.shape, f32)
        acc_ref[...] = both_streams(vn_ref[0]).astype(bf16).astype(f32)

    qh = qh_ref[...]
    s = jnp.concatenate([_dot_nt(qh, k_ref[...].astype(bf16)) for k_ref in k_refs], axis=1)
    s = s + bias_ref[(c == nc - 1).astype(i32)]
    m_old = m_ref[...]
    m_new = jnp.maximum(m_old, jnp.max(s, axis=-1, keepdims=True))
    alpha = jnp.exp(m_old - m_new)
    p = jnp.exp(s - m_new)
    l_ref[...] = alpha * l_ref[...] + jnp.sum(p, axis=-1, keepdims=True)
    pv = alpha * acc_ref[...]
    for i, v_ref in enumerate(v_refs):
        pv = pv + _dot(p[:, i * prow:(i + 1) * prow].astype(bf16), v_ref[...].astype(bf16))
    acc_ref[...] = pv
    m_ref[...] = m_new

    @pl.when(c == nc - 1)
    def _():
        on = acc_ref[...] / l_ref[...]
        o_ref[0] = _sub_ln(on[0:nh] - lam_ref[0] * on[nh:2 * nh], gain_ref[...], lam_init)


N_PROMPT_SCRATCH = 8


def _attn_kernel(pt_ref, lam_ref, far_ref, q_ref, k_ref, vt_ref, tile_ref, gain_col_ref,
                 qs_ref, kn_ref, vn_ref, bias_ref, b0_ref, gain_row_ref, *rest, t, cb, npg, lam_init):
    k_pages, v_pages = rest[:npg], rest[npg:2 * npg]
    o_ref, os_ref = rest[2 * npg:2 * npg + 2]
    scratch = rest[2 * npg + 2:]
    _attn_decode_chunk(pt_ref, lam_ref, qs_ref, kn_ref, vn_ref, bias_ref, b0_ref, gain_row_ref, k_pages, v_pages,
                       os_ref, *scratch[N_PROMPT_SCRATCH:], chunk_axis=2, lam_init=lam_init)
    _attn_prompt_tile(lam_ref, far_ref, q_ref, k_ref, vt_ref, tile_ref, gain_col_ref, o_ref,
                      *scratch[:N_PROMPT_SCRATCH], t=t, cb=cb, lam_init=lam_init)


def _attn(q, k, vt, tiles, far, lam, gain, batch, seq, qs, kn, vn, cache_k, cache_v, layer, page_table, bias2, b0,
          lam_init):
    t = ATT_TILE
    nq = seq // t
    hd = DIFF_HD
    nh = DIFF_HEADS
    nb = qs.shape[0]
    pool = cache_k.shape[1]
    npg = page_table.shape[1] // nq
    assert nb == batch * nh and npg * nq == page_table.shape[1], "one sample sequence per prompt (batch, head)"
    nrow = 2 * nh
    prow = PAGE_SIZE * nh
    smem = pl.BlockSpec(memory_space=pltpu.SMEM)
    page_spec = lambda j: pl.BlockSpec(
        (prow, hd), lambda b, h, i, pt: (layer * pool + pt[b * nh + h, i * npg + j], 0))
    rowspec = pl.BlockSpec((1, nh, hd), lambda b, h, i, pt: (b * nh + h, 0, 0))
    const = lambda shape: pl.BlockSpec(shape, lambda b, h, i, pt: (0,) * len(shape), pipeline_mode=pl.Buffered(1))
    grid_spec = pltpu.PrefetchScalarGridSpec(
        num_scalar_prefetch=1,
        grid=(batch, nh, nq),
        in_specs=[smem, smem,
                  pl.BlockSpec((t, hd), lambda b, h, i, pt: (b * nq + i, h)),
                  pl.BlockSpec((seq, hd), lambda b, h, i, pt: (b, h)),
                  pl.BlockSpec((hd, seq), lambda b, h, i, pt: (b * nh + h, 0)),
                  pl.BlockSpec((1, 2, t, t), lambda b, h, i, pt: (h, 0, 0, 0)),
                  const((hd, 1)),
                  rowspec, rowspec, rowspec, const((2, nrow, npg * prow)), const((nrow, 1)), const((1, hd))]
                 + [page_spec(j) for j in range(npg)] * 2,
        out_specs=[pl.BlockSpec((t, hd), lambda b, h, i, pt: (b * nq + i, h)), rowspec],
        scratch_shapes=[pltpu.VMEM((2 * t, hd), bf16), pltpu.VMEM((1, 2 * t), f32),
                        pltpu.VMEM((1, 2 * t), f32), pltpu.VMEM((hd, 2 * t), f32),
                        pltpu.VMEM((t, 2 * t), f32), pltpu.VMEM((1, 2 * t), f32),
                        pltpu.VMEM((t, 2 * t), f32), pltpu.VMEM((1, 2 * t), f32),
                        pltpu.VMEM((nrow, hd), bf16), pltpu.VMEM((nrow, 1), f32),
                        pltpu.VMEM((nrow, 1), f32), pltpu.VMEM((nrow, hd), f32)],
    )
    ck = cache_k.reshape(-1, hd)
    cv = cache_v.reshape(-1, hd)
    return pl.pallas_call(
        functools.partial(_attn_kernel, t=t, cb=ATT_QUERY_BLOCK, npg=npg, lam_init=lam_init),
        grid_spec=grid_spec,
        out_shape=[jax.ShapeDtypeStruct((batch * seq, nh * hd), bf16), jax.ShapeDtypeStruct((nb, nh, hd), f32)],
        compiler_params=pltpu.CompilerParams(dimension_semantics=("arbitrary",) * 3,
                                             vmem_limit_bytes=ATTN_VMEM_LIMIT_BYTES),
        name="attn",
    )(page_table, lam, far, q, k, vt, tiles, gain.reshape(hd, 1), qs, kn, vn, bias2, b0, gain.reshape(1, hd),
      *([ck] * npg), *([cv] * npg))


def _softmax_rows(s):
    e = jnp.exp(s - jnp.max(s, axis=-1, keepdims=True))
    return e / jnp.sum(e, axis=-1, keepdims=True)


def _cross_prompt_kernel(x_ref, g_ref, wq_ref, mk_ref, mv_ref, wo_ref, o_ref, a_ref):
    x = x_ref[...]
    d = x.shape[1]
    hd = d // MEM_HEADS
    h = _rms(x, g_ref[...]).astype(bf16)
    q = (_dot(h, wq_ref[...]) * (hd ** -0.5)).astype(bf16)
    for hh in range(MEM_HEADS):
        sl = slice(hh * hd, (hh + 1) * hd)
        a = _softmax_rows(_dot_nt(q[:, sl], mk_ref[:, sl])).astype(bf16)
        a_ref[:, sl] = _dot(a, mv_ref[:, sl]).astype(bf16)
    o_ref[...] = x + _dot(a_ref[...], wo_ref[...])


def _cross_prompt(x, g, wq, mk, mv, wo, seq):
    m, d = x.shape
    _, wq_spec, wq = _stacked(wq)
    _, wo_spec, wo = _stacked(wo)
    tm = ROW_TILE
    per = seq // tm
    mem = mk.shape[0] // (m // seq)
    return pl.pallas_call(
        _cross_prompt_kernel,
        grid=(m // tm,),
        in_specs=[pl.BlockSpec((tm, d), lambda i: (i, 0)), _resident((1, d)), wq_spec,
                  pl.BlockSpec((mem, d), lambda i: (i // per, 0)), pl.BlockSpec((mem, d), lambda i: (i // per, 0)),
                  wo_spec],
        out_specs=pl.BlockSpec((tm, d), lambda i: (i, 0)),
        out_shape=jax.ShapeDtypeStruct((m, d), f32),
        scratch_shapes=[pltpu.VMEM((tm, d), bf16)],
        compiler_params=_params("arbitrary"),
        name="cross_prompt",
    )(x, g.reshape(1, d), wq, mk, mv, wo)


def _cross_sample_kernel(q_ref, mk_ref, mv_ref, mask_ref, o_ref):
    nh, hd = q_ref.shape[1], q_ref.shape[2]
    q = jnp.concatenate([q_ref[0] * (hd ** -0.5), jnp.zeros((SUBLANES - nh, hd), f32)], axis=0)
    rows = mask_ref.shape[1]
    s = _dot_nt(q.astype(bf16), mk_ref[...].reshape(rows, hd).astype(bf16)) + mask_ref[...]
    o = _dot(_softmax_rows(s).astype(bf16), mv_ref[...].reshape(rows, hd).astype(bf16))
    o_ref[0] = o[0:nh]


def _cross_sample(q, cache_mk, cache_mv, layer):
    nb, nh, hd = q.shape
    rows = cache_mk.shape[2] * nh
    head_of_row = jnp.arange(rows, dtype=i32) % nh
    mask = jnp.where(head_of_row[None, :] == jnp.arange(SUBLANES, dtype=i32)[:, None], 0.0, NEG_INF).astype(f32)
    mask = mask.at[nh:].set(0.0)
    mem_spec = pl.BlockSpec((None, None, rows // nh, nh, hd), lambda b: (layer, b, 0, 0, 0))
    return pl.pallas_call(
        _cross_sample_kernel,
        grid=(nb,),
        in_specs=[pl.BlockSpec((1, nh, hd), lambda b: (b, 0, 0)), mem_spec, mem_spec, _resident((SUBLANES, rows))],
        out_specs=pl.BlockSpec((1, nh, hd), lambda b: (b, 0, 0)),
        out_shape=jax.ShapeDtypeStruct((nb, nh, hd), f32),
        compiler_params=_params("arbitrary"),
        name="cross_sample",
    )(q, cache_mk, cache_mv, mask)


def _rotary_tables(pos, half):
    inv = 1.0 / (10000.0 ** jnp.linspace(0.0, 1.0, half, dtype=f32))
    ang = pos.astype(f32)[:, None] * inv[None, :]
    cos, sin = jnp.cos(ang), jnp.sin(ang)
    return jnp.concatenate([cos, cos], axis=-1), jnp.concatenate([-sin, sin], axis=-1)


def kernel(x_prompt, x_sample, state_ret, state_pool, cache_k_diff, cache_v_diff, cache_mem_k, cache_mem_v, page_table, mem_prompt, norms, final_norm, ffn_w_gate, ffn_w_up, ffn_w_down, w_in_even, ret_gain, pool_w, pool_scale, w_out_even, w_qkv_odd, lambda_q1, lambda_k1, lambda_q2, lambda_k2, subln_gain, w_out_odd, rel_bias, w_cq, w_ckv, w_co):
    batch, seq, d = x_prompt.shape
    nb = x_sample.shape[0]
    depth = norms.shape[0]
    past = page_table.shape[1] * PAGE_SIZE
    mem_len = mem_prompt.shape[1]
    rw = RET_HEADS * RET_DK
    dw = DIFF_HEADS * DIFF_HD

    xp = x_prompt.reshape(batch * seq, d)
    xs = x_sample.reshape(nb, d)
    mem2d = mem_prompt.reshape(batch * mem_len, d)

    cos_p, sin_p = _rotary_tables(jnp.arange(seq, dtype=i32), RET_DK // 2)
    cos_s, sin_s = _rotary_tables(jnp.full((1,), past, dtype=i32), RET_DK // 2)

    wg, wu, wd = ffn_w_gate.astype(bf16), ffn_w_up.astype(bf16), ffn_w_down.astype(bf16)
    w_in, w_oe = w_in_even.astype(bf16), w_out_even.astype(bf16)
    w_qkv, w_oo = w_qkv_odd.astype(bf16), w_out_odd.astype(bf16)
    wcq, wckv, wco = w_cq.astype(bf16), w_ckv.astype(bf16), w_co.astype(bf16)
    pw = pool_w.astype(bf16)

    p_ret, p_pool, p_k, p_v, p_mk, p_mv = [], [], [], [], [], []
    s_ret, s_pool, s_k, s_v = [], [], [], []
    for l in range(depth):
        last = l == depth - 1
        ffn_w = lambda i: ((wg, (l, i)), (wu, (l, i)), (wd, (l, i)))
        xp = _ffn(xp, norms[l, 0], *ffn_w(0))
        xs = _ffn(xs, norms[l, 0], *ffn_w(0))
        if l % 2 == 0:
            e = l // 2
            zw = w_in.shape[2]
            (zp,) = _proj(xp, norms[l, 1], (w_in, (e,)), [(0, zw, [(0, 1.0)])], [(zw, f32)])
            (zs,) = _proj(xs, norms[l, 1], (w_in, (e,)), [(0, zw, [(0, 1.0)])], [(zw, f32)])
            xp, rp, pp = _even_prompt(xp, zp, cos_p, sin_p, ret_gain[e], pw[e], pool_scale[e], (w_oe, (e,)),
                                      batch, seq)
            mixs, rs, ps = _even_sample(zs, cos_s, sin_s, state_ret[e:e + 1], state_pool[e:e + 1], ret_gain[e],
                                        pool_w[e], pool_scale[e], past)
            xs = _mmres(xs, mixs.reshape(nb, rw + pool_scale.shape[1]), (w_oe, (e,)))
            p_ret.append(rp[0])
            p_pool.append(pp[0])
            s_ret.append(rs[0])
            s_pool.append(ps[0])
        else:
            o = l // 2
            lam_init = 0.8 - 0.6 * math.exp(-0.3 * l)
            scale = DIFF_D ** -0.5
            tiles, tab, lam_t = _bias_tables(rel_bias, lambda_q1[o], lambda_k1[o], lambda_q2[o], lambda_k2[o],
                                             lam_init, ATT_TILE)
            lam = lam_t[0, 0:1]
            tab = tab[:, 0, :]
            far = tab[:, REL_MAX_DIST]
            qkv_groups = [(0, dw, [(0, scale * LOG2E)]), (dw, dw, [(1, 1.0), (3, 1.0)]),
                          (2 * dw, dw, [(2, 1.0), (4, 1.0)])]
            qb, kp, vp, kb, vtb = _proj(xp, norms[l, 1], (w_qkv, (o,)), qkv_groups,
                                        [(dw, bf16), (dw, f32), (dw, f32), (dw, bf16), (dw, bf16, "T")], seq=seq)
            qs, ks_, vs = _proj(xs, norms[l, 1], (w_qkv, (o,)),
                                [(0, dw, [(0, scale)]), (dw, dw, [(1, 1.0)]), (2 * dw, dw, [(2, 1.0)])],
                                [(dw, f32), (dw, f32), (dw, f32)])
            nh = DIFF_HEADS
            pages_per_step = page_table.shape[1] // (seq // ATT_TILE)
            own = jnp.arange(nh)[:, None, None] == jnp.arange(nh)[None, None, :]
            far_page = jnp.where(own, far[:, None, None], NEG_INF) + jnp.zeros((1, PAGE_SIZE, 1), f32)
            last_page = jnp.where(own, tab[:, PAGE_SIZE:0:-1][:, :, None], NEG_INF)
            far_page, last_page = far_page.reshape(nh, -1), last_page.reshape(nh, -1)
            far_step = jnp.tile(far_page, (2, pages_per_step))
            last_step = jnp.concatenate([far_step[:, :-PAGE_SIZE * nh], jnp.tile(last_page, (2, 1))], axis=1)
            bias2 = jnp.stack([far_step, last_step])
            b0 = jnp.tile(tab[:, 0], 2)[:, None]
            heads = lambda a: a.reshape(nb, nh, DIFF_HD)
            attp, atts = _attn(qb, kb, vtb, tiles, far, lam, subln_gain[o], batch, seq,
                               heads(qs), heads(ks_), heads(vs), cache_k_diff, cache_v_diff, o, page_table,
                               bias2, b0, lam_init)
            xp = _mmres(xp, attp, (w_oo, (o,)))
            xs = _mmres(xs, atts.reshape(nb, dw), (w_oo, (o,)))
            p_k.append(kp.reshape(batch, seq, DIFF_HEADS, DIFF_HD))
            p_v.append(vp.reshape(batch, seq, DIFF_HEADS, DIFF_HD))
            s_k.append(ks_.reshape(nb, 1, DIFF_HEADS, DIFF_HD))
            s_v.append(vs.reshape(nb, 1, DIFF_HEADS, DIFF_HD))
        mk, mv, mkb, mvb = _proj(mem2d, None, (wckv, (l,)),
                                 [(0, d, [(0, 1.0), (2, 1.0)]), (d, d, [(1, 1.0), (3, 1.0)])],
                                 [(d, f32), (d, f32), (d, bf16), (d, bf16)])
        p_mk.append(mk.reshape(batch, mem_len, MEM_HEADS, d // MEM_HEADS))
        p_mv.append(mv.reshape(batch, mem_len, MEM_HEADS, d // MEM_HEADS))
        xp = _cross_prompt(xp, norms[l, 2], (wcq, (l,)), mkb, mvb, (wco, (l,)), seq)
        (qcs,) = _proj(xs, norms[l, 2], (wcq, (l,)), [(0, d, [(0, 1.0)])], [(d, f32)])
        acs = _cross_sample(qcs.reshape(nb, MEM_HEADS, d // MEM_HEADS), cache_mem_k, cache_mem_v, l)
        xs = _mmres(xs, acs.reshape(nb, d), (wco, (l,)))
        fin = final_norm if last else None
        xp = _ffn(xp, norms[l, 3], *ffn_w(1), fin)
        xs = _ffn(xs, norms[l, 3], *ffn_w(1), fin)

    return (xp.reshape(batch, seq, d), xs.reshape(nb, 1, d),
            jnp.stack(p_ret), jnp.stack(p_pool), jnp.stack(p_k), jnp.stack(p_v), jnp.stack(p_mk), jnp.stack(p_mv),
            jnp.stack(s_ret), jnp.stack(s_pool), jnp.stack(s_k), jnp.stack(s_v))
```

```python
import functools
import math

import jax
import jax.numpy as jnp
from jax import lax
from jax.experimental import pallas as pl
from jax.experimental.pallas import tpu as pltpu

f32 = jnp.float32
bf16 = jnp.bfloat16
i32 = jnp.int32

EPS = 1e-6
NEG_INF = -1e30
LOG2E = math.log2(math.e)

RET_HEADS = 4
RET_DK = 128
POOL_WINDOWS = (2, 4, 8, 16)
POOL_KEEP = max(POOL_WINDOWS) - 1
DIFF_HEADS = 8
DIFF_D = 64
DIFF_HD = 2 * DIFF_D
REL_BUCKETS = 32
REL_MAX_DIST = 128
MEM_HEADS = 4
PAGE_SIZE = 128

VMEM_LIMIT_BYTES = 56 * 1024 * 1024
LANES = 128
SUBLANES = 8
MXU_TILE = 256

ROW_TILE = 512
ATT_TILE = 512
ATT_QUERY_BLOCK = 512
ATTN_VMEM_LIMIT_BYTES = 60 * 1024 * 1024


def _params(*sem):
    return pltpu.CompilerParams(dimension_semantics=sem, vmem_limit_bytes=VMEM_LIMIT_BYTES)


def _rms(x, g):
    return x * lax.rsqrt(jnp.mean(x * x, axis=-1, keepdims=True) + EPS) * g


def _dot(a, b):
    return jnp.dot(a, b, preferred_element_type=f32)


def _dot_nt(a, b):
    return lax.dot_general(a, b, (((1,), (1,)), ((), ())), preferred_element_type=f32)


def _dot_tn(a, b):
    return lax.dot_general(a, b, (((0,), (0,)), ((), ())), preferred_element_type=f32)


def _resident(shape, lead=()):
    nd = len(shape)
    return pl.BlockSpec((None,) * len(lead) + tuple(shape), lambda *_: tuple(lead) + (0,) * nd,
                        pipeline_mode=pl.Buffered(1))


def _stacked(w):
    arr, lead = w
    shape = arr.shape[len(lead):]
    return shape, _resident(shape, lead), arr


def _ffn_kernel(x_ref, g_ref, wg_ref, wu_ref, wd_ref, *rest, bounds, final):
    o_ref = rest[-1]
    x = x_ref[...]
    h = _rms(x, g_ref[...]).astype(bf16)
    acc = jnp.zeros(x.shape, f32)
    for c0, c1 in zip(bounds[:-1], bounds[1:]):
        gate = _dot(h, wg_ref[:, c0:c1])
        up = _dot(h, wu_ref[:, c0:c1])
        a = (jax.nn.silu(gate) * up).astype(bf16)
        acc = acc + _dot(a, wd_ref[c0:c1, :])
    y = x + 0.5 * acc
    if final:
        y = _rms(y, rest[0][...])
    o_ref[...] = y


def _ffn(x, g, wg, wu, wd, final_g=None):
    m, d = x.shape
    (_, f), wg_spec, wg = _stacked(wg)
    _, wu_spec, wu = _stacked(wu)
    _, wd_spec, wd = _stacked(wd)
    tm = min(ROW_TILE, m)
    split = -(-f // (2 * MXU_TILE)) * MXU_TILE
    bounds = (0, split, f) if split < f else (0, f)
    final = final_g is not None
    in_specs = [pl.BlockSpec((tm, d), lambda i: (i, 0)), _resident((1, d)), wg_spec, wu_spec, wd_spec]
    args = [x, g.reshape(1, d), wg, wu, wd]
    if final:
        in_specs.append(_resident((1, d)))
        args.append(final_g.reshape(1, d))
    return pl.pallas_call(
        functools.partial(_ffn_kernel, bounds=bounds, final=final),
        grid=(m // tm,),
        in_specs=in_specs,
        out_specs=pl.BlockSpec((tm, d), lambda i: (i, 0)),
        out_shape=jax.ShapeDtypeStruct((m, d), f32),
        compiler_params=_params("arbitrary"),
        name="ffn",
    )(*args)


def _proj_kernel(*refs, norm, groups, transposed, nchunk):
    x_ref = refs[0]
    g_ref = refs[1] if norm else None
    w_ref = refs[2] if norm else refs[1]
    o_refs = refs[(3 if norm else 2):]
    x = x_ref[...]
    h = (_rms(x, g_ref[...]) if norm else x).astype(bf16)
    for off, width, targets in groups:
        for c0 in range(0, width, nchunk):
            n = min(nchunk, width - c0)
            r = _dot(h, w_ref[:, off + c0:off + c0 + n])
            for oi, scale in targets:
                o_ref = o_refs[oi]
                val = (r if scale == 1.0 else r * scale)
                if transposed[oi]:
                    o_ref[c0:c0 + n, :] = val.T.astype(o_ref.dtype)
                else:
                    o_ref[:, c0:c0 + n] = val.astype(o_ref.dtype)


def _proj(x, g, w, groups, outs, seq=None):
    m, d = x.shape
    _, w_spec, w = _stacked(w)
    tm = min(ROW_TILE, m)
    norm = g is not None
    in_specs = [pl.BlockSpec((tm, d), lambda i: (i, 0))]
    args = [x]
    if norm:
        in_specs.append(_resident((1, d)))
        args.append(g.reshape(1, d))
    in_specs.append(w_spec)
    args.append(w)
    out_specs, out_shape = [], []
    for wd, dt, *flag in outs:
        if flag:
            per = seq // tm
            out_specs.append(pl.BlockSpec((wd, tm), lambda i: (i // per, i % per)))
            out_shape.append(jax.ShapeDtypeStruct((m // seq * wd, seq), dt))
        else:
            out_specs.append(pl.BlockSpec((tm, wd), lambda i: (i, 0)))
            out_shape.append(jax.ShapeDtypeStruct((m, wd), dt))
    return pl.pallas_call(
        functools.partial(_proj_kernel, norm=norm, groups=groups, transposed=tuple(o[2:] == ("T",) for o in outs),
                          nchunk=512),
        grid=(m // tm,),
        in_specs=in_specs,
        out_specs=out_specs,
        out_shape=out_shape,
        compiler_params=_params("arbitrary"),
        name="proj",
    )(*args)


def _mmres_kernel(x_ref, a_ref, w_ref, o_ref):
    o_ref[...] = x_ref[...] + _dot(a_ref[...].astype(bf16), w_ref[...])


def _mmres(x, a, w):
    m, d = x.shape
    k = a.shape[1]
    _, w_spec, w = _stacked(w)
    tm = min(ROW_TILE, m)
    return pl.pallas_call(
        _mmres_kernel,
        grid=(m // tm,),
        in_specs=[pl.BlockSpec((tm, d), lambda i: (i, 0)), pl.BlockSpec((tm, k), lambda i: (i, 0)), w_spec],
        out_specs=pl.BlockSpec((tm, d), lambda i: (i, 0)),
        out_shape=jax.ShapeDtypeStruct((m, d), f32),
        compiler_params=_params("arbitrary"),
        name="mmres",
    )(x, a, w)


def _ret_log_gamma(h):
    return math.log1p(-(2.0 ** (-5.0 - h)))


def _rotate(x, cosf, sinf):
    return x * cosf + pltpu.roll(x, x.shape[-1] // 2, axis=1) * sinf


def _group_norm_gate(o, gain, gate):
    mu = jnp.mean(o, axis=-1, keepdims=True)
    var = jnp.mean(jnp.square(o - mu), axis=-1, keepdims=True)
    return jax.nn.silu(gate) * ((o - mu) * lax.rsqrt(var + EPS) * gain)


def _even_prompt_kernel(x_ref, zq_ref, zk_ref, zv_ref, zg_ref, zp_ref, cos_ref, sin_ref, gain_ref, pw_ref, ps_ref,
                        wo_ref, xo_ref, sret_ref, spool_ref, state_ref, ext_ref, intra_ref, mix_ref, *, t):
    b = pl.program_id(0)
    c = pl.program_id(1)
    nc = pl.num_programs(1)
    halo = POOL_KEEP + 1
    dk = RET_DK

    @pl.when(jnp.logical_and(b == 0, c == 0))
    def _():
        rel = lax.broadcasted_iota(i32, (t, t), 0) - lax.broadcasted_iota(i32, (t, t), 1)
        relf = jnp.maximum(rel, 0).astype(f32)
        for h in range(RET_HEADS):
            intra_ref[h] = jnp.where(rel >= 0, jnp.exp(_ret_log_gamma(h) * relf), 0.0)

    @pl.when(c == 0)
    def _():
        state_ref[...] = jnp.zeros(state_ref.shape, f32)
        ext_ref[0:halo, :] = jnp.zeros((halo, ext_ref.shape[1]), f32)

    cosf = cos_ref[...]
    sinf = sin_ref[...]
    row = lax.broadcasted_iota(i32, (t, 1), 0).astype(f32)
    for h in range(RET_HEADS):
        sl = slice(h * dk, (h + 1) * dk)
        lg = _ret_log_gamma(h)
        qr = _rotate(zq_ref[:, sl], cosf, sinf)
        kr = _rotate(zk_ref[:, sl], cosf, sinf) * (dk ** -0.5)
        qb = qr.astype(bf16)
        vb = zv_ref[:, sl].astype(bf16)
        att = _dot_nt(qb, kr.astype(bf16)) * intra_ref[h]
        s_old = state_ref[h]
        o = _dot(att.astype(bf16), vb) + _dot(qb, s_old.astype(bf16)) * jnp.exp(lg * (row + 1.0))
        k_dec = jnp.exp(lg * (float(t - 1) - row))
        state_ref[h] = s_old * math.exp(lg * t) + _dot_tn((kr * k_dec).astype(bf16), vb)
        mix_ref[:, sl] = _group_norm_gate(o, gain_ref[:, sl], zg_ref[:, sl]).astype(mix_ref.dtype)

    p = zp_ref[...]
    ext_ref[halo:halo + t, :] = p
    pos = c * t + lax.broadcasted_iota(i32, (t, 1), 0)
    gw = p.shape[1] // len(POOL_WINDOWS)
    ret_w = RET_HEADS * dk
    for gi, w in enumerate(POOL_WINDOWS):
        sl = slice(gi * gw, (gi + 1) * gw)
        win = p[:, sl]
        for back in range(1, w):
            win = win + ext_ref[halo - back:halo - back + t, sl]
        cnt = jnp.minimum(w, pos + 1).astype(f32)
        pooled = win / cnt - p[:, sl]
        mixed = _dot(pooled.astype(bf16), pw_ref[gi]) * ps_ref[:, sl]
        mix_ref[:, ret_w + gi * gw:ret_w + (gi + 1) * gw] = mixed.astype(mix_ref.dtype)

    xo_ref[...] = x_ref[...] + _dot(mix_ref[...], wo_ref[...])

    @pl.when(c == nc - 1)
    def _():
        sret_ref[0, 0] = state_ref[...]
        spool_ref[0, 0] = ext_ref[t + 1:t + halo, :]

    ext_ref[0:halo, :] = ext_ref[t:t + halo, :]


def _even_prompt(x, z, cosf, sinf, gain, pw, ps, wo, batch, seq):
    t = ROW_TILE
    nc = seq // t
    d = x.shape[1]
    rw = RET_HEADS * RET_DK
    pwid = ps.shape[0]
    _, wo_spec, wo = _stacked(wo)
    row = lambda b, c: (b * nc + c, 0)
    zspec = lambda j: pl.BlockSpec((t, rw), lambda b, c: (b * nc + c, j))
    return pl.pallas_call(
        functools.partial(_even_prompt_kernel, t=t),
        grid=(batch, nc),
        in_specs=[pl.BlockSpec((t, d), row), zspec(0), zspec(1), zspec(2), zspec(3), zspec(4),
                  pl.BlockSpec((t, RET_DK), lambda b, c: (c, 0)), pl.BlockSpec((t, RET_DK), lambda b, c: (c, 0)),
                  _resident((1, rw)), _resident(pw.shape), _resident((1, pwid)), wo_spec],
        out_specs=[pl.BlockSpec((t, d), row),
                   pl.BlockSpec((1, 1, RET_HEADS, RET_DK, RET_DK), lambda b, c: (0, b, 0, 0, 0)),
                   pl.BlockSpec((1, 1, POOL_KEEP, pwid), lambda b, c: (0, b, 0, 0))],
        out_shape=[jax.ShapeDtypeStruct((batch * seq, d), f32),
                   jax.ShapeDtypeStruct((1, batch, RET_HEADS, RET_DK, RET_DK), f32),
                   jax.ShapeDtypeStruct((1, batch, POOL_KEEP, pwid), f32)],
        scratch_shapes=[pltpu.VMEM((RET_HEADS, RET_DK, RET_DK), f32),
                        pltpu.VMEM((POOL_KEEP + 1 + t, pwid), f32),
                        pltpu.VMEM((RET_HEADS, t, t), f32),
                        pltpu.VMEM((t, rw + pwid), bf16)],
        compiler_params=_params("arbitrary", "arbitrary"),
        name="even_prompt",
    )(x, z, z, z, z, z, cosf, sinf, gain.reshape(1, rw), pw, ps.reshape(1, pwid), wo)


def _even_sample_kernel(z_ref, cos_ref, sin_ref, s0_ref, rows_ref, gain_ref, pw_ref, ps_ref,
                        mix_ref, sret_ref, spool_ref, ext_ref, *, pos):
    dk = RET_DK
    rw = RET_HEADS * dk
    z = jnp.broadcast_to(z_ref[0], (SUBLANES, z_ref.shape[2]))
    cosf = jnp.broadcast_to(cos_ref[...], (SUBLANES, dk))
    sinf = jnp.broadcast_to(sin_ref[...], (SUBLANES, dk))
    first = lax.broadcasted_iota(i32, (SUBLANES, dk), 0) == 0
    for h in range(RET_HEADS):
        sl = slice(h * dk, (h + 1) * dk)
        gamma = math.exp(_ret_log_gamma(h))
        qr = _rotate(z[:, sl], cosf, sinf)
        kr = _rotate(z[:, rw + h * dk:rw + (h + 1) * dk], cosf, sinf) * (dk ** -0.5)
        v = z[:, 2 * rw + h * dk:2 * rw + (h + 1) * dk]
        s_new = s0_ref[0, 0, h] * gamma + _dot_tn(jnp.where(first, kr, 0.0), v)
        sret_ref[0, 0, h] = s_new
        o = _dot(qr, s_new)
        gate = z[:, 3 * rw + h * dk:3 * rw + (h + 1) * dk]
        mix_ref[0, :, sl] = _group_norm_gate(o, gain_ref[:, sl], gate)[0:1]

    keep = POOL_KEEP
    pwid = ps_ref.shape[1]
    p = z[0:1, 4 * rw:4 * rw + pwid]
    ext_ref[0:keep, :] = rows_ref[0, 0]
    ext_ref[keep:keep + 1, :] = p
    spool_ref[0, 0] = ext_ref[1:keep + 1, :]
    gw = pwid // len(POOL_WINDOWS)
    for gi, w in enumerate(POOL_WINDOWS):
        sl = slice(gi * gw, (gi + 1) * gw)
        win = jnp.sum(ext_ref[keep + 1 - w:keep + 1, sl], axis=0, keepdims=True)
        pooled = win / float(min(w, pos + 1)) - p[:, sl]
        mixed = _dot(jnp.broadcast_to(pooled, (SUBLANES, gw)), pw_ref[gi])[0:1] * ps_ref[:, sl]
        mix_ref[0, :, rw + gi * gw:rw + (gi + 1) * gw] = mixed


def _even_sample(z, cosf, sinf, state_ret, state_pool, gain, pw, ps, pos):
    nb, zw = z.shape
    rw = RET_HEADS * RET_DK
    pwid = ps.shape[0]
    return pl.pallas_call(
        functools.partial(_even_sample_kernel, pos=pos),
        grid=(nb,),
        in_specs=[pl.BlockSpec((1, 1, zw), lambda b: (b, 0, 0)),
                  _resident((1, RET_DK)), _resident((1, RET_DK)),
                  pl.BlockSpec((1, 1, RET_HEADS, RET_DK, RET_DK), lambda b: (0, b, 0, 0, 0)),
                  pl.BlockSpec((1, 1, POOL_KEEP, pwid), lambda b: (0, b, 0, 0)),
                  _resident((1, rw)), _resident(pw.shape), _resident((1, pwid))],
        out_specs=[pl.BlockSpec((1, 1, rw + pwid), lambda b: (b, 0, 0)),
                   pl.BlockSpec((1, 1, RET_HEADS, RET_DK, RET_DK), lambda b: (0, b, 0, 0, 0)),
                   pl.BlockSpec((1, 1, POOL_KEEP, pwid), lambda b: (0, b, 0, 0))],
        out_shape=[jax.ShapeDtypeStruct((nb, 1, rw + pwid), f32),
                   jax.ShapeDtypeStruct((1, nb, RET_HEADS, RET_DK, RET_DK), f32),
                   jax.ShapeDtypeStruct((1, nb, POOL_KEEP, pwid), f32)],
        scratch_shapes=[pltpu.VMEM((POOL_KEEP + 1, pwid), f32)],
        compiler_params=_params("arbitrary"),
        name="even_sample",
    )(z.reshape(nb, 1, zw), cosf, sinf, state_ret, state_pool, gain.reshape(1, rw), pw, ps.reshape(1, pwid))


def _rel_bias_of(dist, rb_ref, h):
    n = jnp.maximum(dist, 0)
    max_exact = REL_BUCKETS // 2
    nf = jnp.maximum(n, 1).astype(f32)
    large = max_exact + (jnp.log(nf / max_exact) / math.log(REL_MAX_DIST / max_exact)
                         * (REL_BUCKETS - max_exact)).astype(i32)
    large = jnp.minimum(large, REL_BUCKETS - 1)
    bucket = jnp.where(n < max_exact, n, large)
    out = jnp.zeros(dist.shape, f32)
    for bk in range(REL_BUCKETS):
        out = jnp.where(bucket == bk, rb_ref[bk, h], out)
    return out


def _bias_kernel(rb_ref, lq1_ref, lk1_ref, lq2_ref, lk2_ref, tile_ref, tab_ref, lam_ref, *, t, lam_init):
    h = pl.program_id(0)
    nd = REL_MAX_DIST
    rel = lax.broadcasted_iota(i32, (nd, nd), 1) - lax.broadcasted_iota(i32, (nd, nd), 0)
    on_diag = jnp.where(rel >= 0, _rel_bias_of(rel, rb_ref, h) * LOG2E, NEG_INF)
    next_diag = _rel_bias_of(rel + nd, rb_ref, h) * LOG2E
    far = _rel_bias_of(jnp.full((nd, nd), nd, i32), rb_ref, h) * LOG2E
    masked = jnp.full((nd, nd), NEG_INF, f32)
    nblk = t // nd
    for bk in range(nblk):
        for bq in range(nblk):
            diag_blk = on_diag if bq == bk else next_diag if bq == bk + 1 else far if bq > bk else masked
            below_blk = next_diag if (bk == nblk - 1 and bq == 0) else far
            tile_ref[0, 0, bk * nd:(bk + 1) * nd, bq * nd:(bq + 1) * nd] = diag_blk
            tile_ref[0, 1, bk * nd:(bk + 1) * nd, bq * nd:(bq + 1) * nd] = below_blk
    tab_ref[0] = _rel_bias_of(lax.broadcasted_iota(i32, (1, 2 * REL_MAX_DIST), 1), rb_ref, h)
    lam = (jnp.exp(jnp.sum(lq1_ref[...] * lk1_ref[...], axis=-1, keepdims=True))
           - jnp.exp(jnp.sum(lq2_ref[...] * lk2_ref[...], axis=-1, keepdims=True)) + lam_init)
    lam_ref[...] = jnp.broadcast_to(lam, lam_ref.shape)


def _bias_tables(rel_bias, lq1, lk1, lq2, lk2, lam_init, t):
    nh = rel_bias.shape[1]
    d = lq1.shape[0]
    vec = lambda a: a.reshape(1, d)
    return pl.pallas_call(
        functools.partial(_bias_kernel, t=t, lam_init=lam_init),
        grid=(nh,),
        in_specs=[pl.BlockSpec(memory_space=pltpu.SMEM)] + [_resident((1, d))] * 4,
        out_specs=[pl.BlockSpec((1, 2, t, t), lambda h: (h, 0, 0, 0)),
                   pl.BlockSpec((1, 1, 2 * REL_MAX_DIST), lambda h: (h, 0, 0)),
                   pl.BlockSpec((SUBLANES, LANES), lambda h: (0, 0))],
        out_shape=[jax.ShapeDtypeStruct((nh, 2, t, t), f32),
                   jax.ShapeDtypeStruct((nh, 1, 2 * REL_MAX_DIST), f32),
                   jax.ShapeDtypeStruct((SUBLANES, LANES), f32)],
        compiler_params=_params("arbitrary"),
        name="bias_tables",
    )(rel_bias, vec(lq1), vec(lk1), vec(lq2), vec(lk2))


def _sub_ln(o, gain, lam_init):
    return o * lax.rsqrt(jnp.mean(o * o, axis=-1, keepdims=True) + EPS) * gain * (1.0 - lam_init)


def _attn_prompt_tile(lam_ref, far_ref, q_ref, k_ref, vt_ref, tile_ref, gain_ref, o_ref,
                      qq_ref, m_ref, l_ref, acc_ref, s0_ref, smax0_ref, s1_ref, smax1_ref, *, t, cb, lam_init,
                      beside_own_tile):
    h = pl.program_id(1)
    qi = pl.program_id(2)
    q = q_ref[...]
    lane = lax.broadcasted_iota(i32, q.shape, 1)
    zero = jnp.zeros_like(q)
    qq_ref[0:t, :] = jnp.where(lane < DIFF_D, q, zero)
    qq_ref[t:2 * t, :] = jnp.where(lane >= DIFF_D, q, zero)
    m_ref[...] = jnp.full(m_ref.shape, NEG_INF, f32)
    l_ref[...] = jnp.zeros(l_ref.shape, f32)
    acc_ref[...] = jnp.zeros(acc_ref.shape, f32)

    far = far_ref[h] * LOG2E

    blocks = [slice(c0, c0 + cb) for c0 in range(0, 2 * t, cb)]

    parked = ((s0_ref, smax0_ref), (s1_ref, smax1_ref))

    def scores(ki, slot):
        s_ref, smax_ref = parked[slot]
        kt = k_ref[pl.ds(pl.multiple_of(ki * t, t), t), :]
        for cols in blocks:
            s = _dot_nt(kt, qq_ref[cols, :])
            s_ref[:, cols] = s
            smax_ref[:, cols] = jnp.max(s, axis=0, keepdims=True)

    def scores_own(ki, slot):
        s_ref, _ = parked[slot]
        k0 = pl.multiple_of(ki * t, t)
        for c0 in (0, t):
            for q0, nk in ((0, t // 2), (t // 2, t)):
                cols = slice(c0 + q0, c0 + q0 + t // 2)
                s_ref[0:nk, cols] = _dot_nt(k_ref[pl.ds(k0, nk), :], qq_ref[cols, :])

    def update(cols, m_old, m_new, p, vt):
        alpha = jnp.exp2(m_old - m_new)
        l_ref[:, cols] = alpha * l_ref[:, cols] + jnp.sum(p, axis=0, keepdims=True)
        acc_ref[:, cols] = alpha * acc_ref[:, cols] + _dot(vt, p.astype(bf16))
        m_ref[:, cols] = m_new

    def absorb_far(ki, slot):
        s_ref, smax_ref = parked[slot]
        vt = vt_ref[:, pl.ds(pl.multiple_of(ki * t, t), t)]
        for cols in blocks:
            m_old = m_ref[:, cols]
            m_new = jnp.maximum(m_old, smax_ref[:, cols] + far)
            update(cols, m_old, m_new, jnp.exp2(s_ref[:, cols] - (m_new - far)), vt)

    def absorb_behind(ki, slot):
        s_ref, smax_ref = parked[slot]
        vt = vt_ref[:, pl.ds(pl.multiple_of(ki * t, t), t)]
        corner = tile_ref[0, 1, t - nd:t, 0:nd]
        for c0 in (0, t):
            cols, near, rest = slice(c0, c0 + t), slice(c0, c0 + nd), slice(c0 + nd, c0 + t)
            m_old = m_ref[:, cols]
            s_near = jnp.concatenate([s_ref[0:t - nd, near] + far, s_ref[t - nd:t, near] + corner], axis=0)
            m_near = jnp.maximum(m_old[:, 0:nd], jnp.max(s_near, axis=0, keepdims=True))
            m_rest = jnp.maximum(m_old[:, nd:t], smax_ref[:, rest] + far)
            p = jnp.concatenate([jnp.exp2(s_near - m_near), jnp.exp2(s_ref[:, rest] - (m_rest - far))], axis=1)
            update(cols, m_old, jnp.concatenate([m_near, m_rest], axis=1), p, vt)

    def absorb_own(ki, slot):
        s_ref, _ = parked[slot]
        k0 = pl.multiple_of(ki * t, t)
        for c0 in (0, t):
            for q0, nk in ((0, t // 2), (t // 2, t)):
                cols = slice(c0 + q0, c0 + q0 + t // 2)
                m_old = m_ref[:, cols]
                s = s_ref[0:nk, cols] + tile_ref[0, 0, 0:nk, q0:q0 + t // 2]
                m_new = jnp.maximum(m_old, jnp.max(s, axis=0, keepdims=True))
                update(cols, m_old, m_new, jnp.exp2(s - m_new), vt_ref[:, pl.ds(k0, nk)])

    nd = REL_MAX_DIST
    n_far = jnp.maximum(qi - 1, 0)
    scores(0, 0)

    def far_pair(j, carry):
        for slot in (0, 1):
            scores(2 * j + slot + 1, 1 - slot)
            absorb_far(2 * j + slot, slot)
        return carry

    lax.fori_loop(0, n_far // 2, far_pair, 0)

    @pl.when(n_far % 2 == 1)
    def _():
        scores(n_far, 1)
        absorb_far(n_far - 1, 0)

    for slot in (0, 1):
        @pl.when(jnp.logical_and(qi >= 1, n_far % 2 == slot))
        def _():
            scores_own(qi, 1 - slot)
            absorb_behind(qi - 1, slot)

    for slot in (0, 1):
        @pl.when(qi % 2 == slot)
        def _():
            beside_own_tile()
            absorb_own(qi, slot)

    o = acc_ref[...] / l_ref[...]
    o = o[:, 0:t] - lam_ref[0] * o[:, t:2 * t]
    y = o * lax.rsqrt(jnp.mean(o * o, axis=0, keepdims=True) + EPS) * gain_ref[...] * (1.0 - lam_init)
    o_ref[...] = y.T.astype(o_ref.dtype)


def _attn_decode_parts(lam_ref, q_ref, kn_ref, vn_ref, bias_ref, b0_ref, gain_ref, k_refs, v_refs, o_ref,
                       qh_ref, m_ref, l_ref, acc_ref, *, chunk_axis, lam_init):
    c = pl.program_id(chunk_axis)
    nc = pl.num_programs(chunk_axis)
    nh = DIFF_HEADS
    prow = PAGE_SIZE * nh

    def both_streams(x):
        return jnp.concatenate([x, x], axis=0)

    def start():
        @pl.when(c == 0)
        def _():
            q = q_ref[0]
            lane = lax.broadcasted_iota(i32, q.shape, 1)
            qh = jnp.concatenate([jnp.where(lane < DIFF_D, q, 0.0), jnp.where(lane >= DIFF_D, q, 0.0)],
                                 axis=0).astype(bf16)
            qh_ref[...] = qh
            kn = both_streams(kn_ref[0]).astype(bf16).astype(f32)
            m_ref[...] = jnp.sum(qh.astype(f32) * kn, axis=-1, keepdims=True) + b0_ref[...]
            l_ref[...] = jnp.ones(l_ref.shape, f32)
            acc_ref[...] = both_streams(vn_ref[0]).astype(bf16).astype(f32)

    def absorb_pages():
        qh = qh_ref[...]
        s = jnp.concatenate([_dot_nt(qh, k_ref[...].astype(bf16)) for k_ref in k_refs], axis=1)
        s = s + bias_ref[(c == nc - 1).astype(i32)]
        m_old = m_ref[...]
        m_new = jnp.maximum(m_old, jnp.max(s, axis=-1, keepdims=True))
        alpha = jnp.exp(m_old - m_new)
        p = jnp.exp(s - m_new)
        l_ref[...] = alpha * l_ref[...] + jnp.sum(p, axis=-1, keepdims=True)
        pv = alpha * acc_ref[...]
        for i, v_ref in enumerate(v_refs):
            pv = pv + _dot(p[:, i * prow:(i + 1) * prow].astype(bf16), v_ref[...].astype(bf16))
        acc_ref[...] = pv
        m_ref[...] = m_new

    def finish():
        @pl.when(c == nc - 1)
        def _():
            on = acc_ref[...] / l_ref[...]
            o_ref[0] = _sub_ln(on[0:nh] - lam_ref[0] * on[nh:2 * nh], gain_ref[...], lam_init)

    return start, absorb_pages, finish


N_PROMPT_SCRATCH = 8


def _attn_kernel(pt_ref, lam_ref, far_ref, q_ref, k_ref, vt_ref, tile_ref, gain_col_ref,
                 qs_ref, kn_ref, vn_ref, bias_ref, b0_ref, gain_row_ref, *rest, t, cb, npg, lam_init):
    k_pages, v_pages = rest[:npg], rest[npg:2 * npg]
    o_ref, os_ref = rest[2 * npg:2 * npg + 2]
    scratch = rest[2 * npg + 2:]
    del pt_ref
    start, absorb_pages, finish = _attn_decode_parts(
        lam_ref, qs_ref, kn_ref, vn_ref, bias_ref, b0_ref, gain_row_ref, k_pages, v_pages, os_ref,
        *scratch[N_PROMPT_SCRATCH:], chunk_axis=2, lam_init=lam_init)
    start()
    _attn_prompt_tile(lam_ref, far_ref, q_ref, k_ref, vt_ref, tile_ref, gain_col_ref, o_ref,
                      *scratch[:N_PROMPT_SCRATCH], t=t, cb=cb, lam_init=lam_init, beside_own_tile=absorb_pages)
    finish()


def _attn(q, k, vt, tiles, far, lam, gain, batch, seq, qs, kn, vn, cache_k, cache_v, layer, page_table, bias2, b0,
          lam_init):
    t = ATT_TILE
    nq = seq // t
    hd = DIFF_HD
    nh = DIFF_HEADS
    nb = qs.shape[0]
    pool = cache_k.shape[1]
    npg = page_table.shape[1] // nq
    assert nb == batch * nh and npg * nq == page_table.shape[1], "one sample sequence per prompt (batch, head)"
    nrow = 2 * nh
    prow = PAGE_SIZE * nh
    smem = pl.BlockSpec(memory_space=pltpu.SMEM)
    page_spec = lambda j: pl.BlockSpec(
        (prow, hd), lambda b, h, i, pt: (layer * pool + pt[b * nh + h, i * npg + j], 0))
    rowspec = pl.BlockSpec((1, nh, hd), lambda b, h, i, pt: (b * nh + h, 0, 0))
    const = lambda shape: pl.BlockSpec(shape, lambda b, h, i, pt: (0,) * len(shape), pipeline_mode=pl.Buffered(1))
    grid_spec = pltpu.PrefetchScalarGridSpec(
        num_scalar_prefetch=1,
        grid=(batch, nh, nq),
        in_specs=[smem, smem,
                  pl.BlockSpec((t, hd), lambda b, h, i, pt: (b * nq + i, h)),
                  pl.BlockSpec((seq, hd), lambda b, h, i, pt: (b, h)),
                  pl.BlockSpec((hd, seq), lambda b, h, i, pt: (b * nh + h, 0)),
                  pl.BlockSpec((1, 2, t, t), lambda b, h, i, pt: (h, 0, 0, 0)),
                  const((hd, 1)),
                  rowspec, rowspec, rowspec, const((2, nrow, npg * prow)), const((nrow, 1)), const((1, hd))]
                 + [page_spec(j) for j in range(npg)] * 2,
        out_specs=[pl.BlockSpec((t, hd), lambda b, h, i, pt: (b * nq + i, h)), rowspec],
        scratch_shapes=[pltpu.VMEM((2 * t, hd), bf16), pltpu.VMEM((1, 2 * t), f32),
                        pltpu.VMEM((1, 2 * t), f32), pltpu.VMEM((hd, 2 * t), f32),
                        pltpu.VMEM((t, 2 * t), f32), pltpu.VMEM((1, 2 * t), f32),
                        pltpu.VMEM((t, 2 * t), f32), pltpu.VMEM((1, 2 * t), f32),
                        pltpu.VMEM((nrow, hd), bf16), pltpu.VMEM((nrow, 1), f32),
                        pltpu.VMEM((nrow, 1), f32), pltpu.VMEM((nrow, hd), f32)],
    )
    ck = cache_k.reshape(-1, hd)
    cv = cache_v.reshape(-1, hd)
    return pl.pallas_call(
        functools.partial(_attn_kernel, t=t, cb=ATT_QUERY_BLOCK, npg=npg, lam_init=lam_init),
        grid_spec=grid_spec,
        out_shape=[jax.ShapeDtypeStruct((batch * seq, nh * hd), bf16), jax.ShapeDtypeStruct((nb, nh, hd), f32)],
        compiler_params=pltpu.CompilerParams(dimension_semantics=("arbitrary",) * 3,
                                             vmem_limit_bytes=ATTN_VMEM_LIMIT_BYTES),
        name="attn",
    )(page_table, lam, far, q, k, vt, tiles, gain.reshape(hd, 1), qs, kn, vn, bias2, b0, gain.reshape(1, hd),
      *([ck] * npg), *([cv] * npg))


def _softmax_rows(s):
    e = jnp.exp(s - jnp.max(s, axis=-1, keepdims=True))
    return e / jnp.sum(e, axis=-1, keepdims=True)


def _cross_prompt_kernel(x_ref, g_ref, wq_ref, mk_ref, mv_ref, wo_ref, o_ref, a_ref):
    x = x_ref[...]
    d = x.shape[1]
    hd = d // MEM_HEADS
    h = _rms(x, g_ref[...]).astype(bf16)
    q = (_dot(h, wq_ref[...]) * (hd ** -0.5)).astype(bf16)
    heads = [slice(hh * hd, (hh + 1) * hd) for hh in range(MEM_HEADS)]
    scores = [_dot_nt(q[:, sl], mk_ref[:, sl]) for sl in heads]
    for sl, s in zip(heads, scores):
        a_ref[:, sl] = _dot(_softmax_rows(s).astype(bf16), mv_ref[:, sl]).astype(bf16)
    o_ref[...] = x + _dot(a_ref[...], wo_ref[...])


def _cross_prompt(x, g, wq, mk, mv, wo, seq):
    m, d = x.shape
    _, wq_spec, wq = _stacked(wq)
    _, wo_spec, wo = _stacked(wo)
    tm = ROW_TILE
    per = seq // tm
    mem = mk.shape[0] // (m // seq)
    return pl.pallas_call(
        _cross_prompt_kernel,
        grid=(m // tm,),
        in_specs=[pl.BlockSpec((tm, d), lambda i: (i, 0)), _resident((1, d)), wq_spec,
                  pl.BlockSpec((mem, d), lambda i: (i // per, 0)), pl.BlockSpec((mem, d), lambda i: (i // per, 0)),
                  wo_spec],
        out_specs=pl.BlockSpec((tm, d), lambda i: (i, 0)),
        out_shape=jax.ShapeDtypeStruct((m, d), f32),
        scratch_shapes=[pltpu.VMEM((tm, d), bf16)],
        compiler_params=_params("arbitrary"),
        name="cross_prompt",
    )(x, g.reshape(1, d), wq, mk, mv, wo)


def _cross_sample_kernel(q_ref, mk_ref, mv_ref, mask_ref, o_ref):
    nh, hd = q_ref.shape[1], q_ref.shape[2]
    q = jnp.concatenate([q_ref[0] * (hd ** -0.5), jnp.zeros((SUBLANES - nh, hd), f32)], axis=0)
    rows = mask_ref.shape[1]
    s = _dot_nt(q.astype(bf16), mk_ref[...].reshape(rows, hd).astype(bf16)) + mask_ref[...]
    o = _dot(_softmax_rows(s).astype(bf16), mv_ref[...].reshape(rows, hd).astype(bf16))
    o_ref[0] = o[0:nh]


def _cross_sample(q, cache_mk, cache_mv, layer):
    nb, nh, hd = q.shape
    rows = cache_mk.shape[2] * nh
    head_of_row = jnp.arange(rows, dtype=i32) % nh
    mask = jnp.where(head_of_row[None, :] == jnp.arange(SUBLANES, dtype=i32)[:, None], 0.0, NEG_INF).astype(f32)
    mask = mask.at[nh:].set(0.0)
    mem_spec = pl.BlockSpec((None, None, rows // nh, nh, hd), lambda b: (layer, b, 0, 0, 0))
    return pl.pallas_call(
        _cross_sample_kernel,
        grid=(nb,),
        in_specs=[pl.BlockSpec((1, nh, hd), lambda b: (b, 0, 0)), mem_spec, mem_spec, _resident((SUBLANES, rows))],
        out_specs=pl.BlockSpec((1, nh, hd), lambda b: (b, 0, 0)),
        out_shape=jax.ShapeDtypeStruct((nb, nh, hd), f32),
        compiler_params=_params("arbitrary"),
        name="cross_sample",
    )(q, cache_mk, cache_mv, mask)


def _rotary_tables(pos, half):
    inv = 1.0 / (10000.0 ** jnp.linspace(0.0, 1.0, half, dtype=f32))
    ang = pos.astype(f32)[:, None] * inv[None, :]
    cos, sin = jnp.cos(ang), jnp.sin(ang)
    return jnp.concatenate([cos, cos], axis=-1), jnp.concatenate([-sin, sin], axis=-1)


def kernel(x_prompt, x_sample, state_ret, state_pool, cache_k_diff, cache_v_diff, cache_mem_k, cache_mem_v, page_table, mem_prompt, norms, final_norm, ffn_w_gate, ffn_w_up, ffn_w_down, w_in_even, ret_gain, pool_w, pool_scale, w_out_even, w_qkv_odd, lambda_q1, lambda_k1, lambda_q2, lambda_k2, subln_gain, w_out_odd, rel_bias, w_cq, w_ckv, w_co):
    batch, seq, d = x_prompt.shape
    nb = x_sample.shape[0]
    depth = norms.shape[0]
    past = page_table.shape[1] * PAGE_SIZE
    mem_len = mem_prompt.shape[1]
    rw = RET_HEADS * RET_DK
    dw = DIFF_HEADS * DIFF_HD

    xp = x_prompt.reshape(batch * seq, d)
    xs = x_sample.reshape(nb, d)
    mem2d = mem_prompt.reshape(batch * mem_len, d)

    cos_p, sin_p = _rotary_tables(jnp.arange(seq, dtype=i32), RET_DK // 2)
    cos_s, sin_s = _rotary_tables(jnp.full((1,), past, dtype=i32), RET_DK // 2)

    wg, wu, wd = ffn_w_gate.astype(bf16), ffn_w_up.astype(bf16), ffn_w_down.astype(bf16)
    w_in, w_oe = w_in_even.astype(bf16), w_out_even.astype(bf16)
    w_qkv, w_oo = w_qkv_odd.astype(bf16), w_out_odd.astype(bf16)
    wcq, wckv, wco = w_cq.astype(bf16), w_ckv.astype(bf16), w_co.astype(bf16)
    pw = pool_w.astype(bf16)

    p_ret, p_pool, p_k, p_v, p_mk, p_mv = [], [], [], [], [], []
    s_ret, s_pool, s_k, s_v = [], [], [], []
    for l in range(depth):
        last = l == depth - 1
        ffn_w = lambda i: ((wg, (l, i)), (wu, (l, i)), (wd, (l, i)))
        xp = _ffn(xp, norms[l, 0], *ffn_w(0))
        xs = _ffn(xs, norms[l, 0], *ffn_w(0))
        if l % 2 == 0:
            e = l // 2
            zw = w_in.shape[2]
            (zp,) = _proj(xp, norms[l, 1], (w_in, (e,)), [(0, zw, [(0, 1.0)])], [(zw, f32)])
            (zs,) = _proj(xs, norms[l, 1], (w_in, (e,)), [(0, zw, [(0, 1.0)])], [(zw, f32)])
            xp, rp, pp = _even_prompt(xp, zp, cos_p, sin_p, ret_gain[e], pw[e], pool_scale[e], (w_oe, (e,)),
                                      batch, seq)
            mixs, rs, ps = _even_sample(zs, cos_s, sin_s, state_ret[e:e + 1], state_pool[e:e + 1], ret_gain[e],
                                        pool_w[e], pool_scale[e], past)
            xs = _mmres(xs, mixs.reshape(nb, rw + pool_scale.shape[1]), (w_oe, (e,)))
            p_ret.append(rp[0])
            p_pool.append(pp[0])
            s_ret.append(rs[0])
            s_pool.append(ps[0])
        else:
            o = l // 2
            lam_init = 0.8 - 0.6 * math.exp(-0.3 * l)
            scale = DIFF_D ** -0.5
            tiles, tab, lam_t = _bias_tables(rel_bias, lambda_q1[o], lambda_k1[o], lambda_q2[o], lambda_k2[o],
                                             lam_init, ATT_TILE)
            lam = lam_t[0, 0:1]
            tab = tab[:, 0, :]
            far = tab[:, REL_MAX_DIST]
            qkv_groups = [(0, dw, [(0, scale * LOG2E)]), (dw, dw, [(1, 1.0), (3, 1.0)]),
                          (2 * dw, dw, [(2, 1.0), (4, 1.0)])]
            qb, kp, vp, kb, vtb = _proj(xp, norms[l, 1], (w_qkv, (o,)), qkv_groups,
                                        [(dw, bf16), (dw, f32), (dw, f32), (dw, bf16), (dw, bf16, "T")], seq=seq)
            qs, ks_, vs = _proj(xs, norms[l, 1], (w_qkv, (o,)),
                                [(0, dw, [(0, scale)]), (dw, dw, [(1, 1.0)]), (2 * dw, dw, [(2, 1.0)])],
                                [(dw, f32), (dw, f32), (dw, f32)])
            nh = DIFF_HEADS
            pages_per_step = page_table.shape[1] // (seq // ATT_TILE)
            own = jnp.arange(nh)[:, None, None] == jnp.arange(nh)[None, None, :]
            far_page = jnp.where(own, far[:, None, None], NEG_INF) + jnp.zeros((1, PAGE_SIZE, 1), f32)
            last_page = jnp.where(own, tab[:, PAGE_SIZE:0:-1][:, :, None], NEG_INF)
            far_page, last_page = far_page.reshape(nh, -1), last_page.reshape(nh, -1)
            far_step = jnp.tile(far_page, (2, pages_per_step))
            last_step = jnp.concatenate([far_step[:, :-PAGE_SIZE * nh], jnp.tile(last_page, (2, 1))], axis=1)
            bias2 = jnp.stack([far_step, last_step])
            b0 = jnp.tile(tab[:, 0], 2)[:, None]
            heads = lambda a: a.reshape(nb, nh, DIFF_HD)
            attp, atts = _attn(qb, kb, vtb, tiles, far, lam, subln_gain[o], batch, seq,
                               heads(qs), heads(ks_), heads(vs), cache_k_diff, cache_v_diff, o, page_table,
                               bias2, b0, lam_init)
            xp = _mmres(xp, attp, (w_oo, (o,)))
            xs = _mmres(xs, atts.reshape(nb, dw), (w_oo, (o,)))
            p_k.append(kp.reshape(batch, seq, DIFF_HEADS, DIFF_HD))
            p_v.append(vp.reshape(batch, seq, DIFF_HEADS, DIFF_HD))
            s_k.append(ks_.reshape(nb, 1, DIFF_HEADS, DIFF_HD))
            s_v.append(vs.reshape(nb, 1, DIFF_HEADS, DIFF_HD))
        mk, mv, mkb, mvb = _proj(mem2d, None, (wckv, (l,)),
                                 [(0, d, [(0, 1.0), (2, 1.0)]), (d, d, [(1, 1.0), (3, 1.0)])],
                                 [(d, f32), (d, f32), (d, bf16), (d, bf16)])
        p_mk.append(mk.reshape(batch, mem_len, MEM_HEADS, d // MEM_HEADS))
        p_mv.append(mv.reshape(batch, mem_len, MEM_HEADS, d // MEM_HEADS))
        xp = _cross_prompt(xp, norms[l, 2], (wcq, (l,)), mkb, mvb, (wco, (l,)), seq)
        (qcs,) = _proj(xs, norms[l, 2], (wcq, (l,)), [(0, d, [(0, 1.0)])], [(d, f32)])
        acs = _cross_sample(qcs.reshape(nb, MEM_HEADS, d // MEM_HEADS), cache_mem_k, cache_mem_v, l)
        xs = _mmres(xs, acs.reshape(nb, d), (wco, (l,)))
        fin = final_norm if last else None
        xp = _ffn(xp, norms[l, 3], *ffn_w(1), fin)
        xs = _ffn(xs, norms[l, 3], *ffn_w(1), fin)

    return (xp.reshape(batch, seq, d), xs.reshape(nb, 1, d),
            jnp.stack(p_ret), jnp.stack(p_pool), jnp.stack(p_k), jnp.stack(p_v), jnp.stack(p_mk), jnp.stack(p_mv),
            jnp.stack(s_ret), jnp.stack(s_pool), jnp.stack(s_k), jnp.stack(s_v))
```

```python
import functools
import math

import jax
import jax.numpy as jnp
from jax import lax
from jax.experimental import pallas as pl
from jax.experimental.pallas import tpu as pltpu

f32 = jnp.float32
bf16 = jnp.bfloat16
i32 = jnp.int32

EPS = 1e-6
NEG_INF = -1e30
LOG2E = math.log2(math.e)

RET_HEADS = 4
RET_DK = 128
POOL_WINDOWS = (2, 4, 8, 16)
POOL_KEEP = max(POOL_WINDOWS) - 1
DIFF_HEADS = 8
DIFF_D = 64
DIFF_HD = 2 * DIFF_D
REL_BUCKETS = 32
REL_MAX_DIST = 128
MEM_HEADS = 4
PAGE_SIZE = 128

VMEM_LIMIT_BYTES = 56 * 1024 * 1024
LANES = 128
SUBLANES = 8
MXU_TILE = 256

ROW_TILE = 512
ATT_TILE = 512
ATT_QUERY_BLOCK = 512
ATTN_VMEM_LIMIT_BYTES = 60 * 1024 * 1024


def _params(*sem):
    return pltpu.CompilerParams(dimension_semantics=sem, vmem_limit_bytes=VMEM_LIMIT_BYTES)


def _rms(x, g):
    return x * lax.rsqrt(jnp.mean(x * x, axis=-1, keepdims=True) + EPS) * g


def _dot(a, b):
    return jnp.dot(a, b, preferred_element_type=f32)


def _dot_nt(a, b):
    return lax.dot_general(a, b, (((1,), (1,)), ((), ())), preferred_element_type=f32)


def _dot_tn(a, b):
    return lax.dot_general(a, b, (((0,), (0,)), ((), ())), preferred_element_type=f32)


def _resident(shape, lead=()):
    nd = len(shape)
    return pl.BlockSpec((None,) * len(lead) + tuple(shape), lambda *_: tuple(lead) + (0,) * nd,
                        pipeline_mode=pl.Buffered(1))


def _stacked(w):
    arr, lead = w
    shape = arr.shape[len(lead):]
    return shape, _resident(shape, lead), arr


def _ffn_kernel(x_ref, g_ref, wg_ref, wu_ref, wd_ref, *rest, bounds, final):
    o_ref = rest[-1]
    x = x_ref[...]
    h = _rms(x, g_ref[...]).astype(bf16)
    acc = jnp.zeros(x.shape, f32)
    for c0, c1 in zip(bounds[:-1], bounds[1:]):
        gate = _dot(h, wg_ref[:, c0:c1])
        up = _dot(h, wu_ref[:, c0:c1])
        a = (jax.nn.silu(gate) * up).astype(bf16)
        acc = acc + _dot(a, wd_ref[c0:c1, :])
    y = x + 0.5 * acc
    if final:
        y = _rms(y, rest[0][...])
    o_ref[...] = y


def _ffn(x, g, wg, wu, wd, final_g=None):
    m, d = x.shape
    (_, f), wg_spec, wg = _stacked(wg)
    _, wu_spec, wu = _stacked(wu)
    _, wd_spec, wd = _stacked(wd)
    tm = min(ROW_TILE, m)
    split = -(-f // (2 * MXU_TILE)) * MXU_TILE
    bounds = (0, split, f) if split < f else (0, f)
    final = final_g is not None
    in_specs = [pl.BlockSpec((tm, d), lambda i: (i, 0)), _resident((1, d)), wg_spec, wu_spec, wd_spec]
    args = [x, g.reshape(1, d), wg, wu, wd]
    if final:
        in_specs.append(_resident((1, d)))
        args.append(final_g.reshape(1, d))
    return pl.pallas_call(
        functools.partial(_ffn_kernel, bounds=bounds, final=final),
        grid=(m // tm,),
        in_specs=in_specs,
        out_specs=pl.BlockSpec((tm, d), lambda i: (i, 0)),
        out_shape=jax.ShapeDtypeStruct((m, d), f32),
        compiler_params=_params("arbitrary"),
        name="ffn",
    )(*args)


def _proj_kernel(*refs, norm, groups, transposed, nchunk):
    x_ref = refs[0]
    g_ref = refs[1] if norm else None
    w_ref = refs[2] if norm else refs[1]
    o_refs = refs[(3 if norm else 2):]
    x = x_ref[...]
    h = (_rms(x, g_ref[...]) if norm else x).astype(bf16)
    for off, width, targets in groups:
        for c0 in range(0, width, nchunk):
            n = min(nchunk, width - c0)
            r = _dot(h, w_ref[:, off + c0:off + c0 + n])
            for oi, scale in targets:
                o_ref = o_refs[oi]
                val = (r if scale == 1.0 else r * scale)
                if transposed[oi]:
                    o_ref[c0:c0 + n, :] = val.T.astype(o_ref.dtype)
                else:
                    o_ref[:, c0:c0 + n] = val.astype(o_ref.dtype)


def _proj(x, g, w, groups, outs, seq=None):
    m, d = x.shape
    _, w_spec, w = _stacked(w)
    tm = min(ROW_TILE, m)
    norm = g is not None
    in_specs = [pl.BlockSpec((tm, d), lambda i: (i, 0))]
    args = [x]
    if norm:
        in_specs.append(_resident((1, d)))
        args.append(g.reshape(1, d))
    in_specs.append(w_spec)
    args.append(w)
    out_specs, out_shape = [], []
    for wd, dt, *flag in outs:
        if flag:
            per = seq // tm
            out_specs.append(pl.BlockSpec((wd, tm), lambda i: (i // per, i % per)))
            out_shape.append(jax.ShapeDtypeStruct((m // seq * wd, seq), dt))
        else:
            out_specs.append(pl.BlockSpec((tm, wd), lambda i: (i, 0)))
            out_shape.append(jax.ShapeDtypeStruct((m, wd), dt))
    return pl.pallas_call(
        functools.partial(_proj_kernel, norm=norm, groups=groups, transposed=tuple(o[2:] == ("T",) for o in outs),
                          nchunk=512),
        grid=(m // tm,),
        in_specs=in_specs,
        out_specs=out_specs,
        out_shape=out_shape,
        compiler_params=_params("arbitrary"),
        name="proj",
    )(*args)


def _mmres_kernel(x_ref, a_ref, w_ref, o_ref):
    o_ref[...] = x_ref[...] + _dot(a_ref[...].astype(bf16), w_ref[...])


def _mmres(x, a, w):
    m, d = x.shape
    k = a.shape[1]
    _, w_spec, w = _stacked(w)
    tm = min(ROW_TILE, m)
    return pl.pallas_call(
        _mmres_kernel,
        grid=(m // tm,),
        in_specs=[pl.BlockSpec((tm, d), lambda i: (i, 0)), pl.BlockSpec((tm, k), lambda i: (i, 0)), w_spec],
        out_specs=pl.BlockSpec((tm, d), lambda i: (i, 0)),
        out_shape=jax.ShapeDtypeStruct((m, d), f32),
        compiler_params=_params("arbitrary"),
        name="mmres",
    )(x, a, w)


def _ret_log_gamma(h):
    return math.log1p(-(2.0 ** (-5.0 - h)))


def _rotate(x, cosf, sinf):
    return x * cosf + pltpu.roll(x, x.shape[-1] // 2, axis=1) * sinf


def _group_norm_gate(o, gain, gate):
    mu = jnp.mean(o, axis=-1, keepdims=True)
    var = jnp.mean(jnp.square(o - mu), axis=-1, keepdims=True)
    return jax.nn.silu(gate) * ((o - mu) * lax.rsqrt(var + EPS) * gain)


def _even_prompt_kernel(x_ref, zq_ref, zk_ref, zv_ref, zg_ref, zp_ref, cos_ref, sin_ref, gain_ref, pw_ref, ps_ref,
                        wo_ref, xo_ref, sret_ref, spool_ref, state_ref, ext_ref, intra_ref, mix_ref, *, t):
    b = pl.program_id(0)
    c = pl.program_id(1)
    nc = pl.num_programs(1)
    halo = POOL_KEEP + 1
    dk = RET_DK

    @pl.when(jnp.logical_and(b == 0, c == 0))
    def _():
        rel = lax.broadcasted_iota(i32, (t, t), 0) - lax.broadcasted_iota(i32, (t, t), 1)
        relf = jnp.maximum(rel, 0).astype(f32)
        for h in range(RET_HEADS):
            intra_ref[h] = jnp.where(rel >= 0, jnp.exp(_ret_log_gamma(h) * relf), 0.0)

    @pl.when(c == 0)
    def _():
        state_ref[...] = jnp.zeros(state_ref.shape, f32)
        ext_ref[0:halo, :] = jnp.zeros((halo, ext_ref.shape[1]), f32)

    cosf = cos_ref[...]
    sinf = sin_ref[...]
    row = lax.broadcasted_iota(i32, (t, 1), 0).astype(f32)
    heads = []
    for h in range(RET_HEADS):
        sl = slice(h * dk, (h + 1) * dk)
        qb = _rotate(zq_ref[:, sl], cosf, sinf).astype(bf16)
        kr = _rotate(zk_ref[:, sl], cosf, sinf) * (dk ** -0.5)
        heads.append((sl, qb, kr, _dot_nt(qb, kr.astype(bf16))))
    for h, (sl, qb, kr, qk) in enumerate(heads):
        lg = _ret_log_gamma(h)
        vb = zv_ref[:, sl].astype(bf16)
        att = qk * intra_ref[h]
        s_old = state_ref[h]
        o = _dot(att.astype(bf16), vb) + _dot(qb, s_old.astype(bf16)) * jnp.exp(lg * (row + 1.0))
        k_dec = jnp.exp(lg * (float(t - 1) - row))
        state_ref[h] = s_old * math.exp(lg * t) + _dot_tn((kr * k_dec).astype(bf16), vb)
        mix_ref[:, sl] = _group_norm_gate(o, gain_ref[:, sl], zg_ref[:, sl]).astype(mix_ref.dtype)

    p = zp_ref[...]
    ext_ref[halo:halo + t, :] = p
    pos = c * t + lax.broadcasted_iota(i32, (t, 1), 0)
    gw = p.shape[1] // len(POOL_WINDOWS)
    ret_w = RET_HEADS * dk
    for gi, w in enumerate(POOL_WINDOWS):
        sl = slice(gi * gw, (gi + 1) * gw)
        win = p[:, sl]
        for back in range(1, w):
            win = win + ext_ref[halo - back:halo - back + t, sl]
        cnt = jnp.minimum(w, pos + 1).astype(f32)
        pooled = win / cnt - p[:, sl]
        mixed = _dot(pooled.astype(bf16), pw_ref[gi]) * ps_ref[:, sl]
        mix_ref[:, ret_w + gi * gw:ret_w + (gi + 1) * gw] = mixed.astype(mix_ref.dtype)

    xo_ref[...] = x_ref[...] + _dot(mix_ref[...], wo_ref[...])

    @pl.when(c == nc - 1)
    def _():
        sret_ref[0, 0] = state_ref[...]
        spool_ref[0, 0] = ext_ref[t + 1:t + halo, :]

    ext_ref[0:halo, :] = ext_ref[t:t + halo, :]


def _even_prompt(x, z, cosf, sinf, gain, pw, ps, wo, batch, seq):
    t = ROW_TILE
    nc = seq // t
    d = x.shape[1]
    rw = RET_HEADS * RET_DK
    pwid = ps.shape[0]
    _, wo_spec, wo = _stacked(wo)
    row = lambda b, c: (b * nc + c, 0)
    zspec = lambda j: pl.BlockSpec((t, rw), lambda b, c: (b * nc + c, j))
    return pl.pallas_call(
        functools.partial(_even_prompt_kernel, t=t),
        grid=(batch, nc),
        in_specs=[pl.BlockSpec((t, d), row), zspec(0), zspec(1), zspec(2), zspec(3), zspec(4),
                  pl.BlockSpec((t, RET_DK), lambda b, c: (c, 0)), pl.BlockSpec((t, RET_DK), lambda b, c: (c, 0)),
                  _resident((1, rw)), _resident(pw.shape), _resident((1, pwid)), wo_spec],
        out_specs=[pl.BlockSpec((t, d), row),
                   pl.BlockSpec((1, 1, RET_HEADS, RET_DK, RET_DK), lambda b, c: (0, b, 0, 0, 0)),
                   pl.BlockSpec((1, 1, POOL_KEEP, pwid), lambda b, c: (0, b, 0, 0))],
        out_shape=[jax.ShapeDtypeStruct((batch * seq, d), f32),
                   jax.ShapeDtypeStruct((1, batch, RET_HEADS, RET_DK, RET_DK), f32),
                   jax.ShapeDtypeStruct((1, batch, POOL_KEEP, pwid), f32)],
        scratch_shapes=[pltpu.VMEM((RET_HEADS, RET_DK, RET_DK), f32),
                        pltpu.VMEM((POOL_KEEP + 1 + t, pwid), f32),
                        pltpu.VMEM((RET_HEADS, t, t), f32),
                        pltpu.VMEM((t, rw + pwid), bf16)],
        compiler_params=_params("arbitrary", "arbitrary"),
        name="even_prompt",
    )(x, z, z, z, z, z, cosf, sinf, gain.reshape(1, rw), pw, ps.reshape(1, pwid), wo)


def _even_sample_kernel(z_ref, cos_ref, sin_ref, s0_ref, rows_ref, gain_ref, pw_ref, ps_ref,
                        mix_ref, sret_ref, spool_ref, ext_ref, *, pos):
    dk = RET_DK
    rw = RET_HEADS * dk
    z = jnp.broadcast_to(z_ref[0], (SUBLANES, z_ref.shape[2]))
    cosf = jnp.broadcast_to(cos_ref[...], (SUBLANES, dk))
    sinf = jnp.broadcast_to(sin_ref[...], (SUBLANES, dk))
    first = lax.broadcasted_iota(i32, (SUBLANES, dk), 0) == 0
    for h in range(RET_HEADS):
        sl = slice(h * dk, (h + 1) * dk)
        gamma = math.exp(_ret_log_gamma(h))
        qr = _rotate(z[:, sl], cosf, sinf)
        kr = _rotate(z[:, rw + h * dk:rw + (h + 1) * dk], cosf, sinf) * (dk ** -0.5)
        v = z[:, 2 * rw + h * dk:2 * rw + (h + 1) * dk]
        s_new = s0_ref[0, 0, h] * gamma + _dot_tn(jnp.where(first, kr, 0.0), v)
        sret_ref[0, 0, h] = s_new
        o = _dot(qr, s_new)
        gate = z[:, 3 * rw + h * dk:3 * rw + (h + 1) * dk]
        mix_ref[0, :, sl] = _group_norm_gate(o, gain_ref[:, sl], gate)[0:1]

    keep = POOL_KEEP
    pwid = ps_ref.shape[1]
    p = z[0:1, 4 * rw:4 * rw + pwid]
    ext_ref[0:keep, :] = rows_ref[0, 0]
    ext_ref[keep:keep + 1, :] = p
    spool_ref[0, 0] = ext_ref[1:keep + 1, :]
    gw = pwid // len(POOL_WINDOWS)
    for gi, w in enumerate(POOL_WINDOWS):
        sl = slice(gi * gw, (gi + 1) * gw)
        win = jnp.sum(ext_ref[keep + 1 - w:keep + 1, sl], axis=0, keepdims=True)
        pooled = win / float(min(w, pos + 1)) - p[:, sl]
        mixed = _dot(jnp.broadcast_to(pooled, (SUBLANES, gw)), pw_ref[gi])[0:1] * ps_ref[:, sl]
        mix_ref[0, :, rw + gi * gw:rw + (gi + 1) * gw] = mixed


def _even_sample(z, cosf, sinf, state_ret, state_pool, gain, pw, ps, pos):
    nb, zw = z.shape
    rw = RET_HEADS * RET_DK
    pwid = ps.shape[0]
    return pl.pallas_call(
        functools.partial(_even_sample_kernel, pos=pos),
        grid=(nb,),
        in_specs=[pl.BlockSpec((1, 1, zw), lambda b: (b, 0, 0)),
                  _resident((1, RET_DK)), _resident((1, RET_DK)),
                  pl.BlockSpec((1, 1, RET_HEADS, RET_DK, RET_DK), lambda b: (0, b, 0, 0, 0)),
                  pl.BlockSpec((1, 1, POOL_KEEP, pwid), lambda b: (0, b, 0, 0)),
                  _resident((1, rw)), _resident(pw.shape), _resident((1, pwid))],
        out_specs=[pl.BlockSpec((1, 1, rw + pwid), lambda b: (b, 0, 0)),
                   pl.BlockSpec((1, 1, RET_HEADS, RET_DK, RET_DK), lambda b: (0, b, 0, 0, 0)),
                   pl.BlockSpec((1, 1, POOL_KEEP, pwid), lambda b: (0, b, 0, 0))],
        out_shape=[jax.ShapeDtypeStruct((nb, 1, rw + pwid), f32),
                   jax.ShapeDtypeStruct((1, nb, RET_HEADS, RET_DK, RET_DK), f32),
                   jax.ShapeDtypeStruct((1, nb, POOL_KEEP, pwid), f32)],
        scratch_shapes=[pltpu.VMEM((POOL_KEEP + 1, pwid), f32)],
        compiler_params=_params("arbitrary"),
        name="even_sample",
    )(z.reshape(nb, 1, zw), cosf, sinf, state_ret, state_pool, gain.reshape(1, rw), pw, ps.reshape(1, pwid))


def _rel_bias_of(dist, rb_ref, h):
    n = jnp.maximum(dist, 0)
    max_exact = REL_BUCKETS // 2
    nf = jnp.maximum(n, 1).astype(f32)
    large = max_exact + (jnp.log(nf / max_exact) / math.log(REL_MAX_DIST / max_exact)
                         * (REL_BUCKETS - max_exact)).astype(i32)
    large = jnp.minimum(large, REL_BUCKETS - 1)
    bucket = jnp.where(n < max_exact, n, large)
    out = jnp.zeros(dist.shape, f32)
    for bk in range(REL_BUCKETS):
        out = jnp.where(bucket == bk, rb_ref[bk, h], out)
    return out


def _bias_kernel(rb_ref, lq1_ref, lk1_ref, lq2_ref, lk2_ref, tile_ref, tab_ref, lam_ref, *, t, lam_init):
    h = pl.program_id(0)
    nd = REL_MAX_DIST
    rel = lax.broadcasted_iota(i32, (nd, nd), 1) - lax.broadcasted_iota(i32, (nd, nd), 0)
    on_diag = jnp.where(rel >= 0, _rel_bias_of(rel, rb_ref, h) * LOG2E, NEG_INF)
    next_diag = _rel_bias_of(rel + nd, rb_ref, h) * LOG2E
    far = _rel_bias_of(jnp.full((nd, nd), nd, i32), rb_ref, h) * LOG2E
    masked = jnp.full((nd, nd), NEG_INF, f32)
    nblk = t // nd
    for bk in range(nblk):
        for bq in range(nblk):
            diag_blk = on_diag if bq == bk else next_diag if bq == bk + 1 else far if bq > bk else masked
            below_blk = next_diag if (bk == nblk - 1 and bq == 0) else far
            tile_ref[0, 0, bk * nd:(bk + 1) * nd, bq * nd:(bq + 1) * nd] = diag_blk
            tile_ref[0, 1, bk * nd:(bk + 1) * nd, bq * nd:(bq + 1) * nd] = below_blk
    tab_ref[0] = _rel_bias_of(lax.broadcasted_iota(i32, (1, 2 * REL_MAX_DIST), 1), rb_ref, h)
    lam = (jnp.exp(jnp.sum(lq1_ref[...] * lk1_ref[...], axis=-1, keepdims=True))
           - jnp.exp(jnp.sum(lq2_ref[...] * lk2_ref[...], axis=-1, keepdims=True)) + lam_init)
    lam_ref[...] = jnp.broadcast_to(lam, lam_ref.shape)


def _bias_tables(rel_bias, lq1, lk1, lq2, lk2, lam_init, t):
    nh = rel_bias.shape[1]
    d = lq1.shape[0]
    vec = lambda a: a.reshape(1, d)
    return pl.pallas_call(
        functools.partial(_bias_kernel, t=t, lam_init=lam_init),
        grid=(nh,),
        in_specs=[pl.BlockSpec(memory_space=pltpu.SMEM)] + [_resident((1, d))] * 4,
        out_specs=[pl.BlockSpec((1, 2, t, t), lambda h: (h, 0, 0, 0)),
                   pl.BlockSpec((1, 1, 2 * REL_MAX_DIST), lambda h: (h, 0, 0)),
                   pl.BlockSpec((SUBLANES, LANES), lambda h: (0, 0))],
        out_shape=[jax.ShapeDtypeStruct((nh, 2, t, t), f32),
                   jax.ShapeDtypeStruct((nh, 1, 2 * REL_MAX_DIST), f32),
                   jax.ShapeDtypeStruct((SUBLANES, LANES), f32)],
        compiler_params=_params("arbitrary"),
        name="bias_tables",
    )(rel_bias, vec(lq1), vec(lk1), vec(lq2), vec(lk2))


def _sub_ln(o, gain, lam_init):
    return o * lax.rsqrt(jnp.mean(o * o, axis=-1, keepdims=True) + EPS) * gain * (1.0 - lam_init)


def _attn_prompt_tile(lam_ref, far_ref, q_ref, k_ref, vt_ref, tile_ref, gain_ref, o_ref,
                      qq_ref, m_ref, l_ref, acc_ref, s0_ref, smax0_ref, s1_ref, smax1_ref, *, t, cb, lam_init,
                      beside_own_tile):
    h = pl.program_id(1)
    qi = pl.program_id(2)
    q = q_ref[...]
    lane = lax.broadcasted_iota(i32, q.shape, 1)
    zero = jnp.zeros_like(q)
    qq_ref[0:t, :] = jnp.where(lane < DIFF_D, q, zero)
    qq_ref[t:2 * t, :] = jnp.where(lane >= DIFF_D, q, zero)
    m_ref[...] = jnp.full(m_ref.shape, NEG_INF, f32)
    l_ref[...] = jnp.zeros(l_ref.shape, f32)
    acc_ref[...] = jnp.zeros(acc_ref.shape, f32)

    far = far_ref[h] * LOG2E

    blocks = [slice(c0, c0 + cb) for c0 in range(0, 2 * t, cb)]

    parked = ((s0_ref, smax0_ref), (s1_ref, smax1_ref))

    def scores(ki, slot):
        s_ref, smax_ref = parked[slot]
        kt = k_ref[pl.ds(pl.multiple_of(ki * t, t), t), :]
        for cols in blocks:
            s = _dot_nt(kt, qq_ref[cols, :])
            s_ref[:, cols] = s
            smax_ref[:, cols] = jnp.max(s, axis=0, keepdims=True)

    def scores_own(ki, slot):
        s_ref, _ = parked[slot]
        k0 = pl.multiple_of(ki * t, t)
        for c0 in (0, t):
            for q0, nk in ((0, t // 2), (t // 2, t)):
                cols = slice(c0 + q0, c0 + q0 + t // 2)
                s_ref[0:nk, cols] = _dot_nt(k_ref[pl.ds(k0, nk), :], qq_ref[cols, :])

    def update(cols, m_old, m_new, p, vt):
        alpha = jnp.exp2(m_old - m_new)
        l_ref[:, cols] = alpha * l_ref[:, cols] + jnp.sum(p, axis=0, keepdims=True)
        acc_ref[:, cols] = alpha * acc_ref[:, cols] + _dot(vt, p.astype(bf16))
        m_ref[:, cols] = m_new

    def absorb_far(ki, slot):
        s_ref, smax_ref = parked[slot]
        vt = vt_ref[:, pl.ds(pl.multiple_of(ki * t, t), t)]
        for cols in blocks:
            m_old = m_ref[:, cols]
            m_new = jnp.maximum(m_old, smax_ref[:, cols] + far)
            update(cols, m_old, m_new, jnp.exp2(s_ref[:, cols] - (m_new - far)), vt)

    def absorb_behind(ki, slot):
        s_ref, smax_ref = parked[slot]
        vt = vt_ref[:, pl.ds(pl.multiple_of(ki * t, t), t)]
        corner = tile_ref[0, 1, t - nd:t, 0:nd]
        for c0 in (0, t):
            cols, near, rest = slice(c0, c0 + t), slice(c0, c0 + nd), slice(c0 + nd, c0 + t)
            m_old = m_ref[:, cols]
            s_near = jnp.concatenate([s_ref[0:t - nd, near] + far, s_ref[t - nd:t, near] + corner], axis=0)
            m_near = jnp.maximum(m_old[:, 0:nd], jnp.max(s_near, axis=0, keepdims=True))
            m_rest = jnp.maximum(m_old[:, nd:t], smax_ref[:, rest] + far)
            p = jnp.concatenate([jnp.exp2(s_near - m_near), jnp.exp2(s_ref[:, rest] - (m_rest - far))], axis=1)
            update(cols, m_old, jnp.concatenate([m_near, m_rest], axis=1), p, vt)

    def absorb_own(ki, slot):
        s_ref, _ = parked[slot]
        k0 = pl.multiple_of(ki * t, t)
        for c0 in (0, t):
            for q0, nk in ((0, t // 2), (t // 2, t)):
                cols = slice(c0 + q0, c0 + q0 + t // 2)
                m_old = m_ref[:, cols]
                s = s_ref[0:nk, cols] + tile_ref[0, 0, 0:nk, q0:q0 + t // 2]
                m_new = jnp.maximum(m_old, jnp.max(s, axis=0, keepdims=True))
                update(cols, m_old, m_new, jnp.exp2(s - m_new), vt_ref[:, pl.ds(k0, nk)])

    nd = REL_MAX_DIST
    n_far = jnp.maximum(qi - 1, 0)
    scores(0, 0)

    def far_pair(j, carry):
        for slot in (0, 1):
            scores(2 * j + slot + 1, 1 - slot)
            absorb_far(2 * j + slot, slot)
        return carry

    lax.fori_loop(0, n_far // 2, far_pair, 0)

    @pl.when(n_far % 2 == 1)
    def _():
        scores(n_far, 1)
        absorb_far(n_far - 1, 0)

    for slot in (0, 1):
        @pl.when(jnp.logical_and(qi >= 1, n_far % 2 == slot))
        def _():
            scores_own(qi, 1 - slot)
            absorb_behind(qi - 1, slot)

    for slot in (0, 1):
        @pl.when(qi % 2 == slot)
        def _():
            beside_own_tile()
            absorb_own(qi, slot)

    o = acc_ref[...] / l_ref[...]
    o = o[:, 0:t] - lam_ref[0] * o[:, t:2 * t]
    y = o * lax.rsqrt(jnp.mean(o * o, axis=0, keepdims=True) + EPS) * gain_ref[...] * (1.0 - lam_init)
    o_ref[...] = y.T.astype(o_ref.dtype)


def _attn_decode_parts(lam_ref, q_ref, kn_ref, vn_ref, bias_ref, b0_ref, gain_ref, k_refs, v_refs, o_ref,
                       qh_ref, m_ref, l_ref, acc_ref, *, chunk_axis, lam_init):
    c = pl.program_id(chunk_axis)
    nc = pl.num_programs(chunk_axis)
    nh = DIFF_HEADS
    prow = PAGE_SIZE * nh

    def both_streams(x):
        return jnp.concatenate([x, x], axis=0)

    def start():
        @pl.when(c == 0)
        def _():
            q = q_ref[0]
            lane = lax.broadcasted_iota(i32, q.shape, 1)
            qh = jnp.concatenate([jnp.where(lane < DIFF_D, q, 0.0), jnp.where(lane >= DIFF_D, q, 0.0)],
                                 axis=0).astype(bf16)
            qh_ref[...] = qh
            kn = both_streams(kn_ref[0]).astype(bf16).astype(f32)
            m_ref[...] = jnp.sum(qh.astype(f32) * kn, axis=-1, keepdims=True) + b0_ref[...]
            l_ref[...] = jnp.ones(l_ref.shape, f32)
            acc_ref[...] = both_streams(vn_ref[0]).astype(bf16).astype(f32)

    def absorb_pages():
        qh = qh_ref[...]
        s = jnp.concatenate([_dot_nt(qh, k_ref[...].astype(bf16)) for k_ref in k_refs], axis=1)
        s = s + bias_ref[(c == nc - 1).astype(i32)]
        m_old = m_ref[...]
        m_new = jnp.maximum(m_old, jnp.max(s, axis=-1, keepdims=True))
        alpha = jnp.exp(m_old - m_new)
        p = jnp.exp(s - m_new)
        l_ref[...] = alpha * l_ref[...] + jnp.sum(p, axis=-1, keepdims=True)
        pv = alpha * acc_ref[...]
        for i, v_ref in enumerate(v_refs):
            pv = pv + _dot(p[:, i * prow:(i + 1) * prow].astype(bf16), v_ref[...].astype(bf16))
        acc_ref[...] = pv
        m_ref[...] = m_new

    def finish():
        @pl.when(c == nc - 1)
        def _():
            on = acc_ref[...] / l_ref[...]
            o_ref[0] = _sub_ln(on[0:nh] - lam_ref[0] * on[nh:2 * nh], gain_ref[...], lam_init)

    return start, absorb_pages, finish


N_PROMPT_SCRATCH = 8


def _attn_kernel(pt_ref, lam_ref, far_ref, q_ref, k_ref, vt_ref, tile_ref, gain_col_ref,
                 qs_ref, kn_ref, vn_ref, bias_ref, b0_ref, gain_row_ref, *rest, t, cb, npg, lam_init):
    k_pages, v_pages = rest[:npg], rest[npg:2 * npg]
    o_ref, os_ref = rest[2 * npg:2 * npg + 2]
    scratch = rest[2 * npg + 2:]
    del pt_ref
    start, absorb_pages, finish = _attn_decode_parts(
        lam_ref, qs_ref, kn_ref, vn_ref, bias_ref, b0_ref, gain_row_ref, k_pages, v_pages, os_ref,
        *scratch[N_PROMPT_SCRATCH:], chunk_axis=2, lam_init=lam_init)
    start()
    _attn_prompt_tile(lam_ref, far_ref, q_ref, k_ref, vt_ref, tile_ref, gain_col_ref, o_ref,
                      *scratch[:N_PROMPT_SCRATCH], t=t, cb=cb, lam_init=lam_init, beside_own_tile=absorb_pages)
    finish()


def _attn(q, k, vt, tiles, far, lam, gain, batch, seq, qs, kn, vn, cache_k, cache_v, layer, page_table, bias2, b0,
          lam_init):
    t = ATT_TILE
    nq = seq // t
    hd = DIFF_HD
    nh = DIFF_HEADS
    nb = qs.shape[0]
    pool = cache_k.shape[1]
    npg = page_table.shape[1] // nq
    assert nb == batch * nh and npg * nq == page_table.shape[1], "one sample sequence per prompt (batch, head)"
    nrow = 2 * nh
    prow = PAGE_SIZE * nh
    smem = pl.BlockSpec(memory_space=pltpu.SMEM)
    page_spec = lambda j: pl.BlockSpec(
        (prow, hd), lambda b, h, i, pt: (layer * pool + pt[b * nh + h, i * npg + j], 0))
    rowspec = pl.BlockSpec((1, nh, hd), lambda b, h, i, pt: (b * nh + h, 0, 0))
    const = lambda shape: pl.BlockSpec(shape, lambda b, h, i, pt: (0,) * len(shape), pipeline_mode=pl.Buffered(1))
    grid_spec = pltpu.PrefetchScalarGridSpec(
        num_scalar_prefetch=1,
        grid=(batch, nh, nq),
        in_specs=[smem, smem,
                  pl.BlockSpec((t, hd), lambda b, h, i, pt: (b * nq + i, h)),
                  pl.BlockSpec((seq, hd), lambda b, h, i, pt: (b, h)),
                  pl.BlockSpec((hd, seq), lambda b, h, i, pt: (b * nh + h, 0)),
                  pl.BlockSpec((1, 2, t, t), lambda b, h, i, pt: (h, 0, 0, 0)),
                  const((hd, 1)),
                  rowspec, rowspec, rowspec, const((2, nrow, npg * prow)), const((nrow, 1)), const((1, hd))]
                 + [page_spec(j) for j in range(npg)] * 2,
        out_specs=[pl.BlockSpec((t, hd), lambda b, h, i, pt: (b * nq + i, h)), rowspec],
        scratch_shapes=[pltpu.VMEM((2 * t, hd), bf16), pltpu.VMEM((1, 2 * t), f32),
                        pltpu.VMEM((1, 2 * t), f32), pltpu.VMEM((hd, 2 * t), f32),
                        pltpu.VMEM((t, 2 * t), f32), pltpu.VMEM((1, 2 * t), f32),
                        pltpu.VMEM((t, 2 * t), f32), pltpu.VMEM((1, 2 * t), f32),
                        pltpu.VMEM((nrow, hd), bf16), pltpu.VMEM((nrow, 1), f32),
                        pltpu.VMEM((nrow, 1), f32), pltpu.VMEM((nrow, hd), f32)],
    )
    ck = cache_k.reshape(-1, hd)
    cv = cache_v.reshape(-1, hd)
    return pl.pallas_call(
        functools.partial(_attn_kernel, t=t, cb=ATT_QUERY_BLOCK, npg=npg, lam_init=lam_init),
        grid_spec=grid_spec,
        out_shape=[jax.ShapeDtypeStruct((batch * seq, nh * hd), bf16), jax.ShapeDtypeStruct((nb, nh, hd), f32)],
        compiler_params=pltpu.CompilerParams(dimension_semantics=("arbitrary",) * 3,
                                             vmem_limit_bytes=ATTN_VMEM_LIMIT_BYTES),
        name="attn",
    )(page_table, lam, far, q, k, vt, tiles, gain.reshape(hd, 1), qs, kn, vn, bias2, b0, gain.reshape(1, hd),
      *([ck] * npg), *([cv] * npg))


def _softmax_rows(s):
    e = jnp.exp(s - jnp.max(s, axis=-1, keepdims=True))
    return e / jnp.sum(e, axis=-1, keepdims=True)


def _cross_prompt_kernel(x_ref, g_ref, wq_ref, mk_ref, mv_ref, wo_ref, o_ref, a_ref):
    x = x_ref[...]
    d = x.shape[1]
    hd = d // MEM_HEADS
    h = _rms(x, g_ref[...]).astype(bf16)
    q = (_dot(h, wq_ref[...]) * (hd ** -0.5)).astype(bf16)
    heads = [slice(hh * hd, (hh + 1) * hd) for hh in range(MEM_HEADS)]
    scores = [_dot_nt(q[:, sl], mk_ref[:, sl]) for sl in heads]
    for sl, s in zip(heads, scores):
        a_ref[:, sl] = _dot(_softmax_rows(s).astype(bf16), mv_ref[:, sl]).astype(bf16)
    o_ref[...] = x + _dot(a_ref[...], wo_ref[...])


def _cross_prompt(x, g, wq, mk, mv, wo, seq):
    m, d = x.shape
    _, wq_spec, wq = _stacked(wq)
    _, wo_spec, wo = _stacked(wo)
    tm = ROW_TILE
    per = seq // tm
    mem = mk.shape[0] // (m // seq)
    return pl.pallas_call(
        _cross_prompt_kernel,
        grid=(m // tm,),
        in_specs=[pl.BlockSpec((tm, d), lambda i: (i, 0)), _resident((1, d)), wq_spec,
                  pl.BlockSpec((mem, d), lambda i: (i // per, 0)), pl.BlockSpec((mem, d), lambda i: (i // per, 0)),
                  wo_spec],
        out_specs=pl.BlockSpec((tm, d), lambda i: (i, 0)),
        out_shape=jax.ShapeDtypeStruct((m, d), f32),
        scratch_shapes=[pltpu.VMEM((tm, d), bf16)],
        compiler_params=_params("arbitrary"),
        name="cross_prompt",
    )(x, g.reshape(1, d), wq, mk, mv, wo)


def _cross_sample_kernel(q_ref, mk_ref, mv_ref, mask_ref, o_ref):
    nseq, nh, hd = q_ref.shape
    rows = mask_ref.shape[1]
    for j in range(nseq):
        q = jnp.concatenate([q_ref[j] * (hd ** -0.5), jnp.zeros((SUBLANES - nh, hd), f32)], axis=0)
        s = _dot_nt(q.astype(bf16), mk_ref[j].reshape(rows, hd).astype(bf16)) + mask_ref[...]
        o = _dot(_softmax_rows(s).astype(bf16), mv_ref[j].reshape(rows, hd).astype(bf16))
        o_ref[j] = o[0:nh]


def _cross_sample(q, cache_mk, cache_mv, layer):
    nb, nh, hd = q.shape
    rows = cache_mk.shape[2] * nh
    head_of_row = jnp.arange(rows, dtype=i32) % nh
    mask = jnp.where(head_of_row[None, :] == jnp.arange(SUBLANES, dtype=i32)[:, None], 0.0, NEG_INF).astype(f32)
    mask = mask.at[nh:].set(0.0)
    nseq = 2 if nb % 2 == 0 else 1
    mem_spec = pl.BlockSpec((None, nseq, rows // nh, nh, hd), lambda b: (layer, b, 0, 0, 0))
    return pl.pallas_call(
        _cross_sample_kernel,
        grid=(nb // nseq,),
        in_specs=[pl.BlockSpec((nseq, nh, hd), lambda b: (b, 0, 0)), mem_spec, mem_spec,
                  _resident((SUBLANES, rows))],
        out_specs=pl.BlockSpec((nseq, nh, hd), lambda b: (b, 0, 0)),
        out_shape=jax.ShapeDtypeStruct((nb, nh, hd), f32),
        compiler_params=_params("arbitrary"),
        name="cross_sample",
    )(q, cache_mk, cache_mv, mask)


def _rotary_tables(pos, half):
    inv = 1.0 / (10000.0 ** jnp.linspace(0.0, 1.0, half, dtype=f32))
    ang = pos.astype(f32)[:, None] * inv[None, :]
    cos, sin = jnp.cos(ang), jnp.sin(ang)
    return jnp.concatenate([cos, cos], axis=-1), jnp.concatenate([-sin, sin], axis=-1)


def kernel(x_prompt, x_sample, state_ret, state_pool, cache_k_diff, cache_v_diff, cache_mem_k, cache_mem_v, page_table, mem_prompt, norms, final_norm, ffn_w_gate, ffn_w_up, ffn_w_down, w_in_even, ret_gain, pool_w, pool_scale, w_out_even, w_qkv_odd, lambda_q1, lambda_k1, lambda_q2, lambda_k2, subln_gain, w_out_odd, rel_bias, w_cq, w_ckv, w_co):
    batch, seq, d = x_prompt.shape
    nb = x_sample.shape[0]
    depth = norms.shape[0]
    past = page_table.shape[1] * PAGE_SIZE
    mem_len = mem_prompt.shape[1]
    rw = RET_HEADS * RET_DK
    dw = DIFF_HEADS * DIFF_HD

    xp = x_prompt.reshape(batch * seq, d)
    xs = x_sample.reshape(nb, d)
    mem2d = mem_prompt.reshape(batch * mem_len, d)

    cos_p, sin_p = _rotary_tables(jnp.arange(seq, dtype=i32), RET_DK // 2)
    cos_s, sin_s = _rotary_tables(jnp.full((1,), past, dtype=i32), RET_DK // 2)

    wg, wu, wd = ffn_w_gate.astype(bf16), ffn_w_up.astype(bf16), ffn_w_down.astype(bf16)
    w_in, w_oe = w_in_even.astype(bf16), w_out_even.astype(bf16)
    w_qkv, w_oo = w_qkv_odd.astype(bf16), w_out_odd.astype(bf16)
    wcq, wckv, wco = w_cq.astype(bf16), w_ckv.astype(bf16), w_co.astype(bf16)
    pw = pool_w.astype(bf16)

    p_ret, p_pool, p_k, p_v, p_mk, p_mv = [], [], [], [], [], []
    s_ret, s_pool, s_k, s_v = [], [], [], []
    for l in range(depth):
        last = l == depth - 1
        ffn_w = lambda i: ((wg, (l, i)), (wu, (l, i)), (wd, (l, i)))
        xp = _ffn(xp, norms[l, 0], *ffn_w(0))
        xs = _ffn(xs, norms[l, 0], *ffn_w(0))
        if l % 2 == 0:
            e = l // 2
            zw = w_in.shape[2]
            (zp,) = _proj(xp, norms[l, 1], (w_in, (e,)), [(0, zw, [(0, 1.0)])], [(zw, f32)])
            (zs,) = _proj(xs, norms[l, 1], (w_in, (e,)), [(0, zw, [(0, 1.0)])], [(zw, f32)])
            xp, rp, pp = _even_prompt(xp, zp, cos_p, sin_p, ret_gain[e], pw[e], pool_scale[e], (w_oe, (e,)),
                                      batch, seq)
            mixs, rs, ps = _even_sample(zs, cos_s, sin_s, state_ret[e:e + 1], state_pool[e:e + 1], ret_gain[e],
                                        pool_w[e], pool_scale[e], past)
            xs = _mmres(xs, mixs.reshape(nb, rw + pool_scale.shape[1]), (w_oe, (e,)))
            p_ret.append(rp[0])
            p_pool.append(pp[0])
            s_ret.append(rs[0])
            s_pool.append(ps[0])
        else:
            o = l // 2
            lam_init = 0.8 - 0.6 * math.exp(-0.3 * l)
            scale = DIFF_D ** -0.5
            tiles, tab, lam_t = _bias_tables(rel_bias, lambda_q1[o], lambda_k1[o], lambda_q2[o], lambda_k2[o],
                                             lam_init, ATT_TILE)
            lam = lam_t[0, 0:1]
            tab = tab[:, 0, :]
            far = tab[:, REL_MAX_DIST]
            qkv_groups = [(0, dw, [(0, scale * LOG2E)]), (dw, dw, [(1, 1.0), (3, 1.0)]),
                          (2 * dw, dw, [(2, 1.0), (4, 1.0)])]
            qb, kp, vp, kb, vtb = _proj(xp, norms[l, 1], (w_qkv, (o,)), qkv_groups,
                                        [(dw, bf16), (dw, f32), (dw, f32), (dw, bf16), (dw, bf16, "T")], seq=seq)
            qs, ks_, vs = _proj(xs, norms[l, 1], (w_qkv, (o,)),
                                [(0, dw, [(0, scale)]), (dw, dw, [(1, 1.0)]), (2 * dw, dw, [(2, 1.0)])],
                                [(dw, f32), (dw, f32), (dw, f32)])
            nh = DIFF_HEADS
            pages_per_step = page_table.shape[1] // (seq // ATT_TILE)
            own = jnp.arange(nh)[:, None, None] == jnp.arange(nh)[None, None, :]
            far_page = jnp.where(own, far[:, None, None], NEG_INF) + jnp.zeros((1, PAGE_SIZE, 1), f32)
            last_page = jnp.where(own, tab[:, PAGE_SIZE:0:-1][:, :, None], NEG_INF)
            far_page, last_page = far_page.reshape(nh, -1), last_page.reshape(nh, -1)
            far_step = jnp.tile(far_page, (2, pages_per_step))
            last_step = jnp.concatenate([far_step[:, :-PAGE_SIZE * nh], jnp.tile(last_page, (2, 1))], axis=1)
            bias2 = jnp.stack([far_step, last_step])
            b0 = jnp.tile(tab[:, 0], 2)[:, None]
            heads = lambda a: a.reshape(nb, nh, DIFF_HD)
            attp, atts = _attn(qb, kb, vtb, tiles, far, lam, subln_gain[o], batch, seq,
                               heads(qs), heads(ks_), heads(vs), cache_k_diff, cache_v_diff, o, page_table,
                               bias2, b0, lam_init)
            xp = _mmres(xp, attp, (w_oo, (o,)))
            xs = _mmres(xs, atts.reshape(nb, dw), (w_oo, (o,)))
            p_k.append(kp.reshape(batch, seq, DIFF_HEADS, DIFF_HD))
            p_v.append(vp.reshape(batch, seq, DIFF_HEADS, DIFF_HD))
            s_k.append(ks_.reshape(nb, 1, DIFF_HEADS, DIFF_HD))
            s_v.append(vs.reshape(nb, 1, DIFF_HEADS, DIFF_HD))
        mk, mv, mkb, mvb = _proj(mem2d, None, (wckv, (l,)),
                                 [(0, d, [(0, 1.0), (2, 1.0)]), (d, d, [(1, 1.0), (3, 1.0)])],
                                 [(d, f32), (d, f32), (d, bf16), (d, bf16)])
        p_mk.append(mk.reshape(batch, mem_len, MEM_HEADS, d // MEM_HEADS))
        p_mv.append(mv.reshape(batch, mem_len, MEM_HEADS, d // MEM_HEADS))
        xp = _cross_prompt(xp, norms[l, 2], (wcq, (l,)), mkb, mvb, (wco, (l,)), seq)
        (qcs,) = _proj(xs, norms[l, 2], (wcq, (l,)), [(0, d, [(0, 1.0)])], [(d, f32)])
        acs = _cross_sample(qcs.reshape(nb, MEM_HEADS, d // MEM_HEADS), cache_mem_k, cache_mem_v, l)
        xs = _mmres(xs, acs.reshape(nb, d), (wco, (l,)))
        fin = final_norm if last else None
        xp = _ffn(xp, norms[l, 3], *ffn_w(1), fin)
        xs = _ffn(xs, norms[l, 3], *ffn_w(1), fin)

    return (xp.reshape(batch, seq, d), xs.reshape(nb, 1, d),
            jnp.stack(p_ret), jnp.stack(p_pool), jnp.stack(p_k), jnp.stack(p_v), jnp.stack(p_mk), jnp.stack(p_mv),
            jnp.stack(s_ret), jnp.stack(s_pool), jnp.stack(s_k), jnp.stack(s_v))
```

```python
import functools
import math

import jax
import jax.numpy as jnp
from jax import lax
from jax.experimental import pallas as pl
from jax.experimental.pallas import tpu as pltpu

f32 = jnp.float32
bf16 = jnp.bfloat16
i32 = jnp.int32

EPS = 1e-6
NEG_INF = -1e30
LOG2E = math.log2(math.e)

RET_HEADS = 4
RET_DK = 128
POOL_WINDOWS = (2, 4, 8, 16)
POOL_KEEP = max(POOL_WINDOWS) - 1
DIFF_HEADS = 8
DIFF_D = 64
DIFF_HD = 2 * DIFF_D
REL_BUCKETS = 32
REL_MAX_DIST = 128
MEM_HEADS = 4
PAGE_SIZE = 128

VMEM_LIMIT_BYTES = 56 * 1024 * 1024
LANES = 128
SUBLANES = 8
MXU_TILE = 256

ROW_TILE = 512
ATT_TILE = 512
ATT_QUERY_BLOCK = 512
ATTN_VMEM_LIMIT_BYTES = 60 * 1024 * 1024


def _params(*sem):
    return pltpu.CompilerParams(dimension_semantics=sem, vmem_limit_bytes=VMEM_LIMIT_BYTES)


def _rms(x, g):
    return x * lax.rsqrt(jnp.mean(x * x, axis=-1, keepdims=True) + EPS) * g


def _dot(a, b):
    return jnp.dot(a, b, preferred_element_type=f32)


def _dot_nt(a, b):
    return lax.dot_general(a, b, (((1,), (1,)), ((), ())), preferred_element_type=f32)


def _dot_tn(a, b):
    return lax.dot_general(a, b, (((0,), (0,)), ((), ())), preferred_element_type=f32)


def _resident(shape, lead=()):
    nd = len(shape)
    return pl.BlockSpec((None,) * len(lead) + tuple(shape), lambda *_: tuple(lead) + (0,) * nd,
                        pipeline_mode=pl.Buffered(1))


def _stacked(w):
    arr, lead = w
    shape = arr.shape[len(lead):]
    return shape, _resident(shape, lead), arr


def _ffn_kernel(x_ref, g_ref, wg_ref, wu_ref, wd_ref, *rest, bounds, final):
    o_ref = rest[-1]
    x = x_ref[...]
    h = _rms(x, g_ref[...]).astype(bf16)
    acc = jnp.zeros(x.shape, f32)
    for c0, c1 in zip(bounds[:-1], bounds[1:]):
        gate = _dot(h, wg_ref[:, c0:c1])
        up = _dot(h, wu_ref[:, c0:c1])
        a = (jax.nn.silu(gate) * up).astype(bf16)
        acc = acc + _dot(a, wd_ref[c0:c1, :])
    y = x + 0.5 * acc
    if final:
        y = _rms(y, rest[0][...])
    o_ref[...] = y


def _ffn(x, g, wg, wu, wd, final_g=None):
    m, d = x.shape
    (_, f), wg_spec, wg = _stacked(wg)
    _, wu_spec, wu = _stacked(wu)
    _, wd_spec, wd = _stacked(wd)
    tm = min(ROW_TILE, m)
    split = -(-f // (2 * MXU_TILE)) * MXU_TILE
    bounds = (0, split, f) if split < f else (0, f)
    final = final_g is not None
    in_specs = [pl.BlockSpec((tm, d), lambda i: (i, 0)), _resident((1, d)), wg_spec, wu_spec, wd_spec]
    args = [x, g.reshape(1, d), wg, wu, wd]
    if final:
        in_specs.append(_resident((1, d)))
        args.append(final_g.reshape(1, d))
    return pl.pallas_call(
        functools.partial(_ffn_kernel, bounds=bounds, final=final),
        grid=(m // tm,),
        in_specs=in_specs,
        out_specs=pl.BlockSpec((tm, d), lambda i: (i, 0)),
        out_shape=jax.ShapeDtypeStruct((m, d), f32),
        compiler_params=_params("arbitrary"),
        name="ffn",
    )(*args)


def _proj_kernel(*refs, norm, groups, transposed, nchunk):
    x_ref = refs[0]
    g_ref = refs[1] if norm else None
    w_ref = refs[2] if norm else refs[1]
    o_refs = refs[(3 if norm else 2):]
    x = x_ref[...]
    h = (_rms(x, g_ref[...]) if norm else x).astype(bf16)
    for off, width, targets in groups:
        for c0 in range(0, width, nchunk):
            n = min(nchunk, width - c0)
            r = _dot(h, w_ref[:, off + c0:off + c0 + n])
            for oi, scale in targets:
                o_ref = o_refs[oi]
                val = (r if scale == 1.0 else r * scale)
                if transposed[oi]:
                    o_ref[c0:c0 + n, :] = val.T.astype(o_ref.dtype)
                else:
                    o_ref[:, c0:c0 + n] = val.astype(o_ref.dtype)


def _proj(x, g, w, groups, outs, seq=None):
    m, d = x.shape
    _, w_spec, w = _stacked(w)
    tm = min(ROW_TILE, m)
    norm = g is not None
    in_specs = [pl.BlockSpec((tm, d), lambda i: (i, 0))]
    args = [x]
    if norm:
        in_specs.append(_resident((1, d)))
        args.append(g.reshape(1, d))
    in_specs.append(w_spec)
    args.append(w)
    out_specs, out_shape = [], []
    for wd, dt, *flag in outs:
        if flag:
            per = seq // tm
            out_specs.append(pl.BlockSpec((wd, tm), lambda i: (i // per, i % per)))
            out_shape.append(jax.ShapeDtypeStruct((m // seq * wd, seq), dt))
        else:
            out_specs.append(pl.BlockSpec((tm, wd), lambda i: (i, 0)))
            out_shape.append(jax.ShapeDtypeStruct((m, wd), dt))
    return pl.pallas_call(
        functools.partial(_proj_kernel, norm=norm, groups=groups, transposed=tuple(o[2:] == ("T",) for o in outs),
                          nchunk=512),
        grid=(m // tm,),
        in_specs=in_specs,
        out_specs=out_specs,
        out_shape=out_shape,
        compiler_params=_params("arbitrary"),
        name="proj",
    )(*args)


def _mmres_kernel(x_ref, a_ref, w_ref, o_ref):
    o_ref[...] = x_ref[...] + _dot(a_ref[...].astype(bf16), w_ref[...])


def _mmres(x, a, w):
    m, d = x.shape
    k = a.shape[1]
    _, w_spec, w = _stacked(w)
    tm = min(ROW_TILE, m)
    return pl.pallas_call(
        _mmres_kernel,
        grid=(m // tm,),
        in_specs=[pl.BlockSpec((tm, d), lambda i: (i, 0)), pl.BlockSpec((tm, k), lambda i: (i, 0)), w_spec],
        out_specs=pl.BlockSpec((tm, d), lambda i: (i, 0)),
        out_shape=jax.ShapeDtypeStruct((m, d), f32),
        compiler_params=_params("arbitrary"),
        name="mmres",
    )(x, a, w)


def _ret_log_gamma(h):
    return math.log1p(-(2.0 ** (-5.0 - h)))


def _rotate(x, cosf, sinf):
    return x * cosf + pltpu.roll(x, x.shape[-1] // 2, axis=1) * sinf


def _group_norm_gate(o, gain, gate):
    mu = jnp.mean(o, axis=-1, keepdims=True)
    var = jnp.mean(jnp.square(o - mu), axis=-1, keepdims=True)
    return jax.nn.silu(gate) * ((o - mu) * lax.rsqrt(var + EPS) * gain)


def _even_prompt_kernel(x_ref, zq_ref, zk_ref, zv_ref, zg_ref, zp_ref, cos_ref, sin_ref, gain_ref, pw_ref, ps_ref,
                        wo_ref, xo_ref, sret_ref, spool_ref, state_ref, ext_ref, intra_ref, mix_ref, *, t):
    b = pl.program_id(0)
    c = pl.program_id(1)
    nc = pl.num_programs(1)
    halo = POOL_KEEP + 1
    dk = RET_DK

    @pl.when(jnp.logical_and(b == 0, c == 0))
    def _():
        rel = lax.broadcasted_iota(i32, (t, t), 0) - lax.broadcasted_iota(i32, (t, t), 1)
        relf = jnp.maximum(rel, 0).astype(f32)
        for h in range(RET_HEADS):
            intra_ref[h] = jnp.where(rel >= 0, jnp.exp(_ret_log_gamma(h) * relf), 0.0)

    @pl.when(c == 0)
    def _():
        state_ref[...] = jnp.zeros(state_ref.shape, f32)
        ext_ref[0:halo, :] = jnp.zeros((halo, ext_ref.shape[1]), f32)

    cosf = cos_ref[...]
    sinf = sin_ref[...]
    row = lax.broadcasted_iota(i32, (t, 1), 0).astype(f32)
    heads = []
    for h in range(RET_HEADS):
        sl = slice(h * dk, (h + 1) * dk)
        qb = _rotate(zq_ref[:, sl], cosf, sinf).astype(bf16)
        kr = _rotate(zk_ref[:, sl], cosf, sinf) * (dk ** -0.5)
        heads.append((sl, qb, kr, _dot_nt(qb, kr.astype(bf16))))
    for h, (sl, qb, kr, qk) in enumerate(heads):
        lg = _ret_log_gamma(h)
        vb = zv_ref[:, sl].astype(bf16)
        att = qk * intra_ref[h]
        s_old = state_ref[h]
        o = _dot(att.astype(bf16), vb) + _dot(qb, s_old.astype(bf16)) * jnp.exp(lg * (row + 1.0))
        k_dec = jnp.exp(lg * (float(t - 1) - row))
        state_ref[h] = s_old * math.exp(lg * t) + _dot_tn((kr * k_dec).astype(bf16), vb)
        mix_ref[:, sl] = _group_norm_gate(o, gain_ref[:, sl], zg_ref[:, sl]).astype(mix_ref.dtype)

    p = zp_ref[...]
    ext_ref[halo:halo + t, :] = p
    pos = c * t + lax.broadcasted_iota(i32, (t, 1), 0)
    gw = p.shape[1] // len(POOL_WINDOWS)
    ret_w = RET_HEADS * dk
    for gi, w in enumerate(POOL_WINDOWS):
        sl = slice(gi * gw, (gi + 1) * gw)
        win = p[:, sl]
        for back in range(1, w):
            win = win + ext_ref[halo - back:halo - back + t, sl]
        cnt = jnp.minimum(w, pos + 1).astype(f32)
        pooled = win / cnt - p[:, sl]
        mixed = _dot(pooled.astype(bf16), pw_ref[gi]) * ps_ref[:, sl]
        mix_ref[:, ret_w + gi * gw:ret_w + (gi + 1) * gw] = mixed.astype(mix_ref.dtype)

    xo_ref[...] = x_ref[...] + _dot(mix_ref[...], wo_ref[...])

    @pl.when(c == nc - 1)
    def _():
        sret_ref[0, 0] = state_ref[...]
        spool_ref[0, 0] = ext_ref[t + 1:t + halo, :]

    ext_ref[0:halo, :] = ext_ref[t:t + halo, :]


def _even_prompt(x, z, cosf, sinf, gain, pw, ps, wo, batch, seq):
    t = ROW_TILE
    nc = seq // t
    d = x.shape[1]
    rw = RET_HEADS * RET_DK
    pwid = ps.shape[0]
    _, wo_spec, wo = _stacked(wo)
    row = lambda b, c: (b * nc + c, 0)
    zspec = lambda j: pl.BlockSpec((t, rw), lambda b, c: (b * nc + c, j))
    return pl.pallas_call(
        functools.partial(_even_prompt_kernel, t=t),
        grid=(batch, nc),
        in_specs=[pl.BlockSpec((t, d), row), zspec(0), zspec(1), zspec(2), zspec(3), zspec(4),
                  pl.BlockSpec((t, RET_DK), lambda b, c: (c, 0)), pl.BlockSpec((t, RET_DK), lambda b, c: (c, 0)),
                  _resident((1, rw)), _resident(pw.shape), _resident((1, pwid)), wo_spec],
        out_specs=[pl.BlockSpec((t, d), row),
                   pl.BlockSpec((1, 1, RET_HEADS, RET_DK, RET_DK), lambda b, c: (0, b, 0, 0, 0)),
                   pl.BlockSpec((1, 1, POOL_KEEP, pwid), lambda b, c: (0, b, 0, 0))],
        out_shape=[jax.ShapeDtypeStruct((batch * seq, d), f32),
                   jax.ShapeDtypeStruct((1, batch, RET_HEADS, RET_DK, RET_DK), f32),
                   jax.ShapeDtypeStruct((1, batch, POOL_KEEP, pwid), f32)],
        scratch_shapes=[pltpu.VMEM((RET_HEADS, RET_DK, RET_DK), f32),
                        pltpu.VMEM((POOL_KEEP + 1 + t, pwid), f32),
                        pltpu.VMEM((RET_HEADS, t, t), f32),
                        pltpu.VMEM((t, rw + pwid), bf16)],
        compiler_params=_params("arbitrary", "arbitrary"),
        name="even_prompt",
    )(x, z, z, z, z, z, cosf, sinf, gain.reshape(1, rw), pw, ps.reshape(1, pwid), wo)


def _even_sample_kernel(z_ref, cos_ref, sin_ref, s0_ref, rows_ref, gain_ref, pw_ref, ps_ref,
                        mix_ref, sret_ref, spool_ref, ext_ref, *, pos):
    dk = RET_DK
    rw = RET_HEADS * dk
    z = jnp.broadcast_to(z_ref[0], (SUBLANES, z_ref.shape[2]))
    cosf = jnp.broadcast_to(cos_ref[...], (SUBLANES, dk))
    sinf = jnp.broadcast_to(sin_ref[...], (SUBLANES, dk))
    first = lax.broadcasted_iota(i32, (SUBLANES, dk), 0) == 0
    for h in range(RET_HEADS):
        sl = slice(h * dk, (h + 1) * dk)
        gamma = math.exp(_ret_log_gamma(h))
        qr = _rotate(z[:, sl], cosf, sinf)
        kr = _rotate(z[:, rw + h * dk:rw + (h + 1) * dk], cosf, sinf) * (dk ** -0.5)
        v = z[:, 2 * rw + h * dk:2 * rw + (h + 1) * dk]
        s_new = s0_ref[0, 0, h] * gamma + _dot_tn(jnp.where(first, kr, 0.0), v)
        sret_ref[0, 0, h] = s_new
        o = _dot(qr, s_new)
        gate = z[:, 3 * rw + h * dk:3 * rw + (h + 1) * dk]
        mix_ref[0, :, sl] = _group_norm_gate(o, gain_ref[:, sl], gate)[0:1]

    keep = POOL_KEEP
    pwid = ps_ref.shape[1]
    p = z[0:1, 4 * rw:4 * rw + pwid]
    ext_ref[0:keep, :] = rows_ref[0, 0]
    ext_ref[keep:keep + 1, :] = p
    spool_ref[0, 0] = ext_ref[1:keep + 1, :]
    gw = pwid // len(POOL_WINDOWS)
    for gi, w in enumerate(POOL_WINDOWS):
        sl = slice(gi * gw, (gi + 1) * gw)
        win = jnp.sum(ext_ref[keep + 1 - w:keep + 1, sl], axis=0, keepdims=True)
        pooled = win / float(min(w, pos + 1)) - p[:, sl]
        mixed = _dot(jnp.broadcast_to(pooled, (SUBLANES, gw)), pw_ref[gi])[0:1] * ps_ref[:, sl]
        mix_ref[0, :, rw + gi * gw:rw + (gi + 1) * gw] = mixed


def _even_sample(z, cosf, sinf, state_ret, state_pool, gain, pw, ps, pos):
    nb, zw = z.shape
    rw = RET_HEADS * RET_DK
    pwid = ps.shape[0]
    return pl.pallas_call(
        functools.partial(_even_sample_kernel, pos=pos),
        grid=(nb,),
        in_specs=[pl.BlockSpec((1, 1, zw), lambda b: (b, 0, 0)),
                  _resident((1, RET_DK)), _resident((1, RET_DK)),
                  pl.BlockSpec((1, 1, RET_HEADS, RET_DK, RET_DK), lambda b: (0, b, 0, 0, 0)),
                  pl.BlockSpec((1, 1, POOL_KEEP, pwid), lambda b: (0, b, 0, 0)),
                  _resident((1, rw)), _resident(pw.shape), _resident((1, pwid))],
        out_specs=[pl.BlockSpec((1, 1, rw + pwid), lambda b: (b, 0, 0)),
                   pl.BlockSpec((1, 1, RET_HEADS, RET_DK, RET_DK), lambda b: (0, b, 0, 0, 0)),
                   pl.BlockSpec((1, 1, POOL_KEEP, pwid), lambda b: (0, b, 0, 0))],
        out_shape=[jax.ShapeDtypeStruct((nb, 1, rw + pwid), f32),
                   jax.ShapeDtypeStruct((1, nb, RET_HEADS, RET_DK, RET_DK), f32),
                   jax.ShapeDtypeStruct((1, nb, POOL_KEEP, pwid), f32)],
        scratch_shapes=[pltpu.VMEM((POOL_KEEP + 1, pwid), f32)],
        compiler_params=_params("arbitrary"),
        name="even_sample",
    )(z.reshape(nb, 1, zw), cosf, sinf, state_ret, state_pool, gain.reshape(1, rw), pw, ps.reshape(1, pwid))


def _rel_bias_of(dist, rb_ref, h):
    n = jnp.maximum(dist, 0)
    max_exact = REL_BUCKETS // 2
    nf = jnp.maximum(n, 1).astype(f32)
    large = max_exact + (jnp.log(nf / max_exact) / math.log(REL_MAX_DIST / max_exact)
                         * (REL_BUCKETS - max_exact)).astype(i32)
    large = jnp.minimum(large, REL_BUCKETS - 1)
    bucket = jnp.where(n < max_exact, n, large)
    out = jnp.zeros(dist.shape, f32)
    for bk in range(REL_BUCKETS):
        out = jnp.where(bucket == bk, rb_ref[bk, h], out)
    return out


def _bias_kernel(rb_ref, lq1_ref, lk1_ref, lq2_ref, lk2_ref, tile_ref, tab_ref, lam_ref, *, t, lam_init):
    h = pl.program_id(0)
    nd = REL_MAX_DIST
    rel = lax.broadcasted_iota(i32, (nd, nd), 1) - lax.broadcasted_iota(i32, (nd, nd), 0)
    on_diag = jnp.where(rel >= 0, _rel_bias_of(rel, rb_ref, h) * LOG2E, NEG_INF)
    next_diag = _rel_bias_of(rel + nd, rb_ref, h) * LOG2E
    far = _rel_bias_of(jnp.full((nd, nd), nd, i32), rb_ref, h) * LOG2E
    masked = jnp.full((nd, nd), NEG_INF, f32)
    nblk = t // nd
    for bk in range(nblk):
        for bq in range(nblk):
            diag_blk = on_diag if bq == bk else next_diag if bq == bk + 1 else far if bq > bk else masked
            below_blk = next_diag if (bk == nblk - 1 and bq == 0) else far
            tile_ref[0, 0, bk * nd:(bk + 1) * nd, bq * nd:(bq + 1) * nd] = diag_blk
            tile_ref[0, 1, bk * nd:(bk + 1) * nd, bq * nd:(bq + 1) * nd] = below_blk
    tab_ref[0] = _rel_bias_of(lax.broadcasted_iota(i32, (1, 2 * REL_MAX_DIST), 1), rb_ref, h)
    lam = (jnp.exp(jnp.sum(lq1_ref[...] * lk1_ref[...], axis=-1, keepdims=True))
           - jnp.exp(jnp.sum(lq2_ref[...] * lk2_ref[...], axis=-1, keepdims=True)) + lam_init)
    lam_ref[...] = jnp.broadcast_to(lam, lam_ref.shape)


def _bias_tables(rel_bias, lq1, lk1, lq2, lk2, lam_init, t):
    nh = rel_bias.shape[1]
    d = lq1.shape[0]
    vec = lambda a: a.reshape(1, d)
    return pl.pallas_call(
        functools.partial(_bias_kernel, t=t, lam_init=lam_init),
        grid=(nh,),
        in_specs=[pl.BlockSpec(memory_space=pltpu.SMEM)] + [_resident((1, d))] * 4,
        out_specs=[pl.BlockSpec((1, 2, t, t), lambda h: (h, 0, 0, 0)),
                   pl.BlockSpec((1, 1, 2 * REL_MAX_DIST), lambda h: (h, 0, 0)),
                   pl.BlockSpec((SUBLANES, LANES), lambda h: (0, 0))],
        out_shape=[jax.ShapeDtypeStruct((nh, 2, t, t), f32),
                   jax.ShapeDtypeStruct((nh, 1, 2 * REL_MAX_DIST), f32),
                   jax.ShapeDtypeStruct((SUBLANES, LANES), f32)],
        compiler_params=_params("arbitrary"),
        name="bias_tables",
    )(rel_bias, vec(lq1), vec(lk1), vec(lq2), vec(lk2))


def _sub_ln(o, gain, lam_init):
    return o * lax.rsqrt(jnp.mean(o * o, axis=-1, keepdims=True) + EPS) * gain * (1.0 - lam_init)


def _attn_prompt_tile(lam_ref, far_ref, q_ref, k_ref, vt_ref, tile_ref, gain_ref, o_ref,
                      qq_ref, m_ref, l_ref, acc_ref, s0_ref, smax0_ref, s1_ref, smax1_ref, *, t, cb, lam_init,
                      beside_own_tile):
    h = pl.program_id(1)
    qi = pl.program_id(2)
    q = q_ref[...]
    lane = lax.broadcasted_iota(i32, q.shape, 1)
    zero = jnp.zeros_like(q)
    qq_ref[0:t, :] = jnp.where(lane < DIFF_D, q, zero)
    qq_ref[t:2 * t, :] = jnp.where(lane >= DIFF_D, q, zero)
    m_ref[...] = jnp.full(m_ref.shape, NEG_INF, f32)
    l_ref[...] = jnp.zeros(l_ref.shape, f32)
    acc_ref[...] = jnp.zeros(acc_ref.shape, f32)

    far = far_ref[h] * LOG2E

    blocks = [slice(c0, c0 + cb) for c0 in range(0, 2 * t, cb)]

    parked = ((s0_ref, smax0_ref), (s1_ref, smax1_ref))

    def scores(ki, slot):
        s_ref, smax_ref = parked[slot]
        kt = k_ref[pl.ds(pl.multiple_of(ki * t, t), t), :]
        for cols in blocks:
            s = _dot_nt(kt, qq_ref[cols, :])
            s_ref[:, cols] = s
            smax_ref[:, cols] = jnp.max(s, axis=0, keepdims=True)

    def scores_own(ki, slot):
        s_ref, _ = parked[slot]
        k0 = pl.multiple_of(ki * t, t)
        for c0 in (0, t):
            for q0, nk in ((0, t // 2), (t // 2, t)):
                cols = slice(c0 + q0, c0 + q0 + t // 2)
                s_ref[0:nk, cols] = _dot_nt(k_ref[pl.ds(k0, nk), :], qq_ref[cols, :])

    def update(cols, m_old, m_new, p, vt):
        alpha = jnp.exp2(m_old - m_new)
        l_ref[:, cols] = alpha * l_ref[:, cols] + jnp.sum(p, axis=0, keepdims=True)
        acc_ref[:, cols] = alpha * acc_ref[:, cols] + _dot(vt, p.astype(bf16))
        m_ref[:, cols] = m_new

    def absorb_far(ki, slot):
        s_ref, smax_ref = parked[slot]
        vt = vt_ref[:, pl.ds(pl.multiple_of(ki * t, t), t)]
        for cols in blocks:
            m_old = m_ref[:, cols]
            m_new = jnp.maximum(m_old, smax_ref[:, cols] + far)
            update(cols, m_old, m_new, jnp.exp2(s_ref[:, cols] - (m_new - far)), vt)

    def absorb_behind(ki, slot):
        s_ref, smax_ref = parked[slot]
        vt = vt_ref[:, pl.ds(pl.multiple_of(ki * t, t), t)]
        corner = tile_ref[0, 1, t - nd:t, 0:nd]
        for c0 in (0, t):
            cols, near, rest = slice(c0, c0 + t), slice(c0, c0 + nd), slice(c0 + nd, c0 + t)
            m_old = m_ref[:, cols]
            s_near = jnp.concatenate([s_ref[0:t - nd, near] + far, s_ref[t - nd:t, near] + corner], axis=0)
            m_near = jnp.maximum(m_old[:, 0:nd], jnp.max(s_near, axis=0, keepdims=True))
            m_rest = jnp.maximum(m_old[:, nd:t], smax_ref[:, rest] + far)
            p = jnp.concatenate([jnp.exp2(s_near - m_near), jnp.exp2(s_ref[:, rest] - (m_rest - far))], axis=1)
            update(cols, m_old, jnp.concatenate([m_near, m_rest], axis=1), p, vt)

    def absorb_own(ki, slot):
        s_ref, _ = parked[slot]
        k0 = pl.multiple_of(ki * t, t)
        for c0 in (0, t):
            for q0, nk in ((0, t // 2), (t // 2, t)):
                cols = slice(c0 + q0, c0 + q0 + t // 2)
                m_old = m_ref[:, cols]
                s = s_ref[0:nk, cols] + tile_ref[0, 0, 0:nk, q0:q0 + t // 2]
                m_new = jnp.maximum(m_old, jnp.max(s, axis=0, keepdims=True))
                update(cols, m_old, m_new, jnp.exp2(s - m_new), vt_ref[:, pl.ds(k0, nk)])

    nd = REL_MAX_DIST
    n_far = jnp.maximum(qi - 1, 0)
    scores(0, 0)

    def far_pair(j, carry):
        for slot in (0, 1):
            scores(2 * j + slot + 1, 1 - slot)
            absorb_far(2 * j + slot, slot)
        return carry

    lax.fori_loop(0, n_far // 2, far_pair, 0)

    @pl.when(n_far % 2 == 1)
    def _():
        scores(n_far, 1)
        absorb_far(n_far - 1, 0)

    for slot in (0, 1):
        @pl.when(jnp.logical_and(qi >= 1, n_far % 2 == slot))
        def _():
            scores_own(qi, 1 - slot)
            absorb_behind(qi - 1, slot)

    for slot in (0, 1):
        @pl.when(qi % 2 == slot)
        def _():
            beside_own_tile()
            absorb_own(qi, slot)

    o = acc_ref[...] / l_ref[...]
    o = o[:, 0:t] - lam_ref[0] * o[:, t:2 * t]
    y = o * lax.rsqrt(jnp.mean(o * o, axis=0, keepdims=True) + EPS) * gain_ref[...] * (1.0 - lam_init)
    o_ref[...] = y.T.astype(o_ref.dtype)


def _attn_decode_parts(lam_ref, q_ref, kn_ref, vn_ref, bias_ref, b0_ref, gain_ref, k_refs, v_refs, o_ref,
                       qh_ref, m_ref, l_ref, acc_ref, *, chunk_axis, lam_init):
    c = pl.program_id(chunk_axis)
    nc = pl.num_programs(chunk_axis)
    nh = DIFF_HEADS
    prow = PAGE_SIZE * nh

    def both_streams(x):
        return jnp.concatenate([x, x], axis=0)

    def start():
        @pl.when(c == 0)
        def _():
            q = q_ref[0]
            lane = lax.broadcasted_iota(i32, q.shape, 1)
            qh = jnp.concatenate([jnp.where(lane < DIFF_D, q, 0.0), jnp.where(lane >= DIFF_D, q, 0.0)],
                                 axis=0).astype(bf16)
            qh_ref[...] = qh
            kn = both_streams(kn_ref[0]).astype(bf16).astype(f32)
            m_ref[...] = jnp.sum(qh.astype(f32) * kn, axis=-1, keepdims=True) + b0_ref[...]
            l_ref[...] = jnp.ones(l_ref.shape, f32)
            acc_ref[...] = both_streams(vn_ref[0]).astype(bf16).astype(f32)

    def absorb_pages():
        qh = qh_ref[...]
        s = jnp.concatenate([_dot_nt(qh, k_ref[...].astype(bf16)) for k_ref in k_refs], axis=1)
        s = s + bias_ref[(c == nc - 1).astype(i32)]
        m_old = m_ref[...]
        m_new = jnp.maximum(m_old, jnp.max(s, axis=-1, keepdims=True))
        alpha = jnp.exp(m_old - m_new)
        p = jnp.exp(s - m_new)
        l_ref[...] = alpha * l_ref[...] + jnp.sum(p, axis=-1, keepdims=True)
        pv = alpha * acc_ref[...]
        for i, v_ref in enumerate(v_refs):
            pv = pv + _dot(p[:, i * prow:(i + 1) * prow].astype(bf16), v_ref[...].astype(bf16))
        acc_ref[...] = pv
        m_ref[...] = m_new

    def finish():
        @pl.when(c == nc - 1)
        def _():
            on = acc_ref[...] / l_ref[...]
            o_ref[0] = _sub_ln(on[0:nh] - lam_ref[0] * on[nh:2 * nh], gain_ref[...], lam_init)

    return start, absorb_pages, finish


N_PROMPT_SCRATCH = 8


def _attn_kernel(pt_ref, lam_ref, far_ref, q_ref, k_ref, vt_ref, tile_ref, gain_col_ref,
                 qs_ref, kn_ref, vn_ref, bias_ref, b0_ref, gain_row_ref, *rest, t, cb, npg, lam_init):
    k_pages, v_pages = rest[:npg], rest[npg:2 * npg]
    o_ref, os_ref = rest[2 * npg:2 * npg + 2]
    scratch = rest[2 * npg + 2:]
    del pt_ref
    start, absorb_pages, finish = _attn_decode_parts(
        lam_ref, qs_ref, kn_ref, vn_ref, bias_ref, b0_ref, gain_row_ref, k_pages, v_pages, os_ref,
        *scratch[N_PROMPT_SCRATCH:], chunk_axis=2, lam_init=lam_init)
    start()
    _attn_prompt_tile(lam_ref, far_ref, q_ref, k_ref, vt_ref, tile_ref, gain_col_ref, o_ref,
                      *scratch[:N_PROMPT_SCRATCH], t=t, cb=cb, lam_init=lam_init, beside_own_tile=absorb_pages)
    finish()


def _attn(q, k, vt, tiles, far, lam, gain, batch, seq, qs, kn, vn, cache_k, cache_v, layer, page_table, bias2, b0,
          lam_init):
    t = ATT_TILE
    nq = seq // t
    hd = DIFF_HD
    nh = DIFF_HEADS
    nb = qs.shape[0]
    pool = cache_k.shape[1]
    npg = page_table.shape[1] // nq
    assert nb == batch * nh and npg * nq == page_table.shape[1], "one sample sequence per prompt (batch, head)"
    nrow = 2 * nh
    prow = PAGE_SIZE * nh
    smem = pl.BlockSpec(memory_space=pltpu.SMEM)
    page_spec = lambda j: pl.BlockSpec(
        (prow, hd), lambda b, h, i, pt: (layer * pool + pt[b * nh + h, i * npg + j], 0))
    rowspec = pl.BlockSpec((1, nh, hd), lambda b, h, i, pt: (b * nh + h, 0, 0))
    const = lambda shape: pl.BlockSpec(shape, lambda b, h, i, pt: (0,) * len(shape), pipeline_mode=pl.Buffered(1))
    grid_spec = pltpu.PrefetchScalarGridSpec(
        num_scalar_prefetch=1,
        grid=(batch, nh, nq),
        in_specs=[smem, smem,
                  pl.BlockSpec((t, hd), lambda b, h, i, pt: (b * nq + i, h)),
                  pl.BlockSpec((seq, hd), lambda b, h, i, pt: (b, h)),
                  pl.BlockSpec((hd, seq), lambda b, h, i, pt: (b * nh + h, 0)),
                  pl.BlockSpec((1, 2, t, t), lambda b, h, i, pt: (h, 0, 0, 0)),
                  const((hd, 1)),
                  rowspec, rowspec, rowspec, const((2, nrow, npg * prow)), const((nrow, 1)), const((1, hd))]
                 + [page_spec(j) for j in range(npg)] * 2,
        out_specs=[pl.BlockSpec((t, hd), lambda b, h, i, pt: (b * nq + i, h)), rowspec],
        scratch_shapes=[pltpu.VMEM((2 * t, hd), bf16), pltpu.VMEM((1, 2 * t), f32),
                        pltpu.VMEM((1, 2 * t), f32), pltpu.VMEM((hd, 2 * t), f32),
                        pltpu.VMEM((t, 2 * t), f32), pltpu.VMEM((1, 2 * t), f32),
                        pltpu.VMEM((t, 2 * t), f32), pltpu.VMEM((1, 2 * t), f32),
                        pltpu.VMEM((nrow, hd), bf16), pltpu.VMEM((nrow, 1), f32),
                        pltpu.VMEM((nrow, 1), f32), pltpu.VMEM((nrow, hd), f32)],
    )
    ck = cache_k.reshape(-1, hd)
    cv = cache_v.reshape(-1, hd)
    return pl.pallas_call(
        functools.partial(_attn_kernel, t=t, cb=ATT_QUERY_BLOCK, npg=npg, lam_init=lam_init),
        grid_spec=grid_spec,
        out_shape=[jax.ShapeDtypeStruct((batch * seq, nh * hd), bf16), jax.ShapeDtypeStruct((nb, nh, hd), f32)],
        compiler_params=pltpu.CompilerParams(dimension_semantics=("arbitrary",) * 3,
                                             vmem_limit_bytes=ATTN_VMEM_LIMIT_BYTES),
        name="attn",
    )(page_table, lam, far, q, k, vt, tiles, gain.reshape(hd, 1), qs, kn, vn, bias2, b0, gain.reshape(1, hd),
      *([ck] * npg), *([cv] * npg))


def _softmax_rows(s):
    e = jnp.exp(s - jnp.max(s, axis=-1, keepdims=True))
    return e / jnp.sum(e, axis=-1, keepdims=True)


def _cross_prompt_kernel(x_ref, g_ref, wq_ref, mk_ref, mv_ref, wo_ref, *rest, mixer_pending):
    o_ref, a_ref = rest[-2:]
    x = x_ref[...]
    if mixer_pending:
        ma_ref, mw_ref = rest[:2]
        x = x + _dot(ma_ref[...], mw_ref[...])
    d = x.shape[1]
    hd = d // MEM_HEADS
    h = _rms(x, g_ref[...]).astype(bf16)
    q = (_dot(h, wq_ref[...]) * (hd ** -0.5)).astype(bf16)
    heads = [slice(hh * hd, (hh + 1) * hd) for hh in range(MEM_HEADS)]
    scores = [_dot_nt(q[:, sl], mk_ref[:, sl]) for sl in heads]
    for sl, s in zip(heads, scores):
        a_ref[:, sl] = _dot(_softmax_rows(s).astype(bf16), mv_ref[:, sl]).astype(bf16)
    o_ref[...] = x + _dot(a_ref[...], wo_ref[...])


def _cross_prompt(x, g, wq, mk, mv, wo, seq, mixer=None):
    m, d = x.shape
    _, wq_spec, wq = _stacked(wq)
    _, wo_spec, wo = _stacked(wo)
    tm = ROW_TILE
    per = seq // tm
    mem = mk.shape[0] // (m // seq)
    in_specs = [pl.BlockSpec((tm, d), lambda i: (i, 0)), _resident((1, d)), wq_spec,
                pl.BlockSpec((mem, d), lambda i: (i // per, 0)), pl.BlockSpec((mem, d), lambda i: (i // per, 0)),
                wo_spec]
    args = [x, g.reshape(1, d), wq, mk, mv, wo]
    if mixer is not None:
        a, w = mixer
        _, w_spec, w = _stacked(w)
        in_specs += [pl.BlockSpec((tm, a.shape[1]), lambda i: (i, 0)), w_spec]
        args += [a, w]
    return pl.pallas_call(
        functools.partial(_cross_prompt_kernel, mixer_pending=mixer is not None),
        grid=(m // tm,),
        in_specs=in_specs,
        out_specs=pl.BlockSpec((tm, d), lambda i: (i, 0)),
        out_shape=jax.ShapeDtypeStruct((m, d), f32),
        scratch_shapes=[pltpu.VMEM((tm, d), bf16)],
        compiler_params=_params("arbitrary"),
        name="cross_prompt",
    )(*args)


def _cross_sample_kernel(q_ref, mk_ref, mv_ref, mask_ref, o_ref):
    nseq, nh, hd = q_ref.shape
    rows = mask_ref.shape[1]
    for j in range(nseq):
        q = jnp.concatenate([q_ref[j] * (hd ** -0.5), jnp.zeros((SUBLANES - nh, hd), f32)], axis=0)
        s = _dot_nt(q.astype(bf16), mk_ref[j].reshape(rows, hd).astype(bf16)) + mask_ref[...]
        o = _dot(_softmax_rows(s).astype(bf16), mv_ref[j].reshape(rows, hd).astype(bf16))
        o_ref[j] = o[0:nh]


def _cross_sample(q, cache_mk, cache_mv, layer):
    nb, nh, hd = q.shape
    rows = cache_mk.shape[2] * nh
    head_of_row = jnp.arange(rows, dtype=i32) % nh
    mask = jnp.where(head_of_row[None, :] == jnp.arange(SUBLANES, dtype=i32)[:, None], 0.0, NEG_INF).astype(f32)
    mask = mask.at[nh:].set(0.0)
    nseq = 2 if nb % 2 == 0 else 1
    mem_spec = pl.BlockSpec((None, nseq, rows // nh, nh, hd), lambda b: (layer, b, 0, 0, 0))
    return pl.pallas_call(
        _cross_sample_kernel,
        grid=(nb // nseq,),
        in_specs=[pl.BlockSpec((nseq, nh, hd), lambda b: (b, 0, 0)), mem_spec, mem_spec,
                  _resident((SUBLANES, rows))],
        out_specs=pl.BlockSpec((nseq, nh, hd), lambda b: (b, 0, 0)),
        out_shape=jax.ShapeDtypeStruct((nb, nh, hd), f32),
        compiler_params=_params("arbitrary"),
        name="cross_sample",
    )(q, cache_mk, cache_mv, mask)


def _rotary_tables(pos, half):
    inv = 1.0 / (10000.0 ** jnp.linspace(0.0, 1.0, half, dtype=f32))
    ang = pos.astype(f32)[:, None] * inv[None, :]
    cos, sin = jnp.cos(ang), jnp.sin(ang)
    return jnp.concatenate([cos, cos], axis=-1), jnp.concatenate([-sin, sin], axis=-1)


def kernel(x_prompt, x_sample, state_ret, state_pool, cache_k_diff, cache_v_diff, cache_mem_k, cache_mem_v, page_table, mem_prompt, norms, final_norm, ffn_w_gate, ffn_w_up, ffn_w_down, w_in_even, ret_gain, pool_w, pool_scale, w_out_even, w_qkv_odd, lambda_q1, lambda_k1, lambda_q2, lambda_k2, subln_gain, w_out_odd, rel_bias, w_cq, w_ckv, w_co):
    batch, seq, d = x_prompt.shape
    nb = x_sample.shape[0]
    depth = norms.shape[0]
    past = page_table.shape[1] * PAGE_SIZE
    mem_len = mem_prompt.shape[1]
    rw = RET_HEADS * RET_DK
    dw = DIFF_HEADS * DIFF_HD

    xp = x_prompt.reshape(batch * seq, d)
    xs = x_sample.reshape(nb, d)
    mem2d = mem_prompt.reshape(batch * mem_len, d)

    cos_p, sin_p = _rotary_tables(jnp.arange(seq, dtype=i32), RET_DK // 2)
    cos_s, sin_s = _rotary_tables(jnp.full((1,), past, dtype=i32), RET_DK // 2)

    wg, wu, wd = ffn_w_gate.astype(bf16), ffn_w_up.astype(bf16), ffn_w_down.astype(bf16)
    w_in, w_oe = w_in_even.astype(bf16), w_out_even.astype(bf16)
    w_qkv, w_oo = w_qkv_odd.astype(bf16), w_out_odd.astype(bf16)
    wcq, wckv, wco = w_cq.astype(bf16), w_ckv.astype(bf16), w_co.astype(bf16)
    pw = pool_w.astype(bf16)

    p_ret, p_pool, p_k, p_v, p_mk, p_mv = [], [], [], [], [], []
    s_ret, s_pool, s_k, s_v = [], [], [], []
    for l in range(depth):
        last = l == depth - 1
        ffn_w = lambda i: ((wg, (l, i)), (wu, (l, i)), (wd, (l, i)))
        pending = None
        xp = _ffn(xp, norms[l, 0], *ffn_w(0))
        xs = _ffn(xs, norms[l, 0], *ffn_w(0))
        if l % 2 == 0:
            e = l // 2
            zw = w_in.shape[2]
            (zp,) = _proj(xp, norms[l, 1], (w_in, (e,)), [(0, zw, [(0, 1.0)])], [(zw, f32)])
            (zs,) = _proj(xs, norms[l, 1], (w_in, (e,)), [(0, zw, [(0, 1.0)])], [(zw, f32)])
            xp, rp, pp = _even_prompt(xp, zp, cos_p, sin_p, ret_gain[e], pw[e], pool_scale[e], (w_oe, (e,)),
                                      batch, seq)
            mixs, rs, ps = _even_sample(zs, cos_s, sin_s, state_ret[e:e + 1], state_pool[e:e + 1], ret_gain[e],
                                        pool_w[e], pool_scale[e], past)
            xs = _mmres(xs, mixs.reshape(nb, rw + pool_scale.shape[1]), (w_oe, (e,)))
            p_ret.append(rp[0])
            p_pool.append(pp[0])
            s_ret.append(rs[0])
            s_pool.append(ps[0])
        else:
            o = l // 2
            lam_init = 0.8 - 0.6 * math.exp(-0.3 * l)
            scale = DIFF_D ** -0.5
            tiles, tab, lam_t = _bias_tables(rel_bias, lambda_q1[o], lambda_k1[o], lambda_q2[o], lambda_k2[o],
                                             lam_init, ATT_TILE)
            lam = lam_t[0, 0:1]
            tab = tab[:, 0, :]
            far = tab[:, REL_MAX_DIST]
            qkv_groups = [(0, dw, [(0, scale * LOG2E)]), (dw, dw, [(1, 1.0), (3, 1.0)]),
                          (2 * dw, dw, [(2, 1.0), (4, 1.0)])]
            qb, kp, vp, kb, vtb = _proj(xp, norms[l, 1], (w_qkv, (o,)), qkv_groups,
                                        [(dw, bf16), (dw, f32), (dw, f32), (dw, bf16), (dw, bf16, "T")], seq=seq)
            qs, ks_, vs = _proj(xs, norms[l, 1], (w_qkv, (o,)),
                                [(0, dw, [(0, scale)]), (dw, dw, [(1, 1.0)]), (2 * dw, dw, [(2, 1.0)])],
                                [(dw, f32), (dw, f32), (dw, f32)])
            nh = DIFF_HEADS
            pages_per_step = page_table.shape[1] // (seq // ATT_TILE)
            own = jnp.arange(nh)[:, None, None] == jnp.arange(nh)[None, None, :]
            far_page = jnp.where(own, far[:, None, None], NEG_INF) + jnp.zeros((1, PAGE_SIZE, 1), f32)
            last_page = jnp.where(own, tab[:, PAGE_SIZE:0:-1][:, :, None], NEG_INF)
            far_page, last_page = far_page.reshape(nh, -1), last_page.reshape(nh, -1)
            far_step = jnp.tile(far_page, (2, pages_per_step))
            last_step = jnp.concatenate([far_step[:, :-PAGE_SIZE * nh], jnp.tile(last_page, (2, 1))], axis=1)
            bias2 = jnp.stack([far_step, last_step])
            b0 = jnp.tile(tab[:, 0], 2)[:, None]
            heads = lambda a: a.reshape(nb, nh, DIFF_HD)
            attp, atts = _attn(qb, kb, vtb, tiles, far, lam, subln_gain[o], batch, seq,
                               heads(qs), heads(ks_), heads(vs), cache_k_diff, cache_v_diff, o, page_table,
                               bias2, b0, lam_init)
            pending = (attp, (w_oo, (o,)))
            xs = _mmres(xs, atts.reshape(nb, dw), (w_oo, (o,)))
            p_k.append(kp.reshape(batch, seq, DIFF_HEADS, DIFF_HD))
            p_v.append(vp.reshape(batch, seq, DIFF_HEADS, DIFF_HD))
            s_k.append(ks_.reshape(nb, 1, DIFF_HEADS, DIFF_HD))
            s_v.append(vs.reshape(nb, 1, DIFF_HEADS, DIFF_HD))
        mk, mv, mkb, mvb = _proj(mem2d, None, (wckv, (l,)),
                                 [(0, d, [(0, 1.0), (2, 1.0)]), (d, d, [(1, 1.0), (3, 1.0)])],
                                 [(d, f32), (d, f32), (d, bf16), (d, bf16)])
        p_mk.append(mk.reshape(batch, mem_len, MEM_HEADS, d // MEM_HEADS))
        p_mv.append(mv.reshape(batch, mem_len, MEM_HEADS, d // MEM_HEADS))
        xp = _cross_prompt(xp, norms[l, 2], (wcq, (l,)), mkb, mvb, (wco, (l,)), seq, mixer=pending)
        (qcs,) = _proj(xs, norms[l, 2], (wcq, (l,)), [(0, d, [(0, 1.0)])], [(d, f32)])
        acs = _cross_sample(qcs.reshape(nb, MEM_HEADS, d // MEM_HEADS), cache_mem_k, cache_mem_v, l)
        xs = _mmres(xs, acs.reshape(nb, d), (wco, (l,)))
        fin = final_norm if last else None
        xp = _ffn(xp, norms[l, 3], *ffn_w(1), fin)
        xs = _ffn(xs, norms[l, 3], *ffn_w(1), fin)

    return (xp.reshape(batch, seq, d), xs.reshape(nb, 1, d),
            jnp.stack(p_ret), jnp.stack(p_pool), jnp.stack(p_k), jnp.stack(p_v), jnp.stack(p_mk), jnp.stack(p_mv),
            jnp.stack(s_ret), jnp.stack(s_pool), jnp.stack(s_k), jnp.stack(s_v))
```
